```python
import math
import jax
import jax.numpy as jnp
from jax import lax
import numpy as np

D_MODEL = 2048
BATCH = 4
SEQ = 8192
DEPTH = 4
DEC_BATCH = 8
DEC_SEQ = 64
PAST_LEN = 4096

CHUNK = 64
HGRN_BLOCK = 16
A_HEADS = 8
A_KEY = 128
A_VAL = 128
B_QK_HEADS = 4
B_V_HEADS = 8
B_KEY = 128
B_VAL = 128
CONV_W = 4
C_HEADS = 8
C_KEY = 128
C_VAL = 128
ROPE_BASE = 10000.0
N_BRANCH = 3
BRANCH_W = A_HEADS * A_VAL
B_CONV_CH = 2 * B_QK_HEADS * B_KEY + B_V_HEADS * B_VAL
SPLIT_SIZES = (A_HEADS * A_KEY, A_HEADS * A_KEY, A_HEADS * A_VAL, A_HEADS * A_VAL,
               B_QK_HEADS * B_KEY, B_QK_HEADS * B_KEY, B_V_HEADS * B_VAL, B_V_HEADS * B_VAL,
               B_V_HEADS, B_V_HEADS,
               C_HEADS * C_KEY, C_HEADS * C_KEY, C_HEADS * C_VAL, C_HEADS * C_VAL,
               N_BRANCH * D_MODEL)
IN_COLS = sum(SPLIT_SIZES)
N_EXPERTS = 64
N_GROUPS = 8
TOPK_GROUPS = 4
TOP_K = 6
EXPERT_FF = 512
SHARED_FF = 512
ROUTE_SCALE = 2.5
MOE_BLOCK = 256
PLE_DIM = 256
DN_ALPHA = (2.0 * DEPTH) ** 0.25
DN_BETA = (8.0 * DEPTH) ** -0.25
LN_EPS = 1e-5
NORM_EPS = 1e-6
F32 = jnp.float32

kernel_name = 'hybrid_stream_encoder_step'


def layer_norm(x, g, b):
    xf = x.astype(F32)
    mu = jnp.mean(xf, -1, keepdims=True)
    var = jnp.mean(jnp.square(xf - mu), -1, keepdims=True)
    return ((xf - mu) * lax.rsqrt(var + LN_EPS) * g + b).astype(x.dtype)


def rms_norm(x, g):
    xf = x.astype(F32)
    return (xf * lax.rsqrt(jnp.mean(xf * xf, -1, keepdims=True) + NORM_EPS) * g).astype(x.dtype)


def head_group_norm(x, g):
    xf = x.astype(F32)
    mu = jnp.mean(xf, -1, keepdims=True)
    var = jnp.mean(jnp.square(xf - mu), -1, keepdims=True)
    return ((xf - mu) * lax.rsqrt(var + NORM_EPS) * g).astype(x.dtype)


def l2_normalize(x):
    xf = x.astype(F32)
    return (xf * lax.rsqrt(jnp.sum(xf * xf, -1, keepdims=True) + NORM_EPS)).astype(x.dtype)


def rotary(x, pos):
    half = x.shape[-1] // 2
    inv = ROPE_BASE ** (-jnp.arange(half, dtype=F32) / half)
    ang = pos[:, None] * inv[None, :]
    cos = jnp.cos(ang)[None, :, None, :]
    sin = jnp.sin(ang)[None, :, None, :]
    xf = x.astype(F32)
    x1, x2 = xf[..., :half], xf[..., half:]
    return jnp.concatenate([x1 * cos - x2 * sin, x2 * cos + x1 * sin], -1).astype(x.dtype)


def causal_conv(u, buf, w):
    t = u.shape[1]
    ext = jnp.concatenate([buf.astype(u.dtype), u], axis=1)
    out = sum(ext[:, j:j + t] * w[j] for j in range(CONV_W))
    return out, ext[:, t:]


def to_blocks(x, lc):
    b, t, h, d = x.shape
    return x.reshape(b, t // lc, lc, h, d).transpose(1, 0, 3, 2, 4)


def from_blocks(y):
    n, b, h, lc, d = y.shape
    return y.transpose(1, 0, 3, 2, 4).reshape(b, n * lc, h, d)


def hgrn2_recurrence(q, k, v, log_f, s0):
    lc = math.gcd(q.shape[1], HGRN_BLOCK)
    tri = jnp.tril(jnp.ones((lc, lc), bool))[:, :, None]

    def step(S, blk):
        qb, kb, vb, gb = (a.astype(F32) for a in blk)
        Sf = S.astype(F32)
        cum = jnp.cumsum(gb, axis=2)
        seg = cum[:, :, :, None, :] - cum[:, :, None, :, :]
        dec = jnp.where(tri, jnp.exp(jnp.where(tri, seg, 0.0)), 0.0)
        att = jnp.einsum('bhtk,bhsk,bhtsk->bhts', qb, kb, dec)
        o = (jnp.einsum('bhtk,bhkv->bhtv', qb * jnp.exp(cum), Sf)
             + jnp.einsum('bhts,bhsv->bhtv', att, vb))
        last = cum[:, :, -1, :]
        s_new = (jnp.exp(last)[..., None] * Sf
                 + jnp.einsum('bhsk,bhsv->bhkv', kb * jnp.exp(last[:, :, None, :] - cum), vb))
        return s_new.astype(S.dtype), o.astype(q.dtype)

    s_fin, o = lax.scan(step, s0, tuple(to_blocks(a, lc) for a in (q, k, v, log_f)))
    return from_blocks(o), s_fin


def gated_delta_recurrence(q, k, v, g, beta, s0):
    lc = math.gcd(q.shape[1], CHUNK)
    dv = v.shape[-1]
    tri = jnp.tril(jnp.ones((lc, lc), bool))
    strict = jnp.tril(jnp.ones((lc, lc), bool), -1)
    eye = jnp.eye(lc, dtype=F32)

    def step(S, blk):
        qb, kb, vb, gb, bb = (a.astype(F32) for a in blk)
        Sf = S.astype(F32)
        cum = jnp.cumsum(gb[..., 0], axis=-1)
        seg = cum[..., :, None] - cum[..., None, :]
        dec = jnp.where(tri, jnp.exp(jnp.where(tri, seg, 0.0)), 0.0)
        kbeta = kb * bb
        a_low = jnp.where(strict, jnp.einsum('bhtk,bhsk->bhts', kbeta, kb) * dec, 0.0)
        rhs = jnp.concatenate([vb * bb, kbeta * jnp.exp(cum)[..., None]], axis=-1)
        sol = lax.linalg.triangular_solve(a_low + eye, rhs, left_side=True, lower=True,
                                          unit_diagonal=True)
        u = sol[..., :dv] - jnp.einsum('bhtk,bhkv->bhtv', sol[..., dv:], Sf)
        att = jnp.einsum('bhtk,bhsk->bhts', qb, kb) * dec
        o = (jnp.einsum('bhtk,bhkv->bhtv', qb * jnp.exp(cum)[..., None], Sf)
             + jnp.einsum('bhts,bhsv->bhtv', att, u))
        last = cum[..., -1:]
        s_new = (jnp.exp(last)[..., None] * Sf
                 + jnp.einsum('bhsk,bhsv->bhkv', kb * jnp.exp(last - cum)[..., None], u))
        return s_new.astype(S.dtype), o.astype(q.dtype)

    blocks = (to_blocks(q, lc), to_blocks(k, lc), to_blocks(v, lc),
              to_blocks(g[..., None], lc), to_blocks(beta[..., None], lc))
    s_fin, o = lax.scan(step, s0, blocks)
    return from_blocks(o), s_fin


def retention_recurrence(q, k, v, s0):
    h = q.shape[2]
    lc = math.gcd(q.shape[1], CHUNK)
    lg = jnp.log1p(-jnp.exp2(-5.0 - jnp.arange(h, dtype=F32)))
    idx = jnp.arange(lc, dtype=F32)
    rel = idx[:, None] - idx[None, :]
    intra = jnp.where(rel >= 0, jnp.exp(rel * lg[:, None, None]), 0.0)
    q_dec = jnp.exp((idx + 1.0) * lg[:, None])[..., None]
    k_dec = jnp.exp((lc - 1.0 - idx) * lg[:, None])[..., None]
    c_dec = jnp.exp(lc * lg)[:, None, None]

    def step(S, blk):
        qb, kb, vb = (a.astype(F32) for a in blk)
        Sf = S.astype(F32)
        att = jnp.einsum('bhtk,bhsk->bhts', qb, kb) * intra
        o = (jnp.einsum('bhtk,bhkv->bhtv', qb, Sf) * q_dec
             + jnp.einsum('bhts,bhsv->bhtv', att, vb))
        s_new = c_dec * Sf + jnp.einsum('bhsk,bhsv->bhkv', kb * k_dec, vb)
        return s_new.astype(S.dtype), o.astype(q.dtype)

    s_fin, o = lax.scan(step, s0, tuple(to_blocks(a, lc) for a in (q, k, v)))
    return from_blocks(o), s_fin


def token_mixers(x, pos, w_in, lb, hgrn_norm, gdn_conv, gdn_a_log, gdn_dt_bias, gdn_norm,
                 ret_norm, w_branch, w_out, s_hgrn, s_gdn, s_conv, s_ret):
    b, t, _ = x.shape
    points = np.cumsum(SPLIT_SIZES)[:-1].tolist()
    (a_q, a_f, a_i, a_g, b_q, b_k, b_v, b_z, b_b, b_a, c_q, c_k, c_v, c_g, gate_in) = jnp.split(
        jnp.einsum('btd,dc->btc', x, w_in), points, axis=-1)

    def heads(u, n):
        return u.reshape(b, t, n, -1)

    z_f = heads(a_f, A_HEADS).astype(F32)
    lb_h = lb.reshape(A_HEADS, A_KEY).astype(F32)
    log_f = jnp.logaddexp(jnp.log(lb_h), jnp.log1p(-lb_h) + jax.nn.log_sigmoid(z_f))
    key_a = ((1.0 - lb_h) * jax.nn.sigmoid(-z_f)).astype(x.dtype)
    o_a, s_hgrn_new = hgrn2_recurrence(heads(a_q, A_HEADS), key_a, heads(a_i, A_HEADS), log_f, s_hgrn)
    o_a = rms_norm(o_a, hgrn_norm) * jax.nn.silu(heads(a_g, A_HEADS))

    qkv, s_conv_new = causal_conv(jnp.concatenate([b_q, b_k, b_v], -1), s_conv, gdn_conv)
    qkv = jax.nn.silu(qkv)
    nq = B_QK_HEADS * B_KEY
    rep = B_V_HEADS // B_QK_HEADS
    q_b = jnp.repeat(l2_normalize(heads(qkv[..., :nq], B_QK_HEADS)) * (B_KEY ** -0.5), rep, axis=2)
    k_b = jnp.repeat(l2_normalize(heads(qkv[..., nq:2 * nq], B_QK_HEADS)), rep, axis=2)
    v_b = heads(qkv[..., 2 * nq:], B_V_HEADS)
    beta = jax.nn.sigmoid(b_b.astype(F32))
    g = -jnp.exp(gdn_a_log.astype(F32)) * jax.nn.softplus(b_a.astype(F32) + gdn_dt_bias.astype(F32))
    o_b, s_gdn_new = gated_delta_recurrence(q_b, k_b, v_b, g, beta, s_gdn)
    o_b = rms_norm(o_b, gdn_norm) * jax.nn.silu(heads(b_z, B_V_HEADS))

    q_c = rotary(heads(c_q, C_HEADS), pos)
    k_c = rotary(heads(c_k, C_HEADS), pos) * (C_KEY ** -0.5)
    o_c, s_ret_new = retention_recurrence(q_c, k_c, heads(c_v, C_HEADS), s_ret)
    o_c = head_group_norm(o_c, ret_norm) * jax.nn.silu(heads(c_g, C_HEADS))

    branches = jnp.stack([o.reshape(b, t, BRANCH_W) for o in (o_a, o_b, o_c)], axis=2)
    branch_out = jnp.einsum('btnc,ncd->btnd', branches, w_branch)
    gates = jax.nn.sigmoid(gate_in.reshape(b, t, N_BRANCH, D_MODEL))
    merged = jnp.sum(gates * branch_out, axis=2)
    return jnp.einsum('btd,de->bte', merged, w_out), (s_hgrn_new, s_gdn_new, s_conv_new, s_ret_new)


def routed_experts(x2, topi, topw, w_gate, w_up, w_down):
    t, d = x2.shape
    n_assign = t * TOP_K
    flat_e = topi.reshape(n_assign)
    flat_w = topw.reshape(n_assign)
    order = jnp.argsort(flat_e)
    se = flat_e[order]
    counts = jnp.zeros((N_EXPERTS,), jnp.int32).at[flat_e].add(1)
    padded = (counts + MOE_BLOCK - 1) // MOE_BLOCK * MOE_BLOCK
    pad_end = jnp.cumsum(padded)
    pad_start = pad_end - padded
    start = jnp.cumsum(counts) - counts
    dest = pad_start[se] + jnp.arange(n_assign, dtype=jnp.int32) - start[se]
    n_blocks = -(-n_assign // MOE_BLOCK) + N_EXPERTS
    n_slots = n_blocks * MOE_BLOCK
    slot_tok = jnp.full((n_slots,), t, jnp.int32).at[dest].set((order // TOP_K).astype(jnp.int32))
    slot_w = jnp.zeros((n_slots,), x2.dtype).at[dest].set(flat_w[order])
    block_e = jnp.minimum(jnp.searchsorted(pad_end, jnp.arange(n_blocks) * MOE_BLOCK, side='right'),
                          N_EXPERTS - 1)
    x_pad = jnp.concatenate([x2, jnp.zeros((1, d), x2.dtype)], axis=0)

    def body(acc, blk):
        tok, wt, e = blk
        xb = x_pad[tok]
        hb = jax.nn.silu(xb @ w_gate[e]) * (xb @ w_up[e])
        return acc.at[tok].add(((hb @ w_down[e]) * wt[:, None]).astype(acc.dtype)), None

    acc, _ = lax.scan(body, jnp.zeros((t + 1, d), x2.dtype),
                      (slot_tok.reshape(n_blocks, MOE_BLOCK), slot_w.reshape(n_blocks, MOE_BLOCK), block_e))
    return acc[:t]


def moe_ffn(x2, w_router, router_bias, w_exp_gate, w_exp_up, w_exp_down, w_sh_gate, w_sh_up, w_sh_down):
    t = x2.shape[0]
    per_group = N_EXPERTS // N_GROUPS
    scores = jax.nn.sigmoid(x2.astype(F32) @ w_router.astype(F32))
    choice = scores + router_bias.astype(F32)
    group_score = jnp.sum(lax.top_k(choice.reshape(t, N_GROUPS, per_group), 2)[0], -1)
    gidx = lax.top_k(group_score, TOPK_GROUPS)[1]
    gsel = jnp.any(gidx[:, :, None] == jnp.arange(N_GROUPS)[None, None, :], axis=1)
    masked = jnp.where(jnp.repeat(gsel, per_group, axis=1), choice, -jnp.inf)
    topi = lax.top_k(masked, TOP_K)[1]
    topw = jnp.take_along_axis(scores, topi, axis=1)
    topw = topw / jnp.sum(topw, -1, keepdims=True) * ROUTE_SCALE
    routed = routed_experts(x2, topi, topw.astype(x2.dtype), w_exp_gate, w_exp_up, w_exp_down)
    shared = (jax.nn.silu(x2 @ w_sh_gate) * (x2 @ w_sh_up)) @ w_sh_down
    return routed + shared


def run_trunk(x, p, pos, states, weights):
    (w_in, hgrn_lb_logits, hgrn_norm, gdn_conv, gdn_a_log, gdn_dt_bias, gdn_norm, ret_norm,
     w_branch, w_out, ln1_g, ln1_b, ln2_g, ln2_b, w_router, router_bias,
     w_exp_gate, w_exp_up, w_exp_down, w_sh_gate, w_sh_up, w_sh_down, w_ple_gate, w_ple_proj) = weights
    s_hgrn, s_gdn, s_conv, s_ret = states
    lb_p = jax.nn.softmax(hgrn_lb_logits.astype(F32), axis=0)
    lb_all = jnp.maximum(jnp.cumsum(lb_p, axis=0) - lb_p[:1], 0.0)
    b, t, d = x.shape
    new = ([], [], [], [])
    for i in range(DEPTH):
        h, layer_states = token_mixers(x, pos, w_in[i], lb_all[i], hgrn_norm[i], gdn_conv[i], gdn_a_log[i],
                                       gdn_dt_bias[i], gdn_norm[i], ret_norm[i], w_branch[i], w_out[i],
                                       s_hgrn[i], s_gdn[i], s_conv[i], s_ret[i])
        x = layer_norm(DN_ALPHA * x + h, ln1_g[i], ln1_b[i])
        h = moe_ffn(x.reshape(b * t, d), w_router[i], router_bias[i], w_exp_gate[i], w_exp_up[i],
                    w_exp_down[i], w_sh_gate[i], w_sh_up[i], w_sh_down[i]).reshape(b, t, d)
        x = layer_norm(DN_ALPHA * x + h, ln2_g[i], ln2_b[i])
        gate = jax.nn.sigmoid(jnp.einsum('btd,de->bte', x, w_ple_gate[i]))
        x = x + jnp.einsum('btp,pd->btd', p[i], w_ple_proj[i]) * gate
        for acc, s in zip(new, layer_states):
            acc.append(s)
    return x, tuple(jnp.stack(acc) for acc in new)


def setup_inputs(seed: int = 0) -> dict:
    key = jax.random.key(seed)
    ks = iter(jax.random.split(key, 48))

    def nrm(shape, scale):
        return jax.random.normal(next(ks), shape, F32) * scale

    dt = jnp.exp(jax.random.uniform(next(ks), (DEPTH, B_V_HEADS), F32, math.log(1e-3), math.log(1e-1)))
    return {
        'x_prompt': nrm((BATCH, SEQ, D_MODEL), 1.0),
        'x_sample': nrm((DEC_BATCH, DEC_SEQ, D_MODEL), 1.0),
        'p_prompt': nrm((DEPTH, BATCH, SEQ, PLE_DIM), 1.0),
        'p_sample': nrm((DEPTH, DEC_BATCH, DEC_SEQ, PLE_DIM), 1.0),
        'state_hgrn': nrm((DEPTH, DEC_BATCH, A_HEADS, A_KEY, A_VAL), 0.5),
        'state_gdn': nrm((DEPTH, DEC_BATCH, B_V_HEADS, B_KEY, B_VAL), 0.5),
        'state_gdn_conv': nrm((DEPTH, DEC_BATCH, CONV_W - 1, B_CONV_CH), 1.0),
        'state_ret': nrm((DEPTH, DEC_BATCH, C_HEADS, C_KEY, C_VAL), 1.0),
        'w_in': nrm((DEPTH, D_MODEL, IN_COLS), D_MODEL ** -0.5),
        'hgrn_lb_logits': nrm((DEPTH, A_HEADS * A_KEY), 0.5),
        'hgrn_norm': 1.0 + nrm((DEPTH, A_VAL), 0.02),
        'gdn_conv': nrm((DEPTH, CONV_W, B_CONV_CH), CONV_W ** -0.5),
        'gdn_a_log': jnp.log(jax.random.uniform(next(ks), (DEPTH, B_V_HEADS), F32, 1.0, 16.0)),
        'gdn_dt_bias': dt + jnp.log(-jnp.expm1(-dt)),
        'gdn_norm': 1.0 + nrm((DEPTH, B_VAL), 0.02),
        'ret_norm': 1.0 + nrm((DEPTH, C_VAL), 0.02),
        'w_branch': nrm((DEPTH, N_BRANCH, BRANCH_W, D_MODEL), BRANCH_W ** -0.5),
        'w_out': nrm((DEPTH, D_MODEL, D_MODEL), D_MODEL ** -0.5 * DN_BETA),
        'ln1_g': 1.0 + nrm((DEPTH, D_MODEL), 0.02),
        'ln1_b': nrm((DEPTH, D_MODEL), 0.02),
        'ln2_g': 1.0 + nrm((DEPTH, D_MODEL), 0.02),
        'ln2_b': nrm((DEPTH, D_MODEL), 0.02),
        'w_router': nrm((DEPTH, D_MODEL, N_EXPERTS), D_MODEL ** -0.5),
        'router_bias': nrm((DEPTH, N_EXPERTS), 0.01),
        'w_exp_gate': nrm((DEPTH, N_EXPERTS, D_MODEL, EXPERT_FF), D_MODEL ** -0.5),
        'w_exp_up': nrm((DEPTH, N_EXPERTS, D_MODEL, EXPERT_FF), D_MODEL ** -0.5),
        'w_exp_down': nrm((DEPTH, N_EXPERTS, EXPERT_FF, D_MODEL), EXPERT_FF ** -0.5 * DN_BETA),
        'w_sh_gate': nrm((DEPTH, D_MODEL, SHARED_FF), D_MODEL ** -0.5),
        'w_sh_up': nrm((DEPTH, D_MODEL, SHARED_FF), D_MODEL ** -0.5),
        'w_sh_down': nrm((DEPTH, SHARED_FF, D_MODEL), SHARED_FF ** -0.5 * DN_BETA),
        'w_ple_gate': nrm((DEPTH, D_MODEL, D_MODEL), D_MODEL ** -0.5),
        'w_ple_proj': nrm((DEPTH, PLE_DIM, D_MODEL), PLE_DIM ** -0.5),
    }


def reference(x_prompt, x_sample, p_prompt, p_sample, state_hgrn, state_gdn, state_gdn_conv, state_ret,
              w_in, hgrn_lb_logits, hgrn_norm, gdn_conv, gdn_a_log, gdn_dt_bias, gdn_norm, ret_norm,
              w_branch, w_out, ln1_g, ln1_b, ln2_g, ln2_b, w_router, router_bias,
              w_exp_gate, w_exp_up, w_exp_down, w_sh_gate, w_sh_up, w_sh_down, w_ple_gate, w_ple_proj):
    weights = (w_in, hgrn_lb_logits, hgrn_norm, gdn_conv, gdn_a_log, gdn_dt_bias, gdn_norm, ret_norm,
               w_branch, w_out, ln1_g, ln1_b, ln2_g, ln2_b, w_router, router_bias,
               w_exp_gate, w_exp_up, w_exp_down, w_sh_gate, w_sh_up, w_sh_down, w_ple_gate, w_ple_proj)
    bp = x_prompt.shape[0]
    dt = x_prompt.dtype
    zero_states = (jnp.zeros((DEPTH, bp, A_HEADS, A_KEY, A_VAL), dt),
                   jnp.zeros((DEPTH, bp, B_V_HEADS, B_KEY, B_VAL), dt),
                   jnp.zeros((DEPTH, bp, CONV_W - 1, B_CONV_CH), dt),
                   jnp.zeros((DEPTH, bp, C_HEADS, C_KEY, C_VAL), dt))
    pos_prompt = jnp.arange(x_prompt.shape[1], dtype=F32)
    pos_sample = PAST_LEN + jnp.arange(x_sample.shape[1], dtype=F32)
    y_prompt, (hgrn_p, gdn_p, conv_p, ret_p) = run_trunk(x_prompt, p_prompt, pos_prompt, zero_states, weights)
    y_sample, (hgrn_s, gdn_s, conv_s, ret_s) = run_trunk(
        x_sample, p_sample, pos_sample, (state_hgrn, state_gdn, state_gdn_conv, state_ret), weights)
    return (y_prompt, y_sample, hgrn_p, gdn_p, conv_p, ret_p, hgrn_s, gdn_s, conv_s, ret_s)
```

```python
import functools
import math

import jax
import jax.numpy as jnp
import numpy as np
from jax import lax
from jax.experimental import pallas as pl
from jax.experimental.pallas import tpu as pltpu

F32 = jnp.float32
BF16 = jnp.bfloat16

D_MODEL = 2048
DEPTH = 4
PAST_LEN = 4096
HEAD = 128
N_HEADS = 8
B_QK_HEADS = 4
CONV_W = 4
BRANCH_W = N_HEADS * HEAD
B_CONV_CH = 2 * B_QK_HEADS * HEAD + N_HEADS * HEAD
ROPE_BASE = 10000.0
N_BRANCH = 3
N_EXPERTS = 64
N_GROUPS = 8
TOPK_GROUPS = 4
TOP_K = 6
EXPERT_FF = 512
ROUTE_SCALE = 2.5
MOE_BLOCK = 256
PLE_DIM = 256
DN_ALPHA = (2.0 * DEPTH) ** 0.25
LN_EPS = 1e-5
NORM_EPS = 1e-6

HGRN_SUB = 16
GDN_CHUNK = 64
ACG_COLS = 4 * BRANCH_W + 4 * BRANCH_W + N_BRANCH * D_MODEL
GATE_COL0 = 8 * BRANCH_W
B_COLS = 3200
B_SMALL_COL = 3072

VMEM_LIMIT = 56 * 1024 * 1024


def _params(*sem):
    return pltpu.CompilerParams(dimension_semantics=sem, vmem_limit_bytes=VMEM_LIMIT)


def _bdot(a, b):
    return jnp.dot(a.astype(BF16), b.astype(BF16), preferred_element_type=F32)


def _bdot_nt(a, b):
    return lax.dot_general(a.astype(BF16), b.astype(BF16), (((1,), (1,)), ((), ())),
                           preferred_element_type=F32)


def _bdot_tn(a, b):
    return lax.dot_general(a.astype(BF16), b.astype(BF16), (((0,), (0,)), ((), ())),
                           preferred_element_type=F32)


def _split2(x):
    hi = x.astype(BF16)
    lo = (x - hi.astype(F32)).astype(BF16)
    return hi, lo


def _split3(x):
    hi = x.astype(BF16)
    r = x - hi.astype(F32)
    mid = r.astype(BF16)
    lo = (r - mid.astype(F32)).astype(BF16)
    return hi, mid, lo


def _mask_dot(mask_bf, x):
    hi, mid, lo = _split3(x)
    d = functools.partial(jnp.dot, preferred_element_type=F32)
    return d(mask_bf, hi) + d(mask_bf, mid) + d(mask_bf, lo)


def _dot_x3(a, b):
    ah, al = _split2(a)
    bh, bl = _split2(b)
    d = functools.partial(jnp.dot, preferred_element_type=F32)
    return d(ah, bh) + d(ah, bl) + d(al, bh)


def _sigmoid(x):
    return 1.0 / (1.0 + jnp.exp(-x))


def _silu(x):
    return x * _sigmoid(x)


def _softplus(x):
    return jnp.maximum(x, 0.0) + jnp.log1p(jnp.exp(-jnp.abs(x)))


def _log_sigmoid(x):
    return jnp.minimum(x, 0.0) - jnp.log1p(jnp.exp(-jnp.abs(x)))


def _mm_kernel(x_ref, w_ref, o_ref):
    o_ref[...] = jnp.dot(x_ref[...], w_ref[...], preferred_element_type=F32).astype(o_ref.dtype)


def _matmul(x, w, tm, tn):
    m, k = x.shape
    n = w.shape[1]
    return pl.pallas_call(
        _mm_kernel,
        grid=(m // tm, n // tn),
        in_specs=[pl.BlockSpec((tm, k), lambda i, j: (i, 0)),
                  pl.BlockSpec((k, tn), lambda i, j: (0, j))],
        out_specs=pl.BlockSpec((tm, tn), lambda i, j: (i, j)),
        out_shape=jax.ShapeDtypeStruct((m, n), F32),
        compiler_params=_params("parallel", "parallel"),
        name="inproj_matmul",
    )(x, w)


def _hgrn_kernel(q_ref, f_ref, i_ref, g_ref, llb_ref, l1m_ref, oml_ref, nw_ref, s0_ref,
                 o_ref, sout_ref, st_ref, cum_ref, key_ref, oacc_ref, *, tc):
    c = pl.program_id(2)

    @pl.when(c == 0)
    def _():
        st_ref[...] = s0_ref[0, 0]

    z = f_ref[...]
    bterm = l1m_ref[...] + _log_sigmoid(z)
    a = llb_ref[...]
    log_f = jnp.maximum(a, bterm) + jnp.log1p(jnp.exp(-jnp.abs(a - bterm)))
    key_ref[...] = oml_ref[...] * (1.0 / (1.0 + jnp.exp(z)))
    r = lax.broadcasted_iota(jnp.int32, (tc, tc), 0)
    s = lax.broadcasted_iota(jnp.int32, (tc, tc), 1)
    seg_mask = jnp.where((r // HGRN_SUB == s // HGRN_SUB) & (s <= r), 1.0, 0.0).astype(BF16)
    cum_ref[...] = _mask_dot(seg_mask, log_f)

    rows = lax.broadcasted_iota(jnp.int32, (HGRN_SUB, HEAD), 0)
    ones_bf = jnp.ones((HEAD, HEAD), BF16)

    def sub_block(j, carry):
        base = pl.multiple_of(j * HGRN_SUB, HGRN_SUB)
        sl = pl.ds(base, HGRN_SUB)
        qj = q_ref[sl, :]
        kj = key_ref[sl, :]
        vj = i_ref[sl, :]
        cj = cum_ref[sl, :]
        st = st_ref[...]
        parts = []
        for si in range(HGRN_SUB):
            cs = cum_ref[pl.ds(base + si, 1), :]
            ks = key_ref[pl.ds(base + si, 1), :]
            arg = jnp.where(rows >= si, cj - cs, -1e30)
            parts.append(qj * (ks * jnp.exp(arg)))
        pmat = jnp.concatenate(parts, axis=0)
        p_hi, p_lo = _split2(pmat)
        att = (jnp.dot(p_hi, ones_bf, preferred_element_type=F32)
               + jnp.dot(p_lo, ones_bf, preferred_element_type=F32))
        o = _bdot_nt(qj * jnp.exp(cj), st)
        for si in range(HGRN_SUB):
            vs = i_ref[pl.ds(base + si, 1), :]
            o = o + att[si * HGRN_SUB:(si + 1) * HGRN_SUB, :] * vs
        oacc_ref[sl, :] = o
        last = cum_ref[pl.ds(base + HGRN_SUB - 1, 1), :]
        st_ref[...] = jnp.exp(last) * st + _bdot_tn(vj, kj * jnp.exp(last - cj))
        return carry

    lax.fori_loop(0, tc // HGRN_SUB, sub_block, 0)

    o = oacc_ref[...]
    o = o * lax.rsqrt(jnp.mean(o * o, axis=-1, keepdims=True) + NORM_EPS) * nw_ref[...]
    o_ref[...] = (o * _silu(g_ref[...])).astype(o_ref.dtype)

    @pl.when(c == pl.num_programs(2) - 1)
    def _():
        sout_ref[0, 0] = st_ref[...]


def _hgrn(acg, llb, l1m, oml, nw, s0t, b, t):
    tc = min(t, 128)
    nt = t // tc
    row = lambda bi, h, c: bi * nt + c
    col_spec = lambda part: pl.BlockSpec((tc, HEAD), lambda bi, h, c: (row(bi, h, c), part * N_HEADS + h))
    vec_spec = pl.BlockSpec((1, HEAD), lambda bi, h, c: (0, h))
    st_spec = pl.BlockSpec((1, 1, HEAD, HEAD), lambda bi, h, c: (bi, h, 0, 0))
    return pl.pallas_call(
        functools.partial(_hgrn_kernel, tc=tc),
        grid=(b, N_HEADS, nt),
        in_specs=[col_spec(0), col_spec(1), col_spec(2), col_spec(3),
                  vec_spec, vec_spec, vec_spec, pl.BlockSpec((1, HEAD), lambda bi, h, c: (0, 0)), st_spec],
        out_specs=[pl.BlockSpec((tc, HEAD), lambda bi, h, c: (row(bi, h, c), h)), st_spec],
        out_shape=[jax.ShapeDtypeStruct((b * t, BRANCH_W), BF16),
                   jax.ShapeDtypeStruct((b, N_HEADS, HEAD, HEAD), F32)],
        scratch_shapes=[pltpu.VMEM((HEAD, HEAD), F32), pltpu.VMEM((tc, HEAD), F32),
                        pltpu.VMEM((tc, HEAD), F32), pltpu.VMEM((tc, HEAD), F32)],
        compiler_params=_params("parallel", "parallel", "arbitrary"),
        name="hgrn2",
    )(acg, acg, acg, acg, llb, l1m, oml, nw, s0t)


def _conv_kernel(u_ref, w_ref, buf_ref, o_ref, tail_ref, *, tc):
    c = pl.program_id(1)

    @pl.when(c == 0)
    def _():
        tail_ref[...] = buf_ref[0]

    u = u_ref[...]
    ext = jnp.concatenate([tail_ref[...], u], axis=0)
    acc = u * w_ref[CONV_W - 1:CONV_W, :]
    for j in range(CONV_W - 1):
        acc = acc + ext[5 + j:5 + j + tc, :] * w_ref[j:j + 1, :]
    tail_ref[...] = u[tc - 8:, :]
    y = _silu(acc)
    nqk = B_QK_HEADS * HEAD
    for hh in range(2 * B_QK_HEADS):
        blk = y[:, hh * HEAD:(hh + 1) * HEAD]
        nrm = blk * lax.rsqrt(jnp.sum(blk * blk, axis=-1, keepdims=True) + NORM_EPS)
        if hh < B_QK_HEADS:
            nrm = nrm * (HEAD ** -0.5)
        o_ref[:, hh * HEAD:(hh + 1) * HEAD] = nrm
    o_ref[:, 2 * nqk:] = y[:, 2 * nqk:]


def _gdn_conv(yb, w, buf8, b, t):
    tc = min(t, 256)
    nt = t // tc
    return pl.pallas_call(
        functools.partial(_conv_kernel, tc=tc),
        grid=(b, nt),
        in_specs=[pl.BlockSpec((tc, B_CONV_CH), lambda bi, c: (bi * nt + c, 0)),
                  pl.BlockSpec((CONV_W, B_CONV_CH), lambda bi, c: (0, 0)),
                  pl.BlockSpec((1, 8, B_CONV_CH), lambda bi, c: (bi, 0, 0))],
        out_specs=pl.BlockSpec((tc, B_CONV_CH), lambda bi, c: (bi * nt + c, 0)),
        out_shape=jax.ShapeDtypeStruct((b * t, B_CONV_CH), F32),
        scratch_shapes=[pltpu.VMEM((8, B_CONV_CH), F32)],
        compiler_params=_params("parallel", "arbitrary"),
        name="gdn_conv",
    )(yb, w, buf8)


def _gdn_kernel(qkv_ref, z_ref, sm_ref, alog_ref, dtb_ref, nw_ref, s0_ref, o_ref, sout_ref, st_ref):
    c = pl.program_id(1)
    cs = GDN_CHUNK

    @pl.when(c == 0)
    def _():
        st_ref[...] = s0_ref[0]

    sm = sm_ref[...]
    beta_all = _sigmoid(sm)
    g_all = -jnp.exp(alog_ref[...]) * _softplus(sm + dtb_ref[...])

    r = lax.broadcasted_iota(jnp.int32, (cs, cs), 0)
    s = lax.broadcasted_iota(jnp.int32, (cs, cs), 1)
    lower = r >= s
    strict = r > s
    tril_bf = jnp.where(lower, 1.0, 0.0).astype(BF16)
    r2 = lax.broadcasted_iota(jnp.int32, (2 * cs, cs), 0)
    s2 = lax.broadcasted_iota(jnp.int32, (2 * cs, cs), 1)
    both_bf = jnp.where(((r2 < cs) & (s2 <= r2)) | ((r2 >= cs) & (s2 > r2 - cs)), 1.0, 0.0).astype(BF16)
    nw = nw_ref[...]

    for qh in range(B_QK_HEADS):
        q = qkv_ref[:, qh * HEAD:(qh + 1) * HEAD]
        k = qkv_ref[:, (B_QK_HEADS + qh) * HEAD:(B_QK_HEADS + qh + 1) * HEAD]
        kk = _bdot_nt(k, k)
        qk = _bdot_nt(q, k)
        for rep in range(N_HEADS // B_QK_HEADS):
            h = qh * (N_HEADS // B_QK_HEADS) + rep
            v = qkv_ref[:, (2 * B_QK_HEADS + h) * HEAD:(2 * B_QK_HEADS + h + 1) * HEAD]
            beta = beta_all[:, h:h + 1]
            g = g_all[:, N_HEADS + h:N_HEADS + h + 1]
            gb = jnp.broadcast_to(g, (cs, HEAD))
            cr = _mask_dot(both_bf, gb)
            cum = cr[:cs]
            rem = cr[cs:]
            seg = _mask_dot(tril_bf, jnp.where(strict, gb[:, :cs], 0.0))
            dec = jnp.where(lower, jnp.exp(seg), 0.0)
            st = st_ref[h]
            kbeta = k * beta
            ecum = jnp.exp(cum)
            x = jnp.where(strict, -(kk * beta) * dec, 0.0)
            y = jnp.concatenate([v * beta, kbeta * ecum], axis=1)
            n_lvl = int(math.log2(cs))
            for lvl in range(n_lvl):
                y = y + _dot_x3(x, y)
                if lvl + 1 < n_lvl:
                    x = _dot_x3(x, x)
            u = y[:, :HEAD] - _bdot(y[:, HEAD:], st)
            att = qk * dec
            o = _bdot(q * ecum, st) + _bdot(att, u)
            elast = jnp.exp(cum[cs - 1:cs, :])
            st_ref[h] = elast * st + _bdot_tn(k * jnp.exp(rem), u)
            o = o * lax.rsqrt(jnp.mean(o * o, axis=-1, keepdims=True) + NORM_EPS) * nw
            zg = z_ref[:, h * HEAD:(h + 1) * HEAD]
            o_ref[:, h * HEAD:(h + 1) * HEAD] = (o * _silu(zg)).astype(o_ref.dtype)

    @pl.when(c == pl.num_programs(1) - 1)
    def _():
        sout_ref[0] = st_ref[...]


def _gdn(qkvc, yb, alog_row, dtb_row, nw, s0, b, t):
    cs = GDN_CHUNK
    nt = t // cs
    st_spec = pl.BlockSpec((1, N_HEADS, HEAD, HEAD), lambda bi, c: (bi, 0, 0, 0))
    row_spec = pl.BlockSpec((1, HEAD), lambda bi, c: (0, 0))
    return pl.pallas_call(
        _gdn_kernel,
        grid=(b, nt),
        in_specs=[pl.BlockSpec((cs, B_CONV_CH), lambda bi, c: (bi * nt + c, 0)),
                  pl.BlockSpec((cs, BRANCH_W), lambda bi, c: (bi * nt + c, B_CONV_CH // BRANCH_W)),
                  pl.BlockSpec((cs, HEAD), lambda bi, c: (bi * nt + c, B_SMALL_COL // HEAD)),
                  row_spec, row_spec, row_spec, st_spec],
        out_specs=[pl.BlockSpec((cs, BRANCH_W), lambda bi, c: (bi * nt + c, 0)), st_spec],
        out_shape=[jax.ShapeDtypeStruct((b * t, BRANCH_W), BF16),
                   jax.ShapeDtypeStruct((b, N_HEADS, HEAD, HEAD), F32)],
        scratch_shapes=[pltpu.VMEM((N_HEADS, HEAD, HEAD), F32)],
        compiler_params=_params("parallel", "arbitrary"),
        name="gated_delta",
    )(qkvc, yb, yb, alog_row, dtb_row, nw, s0)


def _ret_kernel(lg_ref, q_ref, k_ref, v_ref, g_ref, cos_ref, sin_ref, nw_ref, s0_ref,
                o_ref, sout_ref, st_ref, *, cs):
    h = pl.program_id(1)
    c = pl.program_id(2)

    @pl.when(c == 0)
    def _():
        st_ref[...] = s0_ref[0, 0]

    lg = lg_ref[h]
    cos = cos_ref[...]
    sin = sin_ref[...]

    def rot(x):
        return x * cos + pltpu.roll(x, HEAD // 2, 1) * sin

    q = rot(q_ref[...])
    k = rot(k_ref[...]) * (HEAD ** -0.5)
    v = v_ref[...]
    idx = lax.broadcasted_iota(jnp.int32, (cs, HEAD), 0).astype(F32)
    q_dec = jnp.exp((idx + 1.0) * lg)
    k_dec = jnp.exp((cs - 1.0 - idx) * lg)
    c_dec = jnp.exp(jnp.zeros((1, HEAD), F32) + cs * lg)
    rel = (lax.broadcasted_iota(jnp.int32, (cs, cs), 0)
           - lax.broadcasted_iota(jnp.int32, (cs, cs), 1))
    intra = jnp.where(rel >= 0, jnp.exp(jnp.maximum(rel, 0).astype(F32) * lg), 0.0)
    st = st_ref[...]
    att = _bdot_nt(q, k) * intra
    o = _bdot(q, st) * q_dec + _bdot(att, v)
    st_ref[...] = c_dec * st + _bdot_tn(k * k_dec, v)
    mu = jnp.mean(o, axis=-1, keepdims=True)
    var = jnp.mean(jnp.square(o - mu), axis=-1, keepdims=True)
    o = (o - mu) * lax.rsqrt(var + NORM_EPS) * nw_ref[...]
    o_ref[...] = (o * _silu(g_ref[...])).astype(o_ref.dtype)

    @pl.when(c == pl.num_programs(2) - 1)
    def _():
        sout_ref[0, 0] = st_ref[...]


def _retention(acg, lg, cos, sin, nw, s0, b, t):
    cs = min(t, 256)
    nt = t // cs
    col_spec = lambda part: pl.BlockSpec(
        (cs, HEAD), lambda bi, h, c, lg_ref: (bi * nt + c, (4 + part) * N_HEADS + h))
    tab_spec = pl.BlockSpec((cs, HEAD), lambda bi, h, c, lg_ref: (c, 0))
    st_spec = pl.BlockSpec((1, 1, HEAD, HEAD), lambda bi, h, c, lg_ref: (bi, h, 0, 0))
    return pl.pallas_call(
        functools.partial(_ret_kernel, cs=cs),
        grid_spec=pltpu.PrefetchScalarGridSpec(
            num_scalar_prefetch=1,
            grid=(b, N_HEADS, nt),
            in_specs=[col_spec(0), col_spec(1), col_spec(2), col_spec(3), tab_spec, tab_spec,
                      pl.BlockSpec((1, HEAD), lambda bi, h, c, lg_ref: (0, 0)), st_spec],
            out_specs=[pl.BlockSpec((cs, HEAD), lambda bi, h, c, lg_ref: (bi * nt + c, h)), st_spec],
            scratch_shapes=[pltpu.VMEM((HEAD, HEAD), F32)]),
        out_shape=[jax.ShapeDtypeStruct((b * t, BRANCH_W), BF16),
                   jax.ShapeDtypeStruct((b, N_HEADS, HEAD, HEAD), F32)],
        compiler_params=_params("parallel", "parallel", "arbitrary"),
        name="retention",
    )(lg, acg, acg, acg, acg, cos, sin, nw, s0)


def _merge_kernel(oa_ref, ob_ref, oc_ref, wb_ref, ga_ref, gb_ref, gc_ref, o_ref):
    acc = _sigmoid(ga_ref[...]) * jnp.dot(oa_ref[...], wb_ref[0], preferred_element_type=F32)
    acc += _sigmoid(gb_ref[...]) * jnp.dot(ob_ref[...], wb_ref[1], preferred_element_type=F32)
    acc += _sigmoid(gc_ref[...]) * jnp.dot(oc_ref[...], wb_ref[2], preferred_element_type=F32)
    o_ref[...] = acc.astype(o_ref.dtype)


def _merge(oa, ob, oc, wb, acg, tm):
    n = oa.shape[0]
    tn = 512
    o_spec = pl.BlockSpec((tm, BRANCH_W), lambda i, j: (i, 0))
    gate_spec = lambda br: pl.BlockSpec(
        (tm, tn), lambda i, j: (i, (GATE_COL0 + br * D_MODEL) // tn + j))
    return pl.pallas_call(
        _merge_kernel,
        grid=(n // tm, D_MODEL // tn),
        in_specs=[o_spec, o_spec, o_spec,
                  pl.BlockSpec((N_BRANCH, BRANCH_W, tn), lambda i, j: (0, 0, j)),
                  gate_spec(0), gate_spec(1), gate_spec(2)],
        out_specs=pl.BlockSpec((tm, tn), lambda i, j: (i, j)),
        out_shape=jax.ShapeDtypeStruct((n, D_MODEL), BF16),
        compiler_params=_params("parallel", "parallel"),
        name="branch_merge",
    )(oa, ob, oc, wb, acg, acg, acg)


def _layer_norm(y, g, b):
    mu = jnp.mean(y, axis=-1, keepdims=True)
    var = jnp.mean(jnp.square(y - mu), axis=-1, keepdims=True)
    return (y - mu) * lax.rsqrt(var + LN_EPS) * g + b


def _outproj_kernel(m_ref, x_ref, w_ref, g_ref, b_ref, wrh_ref, wrl_ref, xo_ref, xb_ref, sc_ref):
    h = jnp.dot(m_ref[...], w_ref[...], preferred_element_type=F32)
    x1 = _layer_norm(DN_ALPHA * x_ref[...] + h, g_ref[...], b_ref[...])
    xo_ref[...] = x1
    xh, xl = _split2(x1)
    xb_ref[...] = xh
    d = functools.partial(jnp.dot, preferred_element_type=F32)
    logits = d(xh, wrh_ref[...]) + d(xl, wrh_ref[...]) + d(xh, wrl_ref[...])
    sc_ref[...] = _sigmoid(logits)


def _outproj(merged, x, w_out, g, b, wr_hi, wr_lo, tm):
    n = x.shape[0]
    row = pl.BlockSpec((tm, D_MODEL), lambda i: (i, 0))
    vec = pl.BlockSpec((1, D_MODEL), lambda i: (0, 0))
    wr = pl.BlockSpec((D_MODEL, N_EXPERTS), lambda i: (0, 0))
    return pl.pallas_call(
        _outproj_kernel,
        grid=(n // tm,),
        in_specs=[row, row, pl.BlockSpec((D_MODEL, D_MODEL), lambda i: (0, 0)), vec, vec, wr, wr],
        out_specs=[row, row, pl.BlockSpec((tm, N_EXPERTS), lambda i: (i, 0))],
        out_shape=[jax.ShapeDtypeStruct((n, D_MODEL), F32), jax.ShapeDtypeStruct((n, D_MODEL), BF16),
                   jax.ShapeDtypeStruct((n, N_EXPERTS), F32)],
        compiler_params=_params("parallel"),
        name="outproj_ln_router",
    )(merged, x, w_out, g, b, wr_hi, wr_lo)


def _expert_kernel(be_ref, nreal_ref, tok_ref, dst_ref, x_hbm, sw_ref, wg_ref, wu_ref, wd_ref,
                   out_hbm, xbuf, ybuf, sem_in, sem_out):
    i = pl.program_id(0)
    nr = nreal_ref[i]

    def in_copy(k):
        return pltpu.make_async_copy(x_hbm.at[pl.ds(tok_ref[0, 0, k], 1)], xbuf.at[pl.ds(k, 1)],
                                     sem_in.at[k])

    def out_copy(k):
        return pltpu.make_async_copy(ybuf.at[pl.ds(k, 1)], out_hbm.at[pl.ds(dst_ref[0, 0, k], 1)],
                                     sem_out.at[k])

    @pl.when(nr > 0)
    def _():
        def start_in(k, carry):
            in_copy(k).start()
            return carry

        def wait_in(k, carry):
            in_copy(k).wait()
            return carry

        lax.fori_loop(0, MOE_BLOCK, start_in, 0)
        lax.fori_loop(0, MOE_BLOCK, wait_in, 0)
        xb = xbuf[...].astype(BF16)
        hg = jnp.dot(xb, wg_ref[0], preferred_element_type=F32)
        hu = jnp.dot(xb, wu_ref[0], preferred_element_type=F32)
        hb = (_silu(hg) * hu).astype(BF16)
        ybuf[...] = jnp.dot(hb, wd_ref[0], preferred_element_type=F32) * sw_ref[...]

        def start_out(k, carry):
            out_copy(k).start()
            return carry

        def wait_out(k, carry):
            out_copy(k).wait()
            return carry

        lax.fori_loop(0, nr, start_out, 0)
        lax.fori_loop(0, nr, wait_out, 0)


def _experts(x1, block_e, n_real, slot_tok, slot_dst, slot_w, wg, wu, wd):
    n = x1.shape[0]
    n_blocks = block_e.shape[0]
    smem_spec = pl.BlockSpec((1, 1, MOE_BLOCK), lambda i, be, nr: (i, 0, 0), memory_space=pltpu.SMEM)
    return pl.pallas_call(
        _expert_kernel,
        grid_spec=pltpu.PrefetchScalarGridSpec(
            num_scalar_prefetch=2,
            grid=(n_blocks,),
            in_specs=[smem_spec, smem_spec,
                      pl.BlockSpec(memory_space=pl.ANY),
                      pl.BlockSpec((MOE_BLOCK, 1), lambda i, be, nr: (i, 0)),
                      pl.BlockSpec((1, D_MODEL, EXPERT_FF), lambda i, be, nr: (be[i], 0, 0)),
                      pl.BlockSpec((1, D_MODEL, EXPERT_FF), lambda i, be, nr: (be[i], 0, 0)),
                      pl.BlockSpec((1, EXPERT_FF, D_MODEL), lambda i, be, nr: (be[i], 0, 0))],
            out_specs=pl.BlockSpec(memory_space=pl.ANY),
            scratch_shapes=[pltpu.VMEM((MOE_BLOCK, D_MODEL), F32), pltpu.VMEM((MOE_BLOCK, D_MODEL), F32),
                            pltpu.SemaphoreType.DMA((MOE_BLOCK,)), pltpu.SemaphoreType.DMA((MOE_BLOCK,))]),
        out_shape=jax.ShapeDtypeStruct((n * TOP_K, D_MODEL), F32),
        compiler_params=_params("arbitrary"),
        name="routed_experts",
    )(block_e, n_real, slot_tok, slot_dst, x1, slot_w, wg, wu, wd)


def _routing(scores, router_bias):
    t = scores.shape[0]
    per_group = N_EXPERTS // N_GROUPS
    choice = scores + router_bias.astype(F32)
    group_score = jnp.sum(lax.top_k(choice.reshape(t, N_GROUPS, per_group), 2)[0], -1)
    gidx = lax.top_k(group_score, TOPK_GROUPS)[1]
    gsel = jnp.any(gidx[:, :, None] == jnp.arange(N_GROUPS)[None, None, :], axis=1)
    masked = jnp.where(jnp.repeat(gsel, per_group, axis=1), choice, -jnp.inf)
    topi = lax.top_k(masked, TOP_K)[1]
    topw = jnp.take_along_axis(scores, topi, axis=1)
    topw = topw / jnp.sum(topw, -1, keepdims=True) * ROUTE_SCALE

    n_assign = t * TOP_K
    flat_e = topi.reshape(n_assign)
    flat_w = topw.reshape(n_assign)
    order = jnp.argsort(flat_e)
    se = flat_e[order]
    counts = jnp.zeros((N_EXPERTS,), jnp.int32).at[flat_e].add(1)
    padded = (counts + MOE_BLOCK - 1) // MOE_BLOCK * MOE_BLOCK
    pad_end = jnp.cumsum(padded)
    pad_start = pad_end - padded
    start = jnp.cumsum(counts) - counts
    dest = pad_start[se] + jnp.arange(n_assign, dtype=jnp.int32) - start[se]
    n_blocks = -(-n_assign // MOE_BLOCK) + N_EXPERTS
    n_slots = n_blocks * MOE_BLOCK
    slot_tok = jnp.zeros((n_slots,), jnp.int32).at[dest].set((order // TOP_K).astype(jnp.int32))
    slot_dst = jnp.zeros((n_slots,), jnp.int32).at[dest].set(order.astype(jnp.int32))
    slot_w = jnp.zeros((n_slots,), F32).at[dest].set(flat_w[order])
    blk_start = jnp.arange(n_blocks, dtype=jnp.int32) * MOE_BLOCK
    block_e = jnp.minimum(jnp.searchsorted(pad_end, blk_start, side='right'), N_EXPERTS - 1).astype(jnp.int32)
    real_end = pad_start + counts
    n_real = jnp.clip(real_end[block_e] - blk_start, 0, MOE_BLOCK).astype(jnp.int32)
    n_real = jnp.where(blk_start < pad_end[-1], n_real, 0)
    return (block_e, n_real, slot_tok.reshape(n_blocks, 1, MOE_BLOCK),
            slot_dst.reshape(n_blocks, 1, MOE_BLOCK), slot_w.reshape(n_slots, 1))


def _combine_kernel(r_ref, x_ref, xb_ref, wg_ref, wu_ref, wd_ref, g_ref, b_ref, xo_ref, xob_ref):
    xb = xb_ref[...]
    hb = (_silu(jnp.dot(xb, wg_ref[...], preferred_element_type=F32))
          * jnp.dot(xb, wu_ref[...], preferred_element_type=F32)).astype(BF16)
    h = jnp.dot(hb, wd_ref[...], preferred_element_type=F32)
    for j in range(TOP_K):
        h = h + r_ref[:, j * D_MODEL:(j + 1) * D_MODEL]
    x2 = _layer_norm(DN_ALPHA * x_ref[...] + h, g_ref[...], b_ref[...])
    xo_ref[...] = x2
    xob_ref[...] = x2.astype(BF16)


def _combine(routed6, x1, x1b, wg, wu, wd, g, b, tm):
    n = x1.shape[0]
    row = pl.BlockSpec((tm, D_MODEL), lambda i: (i, 0))
    vec = pl.BlockSpec((1, D_MODEL), lambda i: (0, 0))
    w_in = pl.BlockSpec((D_MODEL, EXPERT_FF), lambda i: (0, 0))
    return pl.pallas_call(
        _combine_kernel,
        grid=(n // tm,),
        in_specs=[pl.BlockSpec((tm, TOP_K * D_MODEL), lambda i: (i, 0)), row, row, w_in, w_in,
                  pl.BlockSpec((EXPERT_FF, D_MODEL), lambda i: (0, 0)), vec, vec],
        out_specs=[row, row],
        out_shape=[jax.ShapeDtypeStruct((n, D_MODEL), F32), jax.ShapeDtypeStruct((n, D_MODEL), BF16)],
        compiler_params=_params("parallel"),
        name="combine_shared_ln",
    )(routed6, x1, x1b, wg, wu, wd, g, b)


def _ple_kernel(x_ref, xb_ref, p_ref, wg_ref, wp_ref, xo_ref, xob_ref):
    gate = _sigmoid(jnp.dot(xb_ref[...], wg_ref[...], preferred_element_type=F32))
    pe = jnp.dot(p_ref[...].astype(BF16), wp_ref[...], preferred_element_type=F32)
    x3 = x_ref[...] + pe * gate
    xo_ref[...] = x3
    xob_ref[...] = x3.astype(BF16)


def _ple(x2, x2b, p, wg, wp, tm):
    n = x2.shape[0]
    row = pl.BlockSpec((tm, D_MODEL), lambda i: (i, 0))
    return pl.pallas_call(
        _ple_kernel,
        grid=(n // tm,),
        in_specs=[row, row, pl.BlockSpec((tm, PLE_DIM), lambda i: (i, 0)),
                  pl.BlockSpec((D_MODEL, D_MODEL), lambda i: (0, 0)),
                  pl.BlockSpec((PLE_DIM, D_MODEL), lambda i: (0, 0))],
        out_specs=[row, row],
        out_shape=[jax.ShapeDtypeStruct((n, D_MODEL), F32), jax.ShapeDtypeStruct((n, D_MODEL), BF16)],
        compiler_params=_params("parallel"),
        name="ple",
    )(x2, x2b, p, wg, wp)


def _prep_weights(w_in, hgrn_lb_logits, gdn_a_log, gdn_dt_bias, w_branch, w_out, w_router,
                  w_exp_gate, w_exp_up, w_exp_down, w_sh_gate, w_sh_up, w_sh_down, w_ple_gate, w_ple_proj):
    pts = np.cumsum([1024, 1024, 1024, 1024, 512, 512, 1024, 1024, 8, 8, 1024, 1024, 1024, 1024]).tolist()
    a_end, b_end, c_end = pts[3], pts[9], pts[13]
    w_acg = jnp.concatenate([w_in[:, :, :a_end], w_in[:, :, b_end:c_end], w_in[:, :, c_end:]], axis=-1)
    w_b = jnp.concatenate([w_in[:, :, a_end:b_end],
                           jnp.zeros((DEPTH, D_MODEL, B_COLS - (b_end - a_end)), w_in.dtype)], axis=-1)
    lb_p = jax.nn.softmax(hgrn_lb_logits.astype(F32), axis=0)
    lb = jnp.maximum(jnp.cumsum(lb_p, axis=0) - lb_p[:1], 0.0)
    pad_row = lambda v: jnp.zeros((DEPTH, 1, HEAD), F32).at[:, 0, N_HEADS:2 * N_HEADS].set(v.astype(F32))
    wr = w_router.astype(F32)
    wr_hi = wr.astype(BF16)
    return dict(
        w_acg=w_acg.astype(BF16), w_b=w_b.astype(BF16),
        llb=jnp.log(lb)[:, None, :], l1m=jnp.log1p(-lb)[:, None, :], oml=(1.0 - lb)[:, None, :],
        alog_row=pad_row(gdn_a_log), dtb_row=pad_row(gdn_dt_bias),
        w_branch=w_branch.astype(BF16), w_out=w_out.astype(BF16),
        wr_hi=wr_hi, wr_lo=(wr - wr_hi.astype(F32)).astype(BF16),
        wg=w_exp_gate.astype(BF16), wu=w_exp_up.astype(BF16), wd=w_exp_down.astype(BF16),
        sg=w_sh_gate.astype(BF16), su=w_sh_up.astype(BF16), sd=w_sh_down.astype(BF16),
        pg=w_ple_gate.astype(BF16), pp=w_ple_proj.astype(BF16))


def _run_trunk(x, p, pos0, states, pw, small):
    (hgrn_norm, gdn_conv, gdn_norm, ret_norm, ln1_g, ln1_b, ln2_g, ln2_b, router_bias) = small
    s_hgrn, s_gdn, s_conv, s_ret = states
    b, t, d = x.shape
    n = b * t
    tm = min(n, 512)
    xf = x.reshape(n, d)
    xb = xf.astype(BF16)
    half = HEAD // 2
    inv = ROPE_BASE ** (-jnp.arange(half, dtype=F32) / half)
    ang = (pos0 + jnp.arange(t, dtype=F32))[:, None] * inv[None, :]
    cos_t = jnp.concatenate([jnp.cos(ang), jnp.cos(ang)], axis=-1)
    sin_t = jnp.concatenate([-jnp.sin(ang), jnp.sin(ang)], axis=-1)
    lg = jnp.log1p(-jnp.exp2(-5.0 - jnp.arange(N_HEADS, dtype=F32)))
    row = lambda v: v.reshape(1, -1).astype(F32)
    new = ([], [], [], [])
    for i in range(DEPTH):
        acg = _matmul(xb, pw['w_acg'][i], tm, 1024)
        yb = _matmul(xb, pw['w_b'][i], tm, 640)
        o_a, st_a = _hgrn(acg, pw['llb'][i], pw['l1m'][i], pw['oml'][i], row(hgrn_norm[i]),
                          jnp.swapaxes(s_hgrn[i], -1, -2), b, t)
        buf8 = jnp.concatenate([jnp.zeros((b, 8 - (CONV_W - 1), B_CONV_CH), F32), s_conv[i]], axis=1)
        qkvc = _gdn_conv(yb, gdn_conv[i], buf8, b, t)
        o_b, st_b = _gdn(qkvc, yb, pw['alog_row'][i], pw['dtb_row'][i], row(gdn_norm[i]), s_gdn[i], b, t)
        o_c, st_c = _retention(acg, lg, cos_t, sin_t, row(ret_norm[i]), s_ret[i], b, t)
        merged = _merge(o_a, o_b, o_c, pw['w_branch'][i], acg, tm)
        x1, x1b, scores = _outproj(merged, xf, pw['w_out'][i], row(ln1_g[i]), row(ln1_b[i]),
                                   pw['wr_hi'][i], pw['wr_lo'][i], tm)
        block_e, n_real, slot_tok, slot_dst, slot_w = _routing(scores, router_bias[i])
        routed = _experts(x1, block_e, n_real, slot_tok, slot_dst, slot_w,
                          pw['wg'][i], pw['wu'][i], pw['wd'][i])
        x2, x2b = _combine(routed.reshape(n, TOP_K * D_MODEL), x1, x1b, pw['sg'][i], pw['su'][i],
                           pw['sd'][i], row(ln2_g[i]), row(ln2_b[i]), min(n, 128))
        xf, xb = _ple(x2, x2b, p[i].reshape(n, PLE_DIM), pw['pg'][i], pw['pp'][i], tm)
        new[0].append(jnp.swapaxes(st_a, -1, -2))
        new[1].append(st_b)
        new[2].append(yb.reshape(b, t, B_COLS)[:, t - (CONV_W - 1):, :B_CONV_CH])
        new[3].append(st_c)
    return xf.reshape(b, t, d), tuple(jnp.stack(a) for a in new)


def kernel(x_prompt, x_sample, p_prompt, p_sample, state_hgrn, state_gdn, state_gdn_conv, state_ret,
           w_in, hgrn_lb_logits, hgrn_norm, gdn_conv, gdn_a_log, gdn_dt_bias, gdn_norm, ret_norm,
           w_branch, w_out, ln1_g, ln1_b, ln2_g, ln2_b, w_router, router_bias,
           w_exp_gate, w_exp_up, w_exp_down, w_sh_gate, w_sh_up, w_sh_down, w_ple_gate, w_ple_proj):
    pw = _prep_weights(w_in, hgrn_lb_logits, gdn_a_log, gdn_dt_bias, w_branch, w_out, w_router,
                       w_exp_gate, w_exp_up, w_exp_down, w_sh_gate, w_sh_up, w_sh_down,
                       w_ple_gate, w_ple_proj)
    small = (hgrn_norm, gdn_conv, gdn_norm, ret_norm, ln1_g, ln1_b, ln2_g, ln2_b, router_bias)
    bp = x_prompt.shape[0]
    zero_states = (jnp.zeros((DEPTH, bp, N_HEADS, HEAD, HEAD), F32),
                   jnp.zeros((DEPTH, bp, N_HEADS, HEAD, HEAD), F32),
                   jnp.zeros((DEPTH, bp, CONV_W - 1, B_CONV_CH), F32),
                   jnp.zeros((DEPTH, bp, N_HEADS, HEAD, HEAD), F32))
    y_p, (hp, gp, cp, rp) = _run_trunk(x_prompt, p_prompt, 0.0, zero_states, pw, small)
    y_s, (hs, gs, cs, rs) = _run_trunk(x_sample, p_sample, float(PAST_LEN),
                                       (state_hgrn, state_gdn, state_gdn_conv, state_ret), pw, small)
    return (y_p, y_s, hp, gp, cp, rp, hs, gs, cs, rs)
```

```python
import functools
import math

import jax
import jax.numpy as jnp
import numpy as np
from jax import lax
from jax.experimental import pallas as pl
from jax.experimental.pallas import tpu as pltpu

F32 = jnp.float32
BF16 = jnp.bfloat16

D_MODEL = 2048
DEPTH = 4
PAST_LEN = 4096
HEAD = 128
N_HEADS = 8
B_QK_HEADS = 4
CONV_W = 4
BRANCH_W = N_HEADS * HEAD
B_CONV_CH = 2 * B_QK_HEADS * HEAD + N_HEADS * HEAD
ROPE_BASE = 10000.0
N_BRANCH = 3
N_EXPERTS = 64
N_GROUPS = 8
TOPK_GROUPS = 4
TOP_K = 6
EXPERT_FF = 512
ROUTE_SCALE = 2.5
MOE_BLOCK = 256
PLE_DIM = 256
DN_ALPHA = (2.0 * DEPTH) ** 0.25
LN_EPS = 1e-5
NORM_EPS = 1e-6

HGRN_SUB = 16
GDN_CHUNK = 64
GDN_APPLY_PASSES = (3, 1, 1, 1, 1, 1)
GDN_SQUARE_PASSES = (3, 1, 1, 1, 1)
ACG_COLS = 4 * BRANCH_W + 4 * BRANCH_W + N_BRANCH * D_MODEL
GATE_COL0 = 8 * BRANCH_W
B_COLS = 3200
B_SMALL_COL = 3072

VMEM_LIMIT = 56 * 1024 * 1024


def _params(*sem):
    return pltpu.CompilerParams(dimension_semantics=sem, vmem_limit_bytes=VMEM_LIMIT)


def _bdot(a, b):
    return jnp.dot(a.astype(BF16), b.astype(BF16), preferred_element_type=F32)


def _bdot_nt(a, b):
    return lax.dot_general(a.astype(BF16), b.astype(BF16), (((1,), (1,)), ((), ())),
                           preferred_element_type=F32)


def _bdot_tn(a, b):
    return lax.dot_general(a.astype(BF16), b.astype(BF16), (((0,), (0,)), ((), ())),
                           preferred_element_type=F32)


def _split2(x):
    hi = x.astype(BF16)
    lo = (x - hi.astype(F32)).astype(BF16)
    return hi, lo


def _split3(x):
    hi = x.astype(BF16)
    r = x - hi.astype(F32)
    mid = r.astype(BF16)
    lo = (r - mid.astype(F32)).astype(BF16)
    return hi, mid, lo


def _mask_dot(mask_bf, x):
    hi, mid, lo = _split3(x)
    d = functools.partial(jnp.dot, preferred_element_type=F32)
    return d(mask_bf, hi) + d(mask_bf, mid) + d(mask_bf, lo)


def _dot_x3(a, b):
    ah, al = _split2(a)
    bh, bl = _split2(b)
    d = functools.partial(jnp.dot, preferred_element_type=F32)
    return d(ah, bh) + d(ah, bl) + d(al, bh)


def _dot_p(a, b, passes):
    return _dot_x3(a, b) if passes == 3 else _bdot(a, b)


def _sigmoid(x):
    return 1.0 / (1.0 + jnp.exp(-x))


def _silu(x):
    return x * _sigmoid(x)


def _softplus(x):
    return jnp.maximum(x, 0.0) + jnp.log1p(jnp.exp(-jnp.abs(x)))


def _log_sigmoid(x):
    return jnp.minimum(x, 0.0) - jnp.log1p(jnp.exp(-jnp.abs(x)))


def _mm_kernel(x_ref, w_ref, o_ref):
    o_ref[...] = jnp.dot(x_ref[...], w_ref[...], preferred_element_type=F32).astype(o_ref.dtype)


def _matmul(x, w, tm, tn):
    m, k = x.shape
    n = w.shape[1]
    return pl.pallas_call(
        _mm_kernel,
        grid=(m // tm, n // tn),
        in_specs=[pl.BlockSpec((tm, k), lambda i, j: (i, 0)),
                  pl.BlockSpec((k, tn), lambda i, j: (0, j))],
        out_specs=pl.BlockSpec((tm, tn), lambda i, j: (i, j)),
        out_shape=jax.ShapeDtypeStruct((m, n), F32),
        compiler_params=_params("parallel", "parallel"),
        name="inproj_matmul",
    )(x, w)


def _hgrn_kernel(q_ref, f_ref, i_ref, g_ref, llb_ref, l1m_ref, oml_ref, nw_ref, s0_ref,
                 o_ref, sout_ref, st_ref, *, tc):
    c = pl.program_id(2)

    @pl.when(c == 0)
    def _():
        st_ref[...] = s0_ref[0, 0]

    nb = tc // HGRN_SUB
    z = f_ref[...]
    bterm = l1m_ref[...] + _log_sigmoid(z)
    a = llb_ref[...]
    log_f = jnp.maximum(a, bterm) + jnp.log1p(jnp.exp(-jnp.abs(a - bterm)))
    key = oml_ref[...] * (1.0 / (1.0 + jnp.exp(z)))
    r = lax.broadcasted_iota(jnp.int32, (tc, tc), 0)
    s = lax.broadcasted_iota(jnp.int32, (tc, tc), 1)
    seg_mask = jnp.where((r // HGRN_SUB == s // HGRN_SUB) & (s <= r), 1.0, 0.0).astype(BF16)
    cum = _mask_dot(seg_mask, log_f)

    shp = (nb, HGRN_SUB, HEAD)
    cum3, key3 = cum.reshape(shp), key.reshape(shp)
    q3, v3 = q_ref[...].reshape(shp), i_ref[...].reshape(shp)
    rows = lax.broadcasted_iota(jnp.int32, shp, 1)
    parts = []
    for si in range(HGRN_SUB):
        arg = jnp.where(rows >= si, cum3 - cum3[:, si:si + 1, :], -1e30)
        parts.append((q3 * (key3[:, si:si + 1, :] * jnp.exp(arg))).reshape(tc, HEAD))
    p_hi, p_lo = _split2(jnp.concatenate(parts, axis=0))
    ones_bf = jnp.ones((HEAD, HEAD), BF16)
    att = (jnp.dot(p_hi, ones_bf, preferred_element_type=F32)
           + jnp.dot(p_lo, ones_bf, preferred_element_type=F32))
    o = jnp.zeros((tc, HEAD), F32)
    for si in range(HGRN_SUB):
        vs = jnp.broadcast_to(v3[:, si:si + 1, :], shp).reshape(tc, HEAD)
        o = o + att[si * tc:(si + 1) * tc, :] * vs
    last3 = cum3[:, HGRN_SUB - 1:HGRN_SUB, :]
    qtil = q3 * jnp.exp(cum3)
    ktil = key3 * jnp.exp(last3 - cum3)
    elast = jnp.exp(last3)
    st = st_ref[...]
    outs = []
    for j in range(nb):
        outs.append(o[j * HGRN_SUB:(j + 1) * HGRN_SUB, :] + _bdot_nt(qtil[j], st))
        st = elast[j] * st + _bdot_tn(v3[j], ktil[j])
    st_ref[...] = st

    o = jnp.concatenate(outs, axis=0)
    o = o * lax.rsqrt(jnp.mean(o * o, axis=-1, keepdims=True) + NORM_EPS) * nw_ref[...]
    o_ref[...] = (o * _silu(g_ref[...])).astype(o_ref.dtype)

    @pl.when(c == pl.num_programs(2) - 1)
    def _():
        sout_ref[0, 0] = st_ref[...]


def _hgrn(acg, llb, l1m, oml, nw, s0t, b, t):
    tc = min(t, 128)
    nt = t // tc
    row = lambda bi, h, c: bi * nt + c
    col_spec = lambda part: pl.BlockSpec((tc, HEAD), lambda bi, h, c: (row(bi, h, c), part * N_HEADS + h))
    vec_spec = pl.BlockSpec((1, HEAD), lambda bi, h, c: (0, h))
    st_spec = pl.BlockSpec((1, 1, HEAD, HEAD), lambda bi, h, c: (bi, h, 0, 0))
    return pl.pallas_call(
        functools.partial(_hgrn_kernel, tc=tc),
        grid=(b, N_HEADS, nt),
        in_specs=[col_spec(0), col_spec(1), col_spec(2), col_spec(3),
                  vec_spec, vec_spec, vec_spec, pl.BlockSpec((1, HEAD), lambda bi, h, c: (0, 0)), st_spec],
        out_specs=[pl.BlockSpec((tc, HEAD), lambda bi, h, c: (row(bi, h, c), h)), st_spec],
        out_shape=[jax.ShapeDtypeStruct((b * t, BRANCH_W), BF16),
                   jax.ShapeDtypeStruct((b, N_HEADS, HEAD, HEAD), F32)],
        scratch_shapes=[pltpu.VMEM((HEAD, HEAD), F32)],
        compiler_params=_params("parallel", "parallel", "arbitrary"),
        name="hgrn2",
    )(acg, acg, acg, acg, llb, l1m, oml, nw, s0t)


def _conv_kernel(u_ref, w_ref, buf_ref, o_ref, tail_ref, *, tc):
    c = pl.program_id(1)

    @pl.when(c == 0)
    def _():
        tail_ref[...] = buf_ref[0]

    u = u_ref[...]
    ext = jnp.concatenate([tail_ref[...], u], axis=0)
    acc = u * w_ref[CONV_W - 1:CONV_W, :]
    for j in range(CONV_W - 1):
        acc = acc + ext[5 + j:5 + j + tc, :] * w_ref[j:j + 1, :]
    tail_ref[...] = u[tc - 8:, :]
    y = _silu(acc)
    nqk = B_QK_HEADS * HEAD
    for hh in range(2 * B_QK_HEADS):
        blk = y[:, hh * HEAD:(hh + 1) * HEAD]
        nrm = blk * lax.rsqrt(jnp.sum(blk * blk, axis=-1, keepdims=True) + NORM_EPS)
        if hh < B_QK_HEADS:
            nrm = nrm * (HEAD ** -0.5)
        o_ref[:, hh * HEAD:(hh + 1) * HEAD] = nrm
    o_ref[:, 2 * nqk:] = y[:, 2 * nqk:]


def _gdn_conv(yb, w, buf8, b, t):
    tc = min(t, 256)
    nt = t // tc
    return pl.pallas_call(
        functools.partial(_conv_kernel, tc=tc),
        grid=(b, nt),
        in_specs=[pl.BlockSpec((tc, B_CONV_CH), lambda bi, c: (bi * nt + c, 0)),
                  pl.BlockSpec((CONV_W, B_CONV_CH), lambda bi, c: (0, 0)),
                  pl.BlockSpec((1, 8, B_CONV_CH), lambda bi, c: (bi, 0, 0))],
        out_specs=pl.BlockSpec((tc, B_CONV_CH), lambda bi, c: (bi * nt + c, 0)),
        out_shape=jax.ShapeDtypeStruct((b * t, B_CONV_CH), F32),
        scratch_shapes=[pltpu.VMEM((8, B_CONV_CH), F32)],
        compiler_params=_params("parallel", "arbitrary"),
        name="gdn_conv",
    )(yb, w, buf8)


def _gdn_kernel(qkv_ref, z_ref, sm_ref, alog_ref, dtb_ref, nw_ref, s0_ref, o_ref, sout_ref, st_ref):
    c = pl.program_id(1)
    cs = GDN_CHUNK

    @pl.when(c == 0)
    def _():
        st_ref[...] = s0_ref[0]

    sm = sm_ref[...]
    beta_all = _sigmoid(sm)
    g_all = -jnp.exp(alog_ref[...]) * _softplus(sm + dtb_ref[...])

    r = lax.broadcasted_iota(jnp.int32, (cs, cs), 0)
    s = lax.broadcasted_iota(jnp.int32, (cs, cs), 1)
    lower = r >= s
    strict = r > s
    tril_bf = jnp.where(lower, 1.0, 0.0).astype(BF16)
    triu_bf = jnp.where(r <= s, 1.0, 0.0).astype(BF16)
    above_bf = jnp.where(r < s, 1.0, 0.0).astype(BF16)
    nw = nw_ref[...]
    cum_all = _mask_dot(tril_bf, g_all)
    rem_all = _mask_dot(above_bf, g_all)
    d_tn = lambda a: lax.dot_general(a, triu_bf, (((0,), (0,)), ((), ())), preferred_element_type=F32)
    g_hi, g_mid, g_lo = _split3(g_all)
    cum_rows = d_tn(g_hi) + d_tn(g_mid) + d_tn(g_lo)

    rep = N_HEADS // B_QK_HEADS
    heads = range(N_HEADS)
    q = [qkv_ref[:, qh * HEAD:(qh + 1) * HEAD] for qh in range(B_QK_HEADS)]
    k = [qkv_ref[:, (B_QK_HEADS + qh) * HEAD:(B_QK_HEADS + qh + 1) * HEAD] for qh in range(B_QK_HEADS)]
    kk = [_bdot_nt(k[qh], k[qh]) for qh in range(B_QK_HEADS)]
    qk = [_bdot_nt(q[qh], k[qh]) for qh in range(B_QK_HEADS)]
    cum, rem, dec, x, y = [], [], [], [], []
    for h in heads:
        col = N_HEADS + h
        v = qkv_ref[:, (2 * B_QK_HEADS + h) * HEAD:(2 * B_QK_HEADS + h + 1) * HEAD]
        beta = beta_all[:, h:h + 1]
        cum.append(jnp.broadcast_to(cum_all[:, col:col + 1], (cs, HEAD)))
        rem.append(jnp.broadcast_to(rem_all[:, col:col + 1], (cs, HEAD)))
        seg = cum_all[:, col:col + 1] - cum_rows[col:col + 1, :]
        dec.append(jnp.where(lower, jnp.exp(jnp.where(lower, seg, 0.0)), 0.0))
        x.append(jnp.where(strict, -(kk[h // rep] * beta) * dec[h], 0.0))
        y.append(jnp.concatenate([v * beta, k[h // rep] * beta * jnp.exp(cum[h])], axis=1))
    for lvl in range(len(GDN_APPLY_PASSES)):
        y = [y[h] + _dot_p(x[h], y[h], GDN_APPLY_PASSES[lvl]) for h in heads]
        if lvl < len(GDN_SQUARE_PASSES):
            x = [_dot_p(x[h], x[h], GDN_SQUARE_PASSES[lvl]) for h in heads]
    st = [st_ref[h] for h in heads]
    u = [y[h][:, :HEAD] - _bdot(y[h][:, HEAD:], st[h]) for h in heads]
    o = [_bdot(q[h // rep] * jnp.exp(cum[h]), st[h]) + _bdot(qk[h // rep] * dec[h], u[h]) for h in heads]
    for h in heads:
        st_ref[h] = jnp.exp(cum[h][cs - 1:cs, :]) * st[h] + _bdot_tn(k[h // rep] * jnp.exp(rem[h]), u[h])
    for h in heads:
        on = o[h] * lax.rsqrt(jnp.mean(o[h] * o[h], axis=-1, keepdims=True) + NORM_EPS) * nw
        zg = z_ref[:, h * HEAD:(h + 1) * HEAD]
        o_ref[:, h * HEAD:(h + 1) * HEAD] = (on * _silu(zg)).astype(o_ref.dtype)

    @pl.when(c == pl.num_programs(1) - 1)
    def _():
        sout_ref[0] = st_ref[...]


def _gdn(qkvc, yb, alog_row, dtb_row, nw, s0, b, t):
    cs = GDN_CHUNK
    nt = t // cs
    st_spec = pl.BlockSpec((1, N_HEADS, HEAD, HEAD), lambda bi, c: (bi, 0, 0, 0))
    row_spec = pl.BlockSpec((1, HEAD), lambda bi, c: (0, 0))
    return pl.pallas_call(
        _gdn_kernel,
        grid=(b, nt),
        in_specs=[pl.BlockSpec((cs, B_CONV_CH), lambda bi, c: (bi * nt + c, 0)),
                  pl.BlockSpec((cs, BRANCH_W), lambda bi, c: (bi * nt + c, B_CONV_CH // BRANCH_W)),
                  pl.BlockSpec((cs, HEAD), lambda bi, c: (bi * nt + c, B_SMALL_COL // HEAD)),
                  row_spec, row_spec, row_spec, st_spec],
        out_specs=[pl.BlockSpec((cs, BRANCH_W), lambda bi, c: (bi * nt + c, 0)), st_spec],
        out_shape=[jax.ShapeDtypeStruct((b * t, BRANCH_W), BF16),
                   jax.ShapeDtypeStruct((b, N_HEADS, HEAD, HEAD), F32)],
        scratch_shapes=[pltpu.VMEM((N_HEADS, HEAD, HEAD), F32)],
        compiler_params=_params("parallel", "arbitrary"),
        name="gated_delta",
    )(qkvc, yb, yb, alog_row, dtb_row, nw, s0)


def _ret_kernel(lg_ref, q_ref, k_ref, v_ref, g_ref, cos_ref, sin_ref, nw_ref, s0_ref,
                o_ref, sout_ref, st_ref, *, cs):
    h = pl.program_id(1)
    c = pl.program_id(2)

    @pl.when(c == 0)
    def _():
        st_ref[...] = s0_ref[0, 0]

    lg = lg_ref[h]
    cos = cos_ref[...]
    sin = sin_ref[...]

    def rot(x):
        return x * cos + pltpu.roll(x, HEAD // 2, 1) * sin

    q = rot(q_ref[...])
    k = rot(k_ref[...]) * (HEAD ** -0.5)
    v = v_ref[...]
    idx = lax.broadcasted_iota(jnp.int32, (cs, HEAD), 0).astype(F32)
    q_dec = jnp.exp((idx + 1.0) * lg)
    k_dec = jnp.exp((cs - 1.0 - idx) * lg)
    c_dec = jnp.exp(jnp.zeros((1, HEAD), F32) + cs * lg)
    rel = (lax.broadcasted_iota(jnp.int32, (cs, cs), 0)
           - lax.broadcasted_iota(jnp.int32, (cs, cs), 1))
    intra = jnp.where(rel >= 0, jnp.exp(jnp.maximum(rel, 0).astype(F32) * lg), 0.0)
    st = st_ref[...]
    att = _bdot_nt(q, k) * intra
    o = _bdot(q, st) * q_dec + _bdot(att, v)
    st_ref[...] = c_dec * st + _bdot_tn(k * k_dec, v)
    mu = jnp.mean(o, axis=-1, keepdims=True)
    var = jnp.mean(jnp.square(o - mu), axis=-1, keepdims=True)
    o = (o - mu) * lax.rsqrt(var + NORM_EPS) * nw_ref[...]
    o_ref[...] = (o * _silu(g_ref[...])).astype(o_ref.dtype)

    @pl.when(c == pl.num_programs(2) - 1)
    def _():
        sout_ref[0, 0] = st_ref[...]


def _retention(acg, lg, cos, sin, nw, s0, b, t):
    cs = min(t, 256)
    nt = t // cs
    col_spec = lambda part: pl.BlockSpec(
        (cs, HEAD), lambda bi, h, c, lg_ref: (bi * nt + c, (4 + part) * N_HEADS + h))
    tab_spec = pl.BlockSpec((cs, HEAD), lambda bi, h, c, lg_ref: (c, 0))
    st_spec = pl.BlockSpec((1, 1, HEAD, HEAD), lambda bi, h, c, lg_ref: (bi, h, 0, 0))
    return pl.pallas_call(
        functools.partial(_ret_kernel, cs=cs),
        grid_spec=pltpu.PrefetchScalarGridSpec(
            num_scalar_prefetch=1,
            grid=(b, N_HEADS, nt),
            in_specs=[col_spec(0), col_spec(1), col_spec(2), col_spec(3), tab_spec, tab_spec,
                      pl.BlockSpec((1, HEAD), lambda bi, h, c, lg_ref: (0, 0)), st_spec],
            out_specs=[pl.BlockSpec((cs, HEAD), lambda bi, h, c, lg_ref: (bi * nt + c, h)), st_spec],
            scratch_shapes=[pltpu.VMEM((HEAD, HEAD), F32)]),
        out_shape=[jax.ShapeDtypeStruct((b * t, BRANCH_W), BF16),
                   jax.ShapeDtypeStruct((b, N_HEADS, HEAD, HEAD), F32)],
        compiler_params=_params("parallel", "parallel", "arbitrary"),
        name="retention",
    )(lg, acg, acg, acg, acg, cos, sin, nw, s0)


def _merge_kernel(oa_ref, ob_ref, oc_ref, wb_ref, ga_ref, gb_ref, gc_ref, o_ref):
    acc = _sigmoid(ga_ref[...]) * jnp.dot(oa_ref[...], wb_ref[0], preferred_element_type=F32)
    acc += _sigmoid(gb_ref[...]) * jnp.dot(ob_ref[...], wb_ref[1], preferred_element_type=F32)
    acc += _sigmoid(gc_ref[...]) * jnp.dot(oc_ref[...], wb_ref[2], preferred_element_type=F32)
    o_ref[...] = acc.astype(o_ref.dtype)


def _merge(oa, ob, oc, wb, acg, tm):
    n = oa.shape[0]
    tn = 512
    o_spec = pl.BlockSpec((tm, BRANCH_W), lambda i, j: (i, 0))
    gate_spec = lambda br: pl.BlockSpec(
        (tm, tn), lambda i, j: (i, (GATE_COL0 + br * D_MODEL) // tn + j))
    return pl.pallas_call(
        _merge_kernel,
        grid=(n // tm, D_MODEL // tn),
        in_specs=[o_spec, o_spec, o_spec,
                  pl.BlockSpec((N_BRANCH, BRANCH_W, tn), lambda i, j: (0, 0, j)),
                  gate_spec(0), gate_spec(1), gate_spec(2)],
        out_specs=pl.BlockSpec((tm, tn), lambda i, j: (i, j)),
        out_shape=jax.ShapeDtypeStruct((n, D_MODEL), BF16),
        compiler_params=_params("parallel", "parallel"),
        name="branch_merge",
    )(oa, ob, oc, wb, acg, acg, acg)


def _layer_norm(y, g, b):
    mu = jnp.mean(y, axis=-1, keepdims=True)
    var = jnp.mean(jnp.square(y - mu), axis=-1, keepdims=True)
    return (y - mu) * lax.rsqrt(var + LN_EPS) * g + b


def _outproj_kernel(m_ref, x_ref, w_ref, g_ref, b_ref, wrh_ref, wrl_ref, xo_ref, xb_ref, sc_ref):
    h = jnp.dot(m_ref[...], w_ref[...], preferred_element_type=F32)
    x1 = _layer_norm(DN_ALPHA * x_ref[...] + h, g_ref[...], b_ref[...])
    xo_ref[...] = x1
    xh, xl = _split2(x1)
    xb_ref[...] = xh
    d = lambda a, b_: lax.dot_general(a, b_, (((1,), (1,)), ((), ())), preferred_element_type=F32)
    logits = d(wrh_ref[...], xh) + d(wrh_ref[...], xl) + d(wrl_ref[...], xh)
    sc_ref[...] = _sigmoid(logits)


def _outproj(merged, x, w_out, g, b, wr_hi, wr_lo, tm):
    n = x.shape[0]
    row = pl.BlockSpec((tm, D_MODEL), lambda i: (i, 0))
    vec = pl.BlockSpec((1, D_MODEL), lambda i: (0, 0))
    wr = pl.BlockSpec((N_EXPERTS, D_MODEL), lambda i: (0, 0))
    return pl.pallas_call(
        _outproj_kernel,
        grid=(n // tm,),
        in_specs=[row, row, pl.BlockSpec((D_MODEL, D_MODEL), lambda i: (0, 0)), vec, vec, wr, wr],
        out_specs=[row, row, pl.BlockSpec((N_EXPERTS, tm), lambda i: (0, i))],
        out_shape=[jax.ShapeDtypeStruct((n, D_MODEL), F32), jax.ShapeDtypeStruct((n, D_MODEL), BF16),
                   jax.ShapeDtypeStruct((N_EXPERTS, n), F32)],
        compiler_params=_params("parallel"),
        name="outproj_ln_router",
    )(merged, x, w_out, g, b, wr_hi, wr_lo)


def _route_kernel(sc_ref, bias_ref, e_ref, pos_ref, w_ref, cnt_ref, carry_ref, *, tm):
    i = pl.program_id(0)
    per_group = N_EXPERTS // N_GROUPS

    @pl.when(i == 0)
    def _():
        carry_ref[...] = jnp.zeros_like(carry_ref)

    s = sc_ref[...]
    c3 = (s + bias_ref[...]).reshape(N_GROUPS, per_group, tm)
    midx = lax.broadcasted_iota(jnp.int32, (N_GROUPS, per_group, tm), 1)
    gidx = lax.broadcasted_iota(jnp.int32, (N_GROUPS, per_group, tm), 0)

    def beats(other, mine, other_first):
        return jnp.where((other > mine) | ((other == mine) & other_first), 1, 0)

    r1 = jnp.zeros((N_GROUPS, per_group, tm), jnp.int32)
    for m in range(per_group):
        r1 = r1 + beats(c3[:, m:m + 1, :], c3, m < midx)
    gs = jnp.sum(jnp.where(r1 < 2, c3, 0.0), axis=1, keepdims=True)
    gs = jnp.broadcast_to(gs, (N_GROUPS, per_group, tm))
    r2 = jnp.zeros((N_GROUPS, per_group, tm), jnp.int32)
    for g in range(N_GROUPS):
        r2 = r2 + beats(gs[g:g + 1], gs, g < gidx)
    masked = jnp.where(r2 < TOPK_GROUPS, c3, -jnp.inf).reshape(N_EXPERTS, tm)
    eidx = lax.broadcasted_iota(jnp.int32, (N_EXPERTS, tm), 0)
    r3 = jnp.zeros((N_EXPERTS, tm), jnp.int32)
    for e in range(N_EXPERTS):
        r3 = r3 + beats(masked[e:e + 1, :], masked, e < eidx)
    sel = r3 < TOP_K
    w = jnp.where(sel, s, 0.0)
    w = w / jnp.sum(w, axis=0, keepdims=True) * ROUTE_SCALE
    sel_bf = jnp.where(sel, 1.0, 0.0).astype(BF16)
    tr = lax.broadcasted_iota(jnp.int32, (tm, tm), 0)
    tc_ = lax.broadcasted_iota(jnp.int32, (tm, tm), 1)
    before = jnp.where(tr < tc_, 1.0, 0.0).astype(BF16)
    carry = carry_ref[...]
    pos = (jnp.dot(sel_bf, before, preferred_element_type=F32)
           + jnp.concatenate([carry] * (tm // HEAD), axis=1))
    carry_ref[...] = carry + jnp.dot(sel_bf, jnp.ones((tm, HEAD), BF16), preferred_element_type=F32)
    er = lax.broadcasted_iota(jnp.int32, (N_EXPERTS, N_EXPERTS), 0)
    ec = lax.broadcasted_iota(jnp.int32, (N_EXPERTS, N_EXPERTS), 1)
    lower_e = jnp.where(ec < er, 1.0, 0.0).astype(BF16)
    jidx = jnp.dot(lower_e, sel_bf, preferred_element_type=F32)
    eidx_f = eidx.astype(F32)
    e_rows, p_rows, w_rows = [], [], []
    for j in range(TOP_K):
        oh = sel & (jidx == float(j))
        e_rows.append(jnp.sum(jnp.where(oh, eidx_f, 0.0), axis=0, keepdims=True))
        p_rows.append(jnp.sum(jnp.where(oh, pos, 0.0), axis=0, keepdims=True))
        w_rows.append(jnp.sum(jnp.where(oh, w, 0.0), axis=0, keepdims=True))
    zpad = [jnp.zeros((8 - TOP_K, tm), F32)]
    e_ref[...] = jnp.concatenate(e_rows + zpad, axis=0).astype(jnp.int32)
    pos_ref[...] = jnp.concatenate(p_rows + zpad, axis=0).astype(jnp.int32)
    w_ref[...] = jnp.concatenate(w_rows + zpad, axis=0)

    @pl.when(i == pl.num_programs(0) - 1)
    def _():
        cnt_ref[...] = carry_ref[...]


def _route(scores_t, bias_b, tm):
    n = scores_t.shape[1]
    blk = pl.BlockSpec((8, tm), lambda i: (0, i))
    return pl.pallas_call(
        functools.partial(_route_kernel, tm=tm),
        grid=(n // tm,),
        in_specs=[pl.BlockSpec((N_EXPERTS, tm), lambda i: (0, i)),
                  pl.BlockSpec((N_EXPERTS, tm), lambda i: (0, 0))],
        out_specs=[blk, blk, blk, pl.BlockSpec((N_EXPERTS, HEAD), lambda i: (0, 0))],
        out_shape=[jax.ShapeDtypeStruct((8, n), jnp.int32), jax.ShapeDtypeStruct((8, n), jnp.int32),
                   jax.ShapeDtypeStruct((8, n), F32), jax.ShapeDtypeStruct((N_EXPERTS, HEAD), F32)],
        scratch_shapes=[pltpu.VMEM((N_EXPERTS, HEAD), F32)],
        compiler_params=_params("arbitrary"),
        name="route_topk",
    )(scores_t, bias_b)


def _expert_kernel(be_ref, nreal_ref, slot_ref, x_hbm, wg_ref, wu_ref, wd_ref,
                   out_hbm, xbuf, ybuf, sem_in, sem_out, *, n_tok):
    i = pl.program_id(0)
    nr = nreal_ref[i]

    def in_copy(k):
        tok = slot_ref[0, 0, k] >> 3
        return pltpu.make_async_copy(x_hbm.at[pl.ds(tok, 1)], xbuf.at[pl.ds(k, 1)], sem_in.at[k])

    def out_copy(k):
        v = slot_ref[0, 0, k]
        dst = (v & 7) * n_tok + (v >> 3)
        return pltpu.make_async_copy(ybuf.at[pl.ds(k, 1)], out_hbm.at[pl.ds(dst, 1)], sem_out.at[k])

    @pl.when(nr > 0)
    def _():
        def start_in(k, carry):
            in_copy(k).start()
            return carry

        def wait_in(k, carry):
            in_copy(k).wait()
            return carry

        lax.fori_loop(0, MOE_BLOCK, start_in, 0)
        lax.fori_loop(0, MOE_BLOCK, wait_in, 0)
        xb = xbuf[...].astype(BF16)
        hg = jnp.dot(xb, wg_ref[0], preferred_element_type=F32)
        hu = jnp.dot(xb, wu_ref[0], preferred_element_type=F32)
        hb = (_silu(hg) * hu).astype(BF16)
        ybuf[...] = jnp.dot(hb, wd_ref[0], preferred_element_type=F32)

        def start_out(k, carry):
            out_copy(k).start()
            return carry

        def wait_out(k, carry):
            out_copy(k).wait()
            return carry

        lax.fori_loop(0, nr, start_out, 0)
        lax.fori_loop(0, nr, wait_out, 0)


def _experts(x1, block_e, n_real, slots, wg, wu, wd):
    n = x1.shape[0]
    n_blocks = block_e.shape[0]
    return pl.pallas_call(
        functools.partial(_expert_kernel, n_tok=n),
        grid_spec=pltpu.PrefetchScalarGridSpec(
            num_scalar_prefetch=2,
            grid=(n_blocks,),
            in_specs=[pl.BlockSpec((1, 1, MOE_BLOCK), lambda i, be, nr: (i, 0, 0), memory_space=pltpu.SMEM),
                      pl.BlockSpec(memory_space=pl.ANY),
                      pl.BlockSpec((1, D_MODEL, EXPERT_FF), lambda i, be, nr: (be[i], 0, 0)),
                      pl.BlockSpec((1, D_MODEL, EXPERT_FF), lambda i, be, nr: (be[i], 0, 0)),
                      pl.BlockSpec((1, EXPERT_FF, D_MODEL), lambda i, be, nr: (be[i], 0, 0))],
            out_specs=pl.BlockSpec(memory_space=pl.ANY),
            scratch_shapes=[pltpu.VMEM((MOE_BLOCK, D_MODEL), F32), pltpu.VMEM((MOE_BLOCK, D_MODEL), F32),
                            pltpu.SemaphoreType.DMA((MOE_BLOCK,)), pltpu.SemaphoreType.DMA((MOE_BLOCK,))]),
        out_shape=jax.ShapeDtypeStruct((TOP_K * n, D_MODEL), F32),
        compiler_params=_params("arbitrary"),
        name="routed_experts",
    )(block_e, n_real, slots, x1, wg, wu, wd)


def _slot_tables(e6, pos6, counts):
    n = e6.shape[1]
    n_assign = n * TOP_K
    n_blocks = -(-n_assign // MOE_BLOCK) + N_EXPERTS
    padded = (counts + MOE_BLOCK - 1) // MOE_BLOCK * MOE_BLOCK
    pad_end = jnp.cumsum(padded)
    pad_start = pad_end - padded
    dest = pad_start[e6[:TOP_K]] + pos6[:TOP_K]
    packed = (jnp.arange(n, dtype=jnp.int32)[None, :] * 8
              + jnp.arange(TOP_K, dtype=jnp.int32)[:, None])
    slots = jnp.zeros((n_blocks * MOE_BLOCK,), jnp.int32).at[dest.reshape(-1)].set(
        packed.reshape(-1), unique_indices=True)
    blk_start = jnp.arange(n_blocks, dtype=jnp.int32) * MOE_BLOCK
    block_e = jnp.minimum(jnp.sum((pad_end[None, :] <= blk_start[:, None]).astype(jnp.int32), axis=1),
                          N_EXPERTS - 1)
    real_end = pad_start + counts
    n_real = jnp.clip(real_end[block_e] - blk_start, 0, MOE_BLOCK)
    n_real = jnp.where(blk_start < pad_end[-1], n_real, 0).astype(jnp.int32)
    return block_e, n_real, slots.reshape(n_blocks, 1, MOE_BLOCK)


def _combine_kernel(*refs):
    r_refs = refs[:TOP_K]
    w6_ref, x_ref, xb_ref, wg_ref, wu_ref, wd_ref, g_ref, b_ref, xo_ref, xob_ref = refs[TOP_K:]
    xb = xb_ref[...]
    hb = (_silu(jnp.dot(xb, wg_ref[...], preferred_element_type=F32))
          * jnp.dot(xb, wu_ref[...], preferred_element_type=F32)).astype(BF16)
    shared = jnp.dot(hb, wd_ref[...], preferred_element_type=F32)
    w6 = w6_ref[...]
    routed = r_refs[0][...] * w6[:, 0:1]
    for j in range(1, TOP_K):
        routed = routed + r_refs[j][...] * w6[:, j:j + 1]
    x2 = _layer_norm(DN_ALPHA * x_ref[...] + (routed + shared), g_ref[...], b_ref[...])
    xo_ref[...] = x2
    xob_ref[...] = x2.astype(BF16)


def _combine(routed, w6, x1, x1b, wg, wu, wd, g, b, tm):
    n = x1.shape[0]
    nt = n // tm
    row = pl.BlockSpec((tm, D_MODEL), lambda i: (i, 0))
    vec = pl.BlockSpec((1, D_MODEL), lambda i: (0, 0))
    w_in = pl.BlockSpec((D_MODEL, EXPERT_FF), lambda i: (0, 0))
    r_specs = [pl.BlockSpec((tm, D_MODEL), functools.partial(lambda i, j: (j * nt + i, 0), j=j))
               for j in range(TOP_K)]
    return pl.pallas_call(
        _combine_kernel,
        grid=(nt,),
        in_specs=r_specs + [pl.BlockSpec((tm, 8), lambda i: (i, 0)), row, row, w_in, w_in,
                            pl.BlockSpec((EXPERT_FF, D_MODEL), lambda i: (0, 0)), vec, vec],
        out_specs=[row, row],
        out_shape=[jax.ShapeDtypeStruct((n, D_MODEL), F32), jax.ShapeDtypeStruct((n, D_MODEL), BF16)],
        compiler_params=_params("parallel"),
        name="combine_shared_ln",
    )(*([routed] * TOP_K), w6, x1, x1b, wg, wu, wd, g, b)


def _ple_kernel(x_ref, xb_ref, p_ref, wg_ref, wp_ref, xo_ref, xob_ref):
    gate = _sigmoid(jnp.dot(xb_ref[...], wg_ref[...], preferred_element_type=F32))
    pe = jnp.dot(p_ref[...].astype(BF16), wp_ref[...], preferred_element_type=F32)
    x3 = x_ref[...] + pe * gate
    xo_ref[...] = x3
    xob_ref[...] = x3.astype(BF16)


def _ple(x2, x2b, p, wg, wp, tm):
    n = x2.shape[0]
    row = pl.BlockSpec((tm, D_MODEL), lambda i: (i, 0))
    return pl.pallas_call(
        _ple_kernel,
        grid=(n // tm,),
        in_specs=[row, row, pl.BlockSpec((tm, PLE_DIM), lambda i: (i, 0)),
                  pl.BlockSpec((D_MODEL, D_MODEL), lambda i: (0, 0)),
                  pl.BlockSpec((PLE_DIM, D_MODEL), lambda i: (0, 0))],
        out_specs=[row, row],
        out_shape=[jax.ShapeDtypeStruct((n, D_MODEL), F32), jax.ShapeDtypeStruct((n, D_MODEL), BF16)],
        compiler_params=_params("parallel"),
        name="ple",
    )(x2, x2b, p, wg, wp)


def _prep_weights(w_in, hgrn_lb_logits, gdn_a_log, gdn_dt_bias, w_branch, w_out, w_router,
                  w_exp_gate, w_exp_up, w_exp_down, w_sh_gate, w_sh_up, w_sh_down, w_ple_gate, w_ple_proj):
    pts = np.cumsum([1024, 1024, 1024, 1024, 512, 512, 1024, 1024, 8, 8, 1024, 1024, 1024, 1024]).tolist()
    a_end, b_end, c_end = pts[3], pts[9], pts[13]
    w_acg = jnp.concatenate([w_in[:, :, :a_end], w_in[:, :, b_end:c_end], w_in[:, :, c_end:]], axis=-1)
    w_b = jnp.concatenate([w_in[:, :, a_end:b_end],
                           jnp.zeros((DEPTH, D_MODEL, B_COLS - (b_end - a_end)), w_in.dtype)], axis=-1)
    lb_p = jax.nn.softmax(hgrn_lb_logits.astype(F32), axis=0)
    lb = jnp.maximum(jnp.cumsum(lb_p, axis=0) - lb_p[:1], 0.0)
    pad_row = lambda v: jnp.zeros((DEPTH, 1, HEAD), F32).at[:, 0, N_HEADS:2 * N_HEADS].set(v.astype(F32))
    wr = jnp.swapaxes(w_router.astype(F32), -1, -2)
    wr_hi = wr.astype(BF16)
    return dict(
        w_acg=w_acg.astype(BF16), w_b=w_b.astype(BF16),
        llb=jnp.log(lb)[:, None, :], l1m=jnp.log1p(-lb)[:, None, :], oml=(1.0 - lb)[:, None, :],
        alog_row=pad_row(gdn_a_log), dtb_row=pad_row(gdn_dt_bias),
        w_branch=w_branch.astype(BF16), w_out=w_out.astype(BF16),
        wr_hi=wr_hi, wr_lo=(wr - wr_hi.astype(F32)).astype(BF16),
        wg=w_exp_gate.astype(BF16), wu=w_exp_up.astype(BF16), wd=w_exp_down.astype(BF16),
        sg=w_sh_gate.astype(BF16), su=w_sh_up.astype(BF16), sd=w_sh_down.astype(BF16),
        pg=w_ple_gate.astype(BF16), pp=w_ple_proj.astype(BF16))


def _run_trunk(x, p, pos0, states, pw, small):
    (hgrn_norm, gdn_conv, gdn_norm, ret_norm, ln1_g, ln1_b, ln2_g, ln2_b, router_bias) = small
    s_hgrn, s_gdn, s_conv, s_ret = states
    b, t, d = x.shape
    n = b * t
    tm = min(n, 512)
    xf = x.reshape(n, d)
    xb = xf.astype(BF16)
    half = HEAD // 2
    inv = ROPE_BASE ** (-jnp.arange(half, dtype=F32) / half)
    ang = (pos0 + jnp.arange(t, dtype=F32))[:, None] * inv[None, :]
    cos_t = jnp.concatenate([jnp.cos(ang), jnp.cos(ang)], axis=-1)
    sin_t = jnp.concatenate([-jnp.sin(ang), jnp.sin(ang)], axis=-1)
    lg = jnp.log1p(-jnp.exp2(-5.0 - jnp.arange(N_HEADS, dtype=F32)))
    row = lambda v: v.reshape(1, -1).astype(F32)
    new = ([], [], [], [])
    for i in range(DEPTH):
        acg = _matmul(xb, pw['w_acg'][i], tm, 1024)
        yb = _matmul(xb, pw['w_b'][i], tm, 640)
        o_a, st_a = _hgrn(acg, pw['llb'][i], pw['l1m'][i], pw['oml'][i], row(hgrn_norm[i]),
                          jnp.swapaxes(s_hgrn[i], -1, -2), b, t)
        buf8 = jnp.concatenate([jnp.zeros((b, 8 - (CONV_W - 1), B_CONV_CH), F32), s_conv[i]], axis=1)
        qkvc = _gdn_conv(yb, gdn_conv[i], buf8, b, t)
        o_b, st_b = _gdn(qkvc, yb, pw['alog_row'][i], pw['dtb_row'][i], row(gdn_norm[i]), s_gdn[i], b, t)
        o_c, st_c = _retention(acg, lg, cos_t, sin_t, row(ret_norm[i]), s_ret[i], b, t)
        merged = _merge(o_a, o_b, o_c, pw['w_branch'][i], acg, tm)
        x1, x1b, scores = _outproj(merged, xf, pw['w_out'][i], row(ln1_g[i]), row(ln1_b[i]),
                                   pw['wr_hi'][i], pw['wr_lo'][i], tm)
        bias_b = jnp.broadcast_to(router_bias[i].astype(F32)[:, None], (N_EXPERTS, tm))
        e6, pos6, w6, cnt = _route(scores, bias_b, tm)
        block_e, n_real, slots = _slot_tables(e6, pos6, cnt[:, 0].astype(jnp.int32))
        routed = _experts(x1, block_e, n_real, slots, pw['wg'][i], pw['wu'][i], pw['wd'][i])
        x2, x2b = _combine(routed, w6.T, x1, x1b, pw['sg'][i], pw['su'][i],
                           pw['sd'][i], row(ln2_g[i]), row(ln2_b[i]), min(n, 128))
        xf, xb = _ple(x2, x2b, p[i].reshape(n, PLE_DIM), pw['pg'][i], pw['pp'][i], tm)
        new[0].append(jnp.swapaxes(st_a, -1, -2))
        new[1].append(st_b)
        new[2].append(yb.reshape(b, t, B_COLS)[:, t - (CONV_W - 1):, :B_CONV_CH])
        new[3].append(st_c)
    return xf.reshape(b, t, d), tuple(jnp.stack(a) for a in new)


def kernel(x_prompt, x_sample, p_prompt, p_sample, state_hgrn, state_gdn, state_gdn_conv, state_ret,
           w_in, hgrn_lb_logits, hgrn_norm, gdn_conv, gdn_a_log, gdn_dt_bias, gdn_norm, ret_norm,
           w_branch, w_out, ln1_g, ln1_b, ln2_g, ln2_b, w_router, router_bias,
           w_exp_gate, w_exp_up, w_exp_down, w_sh_gate, w_sh_up, w_sh_down, w_ple_gate, w_ple_proj):
    pw = _prep_weights(w_in, hgrn_lb_logits, gdn_a_log, gdn_dt_bias, w_branch, w_out, w_router,
                       w_exp_gate, w_exp_up, w_exp_down, w_sh_gate, w_sh_up, w_sh_down,
                       w_ple_gate, w_ple_proj)
    small = (hgrn_norm, gdn_conv, gdn_norm, ret_norm, ln1_g, ln1_b, ln2_g, ln2_b, router_bias)
    bp = x_prompt.shape[0]
    zero_states = (jnp.zeros((DEPTH, bp, N_HEADS, HEAD, HEAD), F32),
                   jnp.zeros((DEPTH, bp, N_HEADS, HEAD, HEAD), F32),
                   jnp.zeros((DEPTH, bp, CONV_W - 1, B_CONV_CH), F32),
                   jnp.zeros((DEPTH, bp, N_HEADS, HEAD, HEAD), F32))
    y_p, (hp, gp, cp, rp) = _run_trunk(x_prompt, p_prompt, 0.0, zero_states, pw, small)
    y_s, (hs, gs, cs, rs) = _run_trunk(x_sample, p_sample, float(PAST_LEN),
                                       (state_hgrn, state_gdn, state_gdn_conv, state_ret), pw, small)
    return (y_p, y_s, hp, gp, cp, rp, hs, gs, cs, rs)
```

```python
import functools
import math

import jax
import jax.numpy as jnp
import numpy as np
from jax import lax
from jax.experimental import pallas as pl
from jax.experimental.pallas import tpu as pltpu

F32 = jnp.float32
BF16 = jnp.bfloat16

D_MODEL = 2048
DEPTH = 4
PAST_LEN = 4096
HEAD = 128
N_HEADS = 8
B_QK_HEADS = 4
CONV_W = 4
BRANCH_W = N_HEADS * HEAD
B_CONV_CH = 2 * B_QK_HEADS * HEAD + N_HEADS * HEAD
ROPE_BASE = 10000.0
N_BRANCH = 3
N_EXPERTS = 64
N_GROUPS = 8
TOPK_GROUPS = 4
TOP_K = 6
EXPERT_FF = 512
ROUTE_SCALE = 2.5
MOE_BLOCK = 256
PLE_DIM = 256
DN_ALPHA = (2.0 * DEPTH) ** 0.25
LN_EPS = 1e-5
NORM_EPS = 1e-6

HGRN_SUB = 16
GDN_CHUNK = 64
GDN_APPLY_PASSES = (3, 1, 1, 1, 1, 1)
GDN_SQUARE_PASSES = (3, 1, 1, 1, 1)
ACG_COLS = 4 * BRANCH_W + 4 * BRANCH_W + N_BRANCH * D_MODEL
GATE_COL0 = 8 * BRANCH_W
B_COLS = 3200
B_SMALL_COL = 3072

VMEM_LIMIT = 56 * 1024 * 1024


def _params(*sem):
    return pltpu.CompilerParams(dimension_semantics=sem, vmem_limit_bytes=VMEM_LIMIT)


def _bdot(a, b):
    return jnp.dot(a.astype(BF16), b.astype(BF16), preferred_element_type=F32)


def _bdot_nt(a, b):
    return lax.dot_general(a.astype(BF16), b.astype(BF16), (((1,), (1,)), ((), ())),
                           preferred_element_type=F32)


def _bdot_tn(a, b):
    return lax.dot_general(a.astype(BF16), b.astype(BF16), (((0,), (0,)), ((), ())),
                           preferred_element_type=F32)


def _split2(x):
    hi = x.astype(BF16)
    lo = (x - hi.astype(F32)).astype(BF16)
    return hi, lo


def _split3(x):
    hi = x.astype(BF16)
    r = x - hi.astype(F32)
    mid = r.astype(BF16)
    lo = (r - mid.astype(F32)).astype(BF16)
    return hi, mid, lo


def _mask_dot(mask_bf, x):
    hi, mid, lo = _split3(x)
    d = functools.partial(jnp.dot, preferred_element_type=F32)
    return d(mask_bf, hi) + d(mask_bf, mid) + d(mask_bf, lo)


def _dot_x3(a, b):
    ah, al = _split2(a)
    bh, bl = _split2(b)
    d = functools.partial(jnp.dot, preferred_element_type=F32)
    return d(ah, bh) + d(ah, bl) + d(al, bh)


def _dot_p(a, b, passes):
    return _dot_x3(a, b) if passes == 3 else _bdot(a, b)


def _sigmoid(x):
    return 1.0 / (1.0 + jnp.exp(-x))


def _silu(x):
    return x * _sigmoid(x)


def _softplus(x):
    return jnp.maximum(x, 0.0) + jnp.log1p(jnp.exp(-jnp.abs(x)))


def _log_sigmoid(x):
    return jnp.minimum(x, 0.0) - jnp.log1p(jnp.exp(-jnp.abs(x)))


def _mm_kernel(x_ref, w_ref, o_ref):
    o_ref[...] = jnp.dot(x_ref[...], w_ref[...], preferred_element_type=F32).astype(o_ref.dtype)


def _matmul(x, w, tm, tn):
    m, k = x.shape
    n = w.shape[1]
    return pl.pallas_call(
        _mm_kernel,
        grid=(m // tm, n // tn),
        in_specs=[pl.BlockSpec((tm, k), lambda i, j: (i, 0)),
                  pl.BlockSpec((k, tn), lambda i, j: (0, j))],
        out_specs=pl.BlockSpec((tm, tn), lambda i, j: (i, j)),
        out_shape=jax.ShapeDtypeStruct((m, n), F32),
        compiler_params=_params("parallel", "parallel"),
        name="inproj_matmul",
    )(x, w)


def _hgrn_kernel(q_ref, f_ref, i_ref, g_ref, llb_ref, l1m_ref, oml_ref, nw_ref, s0_ref,
                 o_ref, sout_ref, st_ref, *, tc):
    c = pl.program_id(2)

    @pl.when(c == 0)
    def _():
        st_ref[...] = s0_ref[0, 0]

    nb = tc // HGRN_SUB
    z = f_ref[...]
    bterm = l1m_ref[...] + _log_sigmoid(z)
    a = llb_ref[...]
    log_f = jnp.maximum(a, bterm) + jnp.log1p(jnp.exp(-jnp.abs(a - bterm)))
    key = oml_ref[...] * (1.0 / (1.0 + jnp.exp(z)))
    r = lax.broadcasted_iota(jnp.int32, (tc, tc), 0)
    s = lax.broadcasted_iota(jnp.int32, (tc, tc), 1)
    seg_mask = jnp.where((r // HGRN_SUB == s // HGRN_SUB) & (s <= r), 1.0, 0.0).astype(BF16)
    cum = _mask_dot(seg_mask, log_f)

    shp = (nb, HGRN_SUB, HEAD)
    cum3, key3 = cum.reshape(shp), key.reshape(shp)
    q3, v3 = q_ref[...].reshape(shp), i_ref[...].reshape(shp)
    rows = lax.broadcasted_iota(jnp.int32, shp, 1)
    parts = []
    for si in range(HGRN_SUB):
        arg = jnp.where(rows >= si, cum3 - cum3[:, si:si + 1, :], -1e30)
        parts.append((q3 * (key3[:, si:si + 1, :] * jnp.exp(arg))).reshape(tc, HEAD))
    p_hi, p_lo = _split2(jnp.concatenate(parts, axis=0))
    ones_bf = jnp.ones((HEAD, HEAD), BF16)
    att = (jnp.dot(p_hi, ones_bf, preferred_element_type=F32)
           + jnp.dot(p_lo, ones_bf, preferred_element_type=F32))
    o = jnp.zeros((tc, HEAD), F32)
    for si in range(HGRN_SUB):
        vs = jnp.broadcast_to(v3[:, si:si + 1, :], shp).reshape(tc, HEAD)
        o = o + att[si * tc:(si + 1) * tc, :] * vs
    last3 = cum3[:, HGRN_SUB - 1:HGRN_SUB, :]
    qtil = q3 * jnp.exp(cum3)
    ktil = key3 * jnp.exp(last3 - cum3)
    elast = jnp.exp(last3)
    st = st_ref[...]
    outs = []
    for j in range(nb):
        outs.append(o[j * HGRN_SUB:(j + 1) * HGRN_SUB, :] + _bdot_nt(qtil[j], st))
        st = elast[j] * st + _bdot_tn(v3[j], ktil[j])
    st_ref[...] = st

    o = jnp.concatenate(outs, axis=0)
    o = o * lax.rsqrt(jnp.mean(o * o, axis=-1, keepdims=True) + NORM_EPS) * nw_ref[...]
    o_ref[...] = (o * _silu(g_ref[...])).astype(o_ref.dtype)

    @pl.when(c == pl.num_programs(2) - 1)
    def _():
        sout_ref[0, 0] = st_ref[...]


def _hgrn(acg, llb, l1m, oml, nw, s0t, b, t):
    tc = min(t, 256)
    nt = t // tc
    row = lambda bi, h, c: bi * nt + c
    col_spec = lambda part: pl.BlockSpec((tc, HEAD), lambda bi, h, c: (row(bi, h, c), part * N_HEADS + h))
    vec_spec = pl.BlockSpec((1, HEAD), lambda bi, h, c: (0, h))
    st_spec = pl.BlockSpec((1, 1, HEAD, HEAD), lambda bi, h, c: (bi, h, 0, 0))
    return pl.pallas_call(
        functools.partial(_hgrn_kernel, tc=tc),
        grid=(b, N_HEADS, nt),
        in_specs=[col_spec(0), col_spec(1), col_spec(2), col_spec(3),
                  vec_spec, vec_spec, vec_spec, pl.BlockSpec((1, HEAD), lambda bi, h, c: (0, 0)), st_spec],
        out_specs=[pl.BlockSpec((tc, HEAD), lambda bi, h, c: (row(bi, h, c), h)), st_spec],
        out_shape=[jax.ShapeDtypeStruct((b * t, BRANCH_W), BF16),
                   jax.ShapeDtypeStruct((b, N_HEADS, HEAD, HEAD), F32)],
        scratch_shapes=[pltpu.VMEM((HEAD, HEAD), F32)],
        compiler_params=_params("parallel", "parallel", "arbitrary"),
        name="hgrn2",
    )(acg, acg, acg, acg, llb, l1m, oml, nw, s0t)


def _conv_kernel(u_ref, w_ref, buf_ref, o_ref, tail_ref, *, tc):
    c = pl.program_id(1)

    @pl.when(c == 0)
    def _():
        tail_ref[...] = buf_ref[0]

    u = u_ref[...]
    ext = jnp.concatenate([tail_ref[...], u], axis=0)
    acc = u * w_ref[CONV_W - 1:CONV_W, :]
    for j in range(CONV_W - 1):
        acc = acc + ext[5 + j:5 + j + tc, :] * w_ref[j:j + 1, :]
    tail_ref[...] = u[tc - 8:, :]
    y = _silu(acc)
    nqk = B_QK_HEADS * HEAD
    for hh in range(2 * B_QK_HEADS):
        blk = y[:, hh * HEAD:(hh + 1) * HEAD]
        nrm = blk * lax.rsqrt(jnp.sum(blk * blk, axis=-1, keepdims=True) + NORM_EPS)
        if hh < B_QK_HEADS:
            nrm = nrm * (HEAD ** -0.5)
        o_ref[:, hh * HEAD:(hh + 1) * HEAD] = nrm
    o_ref[:, 2 * nqk:] = y[:, 2 * nqk:]


def _gdn_conv(yb, w, buf8, b, t):
    tc = min(t, 256)
    nt = t // tc
    return pl.pallas_call(
        functools.partial(_conv_kernel, tc=tc),
        grid=(b, nt),
        in_specs=[pl.BlockSpec((tc, B_CONV_CH), lambda bi, c: (bi * nt + c, 0)),
                  pl.BlockSpec((CONV_W, B_CONV_CH), lambda bi, c: (0, 0)),
                  pl.BlockSpec((1, 8, B_CONV_CH), lambda bi, c: (bi, 0, 0))],
        out_specs=pl.BlockSpec((tc, B_CONV_CH), lambda bi, c: (bi * nt + c, 0)),
        out_shape=jax.ShapeDtypeStruct((b * t, B_CONV_CH), F32),
        scratch_shapes=[pltpu.VMEM((8, B_CONV_CH), F32)],
        compiler_params=_params("parallel", "arbitrary"),
        name="gdn_conv",
    )(yb, w, buf8)


def _gdn_kernel(qkv_ref, z_ref, sm_ref, alog_ref, dtb_ref, nw_ref, s0_ref, o_ref, sout_ref, st_ref):
    c = pl.program_id(1)
    cs = GDN_CHUNK

    @pl.when(c == 0)
    def _():
        st_ref[...] = s0_ref[0]

    sm = sm_ref[...]
    beta_all = _sigmoid(sm)
    g_all = -jnp.exp(alog_ref[...]) * _softplus(sm + dtb_ref[...])

    r = lax.broadcasted_iota(jnp.int32, (cs, cs), 0)
    s = lax.broadcasted_iota(jnp.int32, (cs, cs), 1)
    lower = r >= s
    strict = r > s
    tril_bf = jnp.where(lower, 1.0, 0.0).astype(BF16)
    triu_bf = jnp.where(r <= s, 1.0, 0.0).astype(BF16)
    above_bf = jnp.where(r < s, 1.0, 0.0).astype(BF16)
    nw = nw_ref[...]
    cum_all = _mask_dot(tril_bf, g_all)
    rem_all = _mask_dot(above_bf, g_all)
    d_tn = lambda a: lax.dot_general(a, triu_bf, (((0,), (0,)), ((), ())), preferred_element_type=F32)
    g_hi, g_mid, g_lo = _split3(g_all)
    cum_rows = d_tn(g_hi) + d_tn(g_mid) + d_tn(g_lo)

    rep = N_HEADS // B_QK_HEADS
    heads = range(N_HEADS)
    q = [qkv_ref[:, qh * HEAD:(qh + 1) * HEAD] for qh in range(B_QK_HEADS)]
    k = [qkv_ref[:, (B_QK_HEADS + qh) * HEAD:(B_QK_HEADS + qh + 1) * HEAD] for qh in range(B_QK_HEADS)]
    kk = [_bdot_nt(k[qh], k[qh]) for qh in range(B_QK_HEADS)]
    qk = [_bdot_nt(q[qh], k[qh]) for qh in range(B_QK_HEADS)]
    cum, rem, dec, x, y = [], [], [], [], []
    for h in heads:
        col = N_HEADS + h
        v = qkv_ref[:, (2 * B_QK_HEADS + h) * HEAD:(2 * B_QK_HEADS + h + 1) * HEAD]
        beta = beta_all[:, h:h + 1]
        cum.append(jnp.broadcast_to(cum_all[:, col:col + 1], (cs, HEAD)))
        rem.append(jnp.broadcast_to(rem_all[:, col:col + 1], (cs, HEAD)))
        seg = cum_all[:, col:col + 1] - cum_rows[col:col + 1, :]
        dec.append(jnp.where(lower, jnp.exp(jnp.where(lower, seg, 0.0)), 0.0))
        x.append(jnp.where(strict, -(kk[h // rep] * beta) * dec[h], 0.0))
        y.append(jnp.concatenate([v * beta, k[h // rep] * beta * jnp.exp(cum[h])], axis=1))
    for lvl in range(len(GDN_APPLY_PASSES)):
        y = [y[h] + _dot_p(x[h], y[h], GDN_APPLY_PASSES[lvl]) for h in heads]
        if lvl < len(GDN_SQUARE_PASSES):
            x = [_dot_p(x[h], x[h], GDN_SQUARE_PASSES[lvl]) for h in heads]
    st = [st_ref[h] for h in heads]
    u = [y[h][:, :HEAD] - _bdot(y[h][:, HEAD:], st[h]) for h in heads]
    o = [_bdot(q[h // rep] * jnp.exp(cum[h]), st[h]) + _bdot(qk[h // rep] * dec[h], u[h]) for h in heads]
    for h in heads:
        st_ref[h] = jnp.exp(cum[h][cs - 1:cs, :]) * st[h] + _bdot_tn(k[h // rep] * jnp.exp(rem[h]), u[h])
    for h in heads:
        on = o[h] * lax.rsqrt(jnp.mean(o[h] * o[h], axis=-1, keepdims=True) + NORM_EPS) * nw
        zg = z_ref[:, h * HEAD:(h + 1) * HEAD]
        o_ref[:, h * HEAD:(h + 1) * HEAD] = (on * _silu(zg)).astype(o_ref.dtype)

    @pl.when(c == pl.num_programs(1) - 1)
    def _():
        sout_ref[0] = st_ref[...]


def _gdn(qkvc, yb, alog_row, dtb_row, nw, s0, b, t):
    cs = GDN_CHUNK
    nt = t // cs
    st_spec = pl.BlockSpec((1, N_HEADS, HEAD, HEAD), lambda bi, c: (bi, 0, 0, 0))
    row_spec = pl.BlockSpec((1, HEAD), lambda bi, c: (0, 0))
    return pl.pallas_call(
        _gdn_kernel,
        grid=(b, nt),
        in_specs=[pl.BlockSpec((cs, B_CONV_CH), lambda bi, c: (bi * nt + c, 0)),
                  pl.BlockSpec((cs, BRANCH_W), lambda bi, c: (bi * nt + c, B_CONV_CH // BRANCH_W)),
                  pl.BlockSpec((cs, HEAD), lambda bi, c: (bi * nt + c, B_SMALL_COL // HEAD)),
                  row_spec, row_spec, row_spec, st_spec],
        out_specs=[pl.BlockSpec((cs, BRANCH_W), lambda bi, c: (bi * nt + c, 0)), st_spec],
        out_shape=[jax.ShapeDtypeStruct((b * t, BRANCH_W), BF16),
                   jax.ShapeDtypeStruct((b, N_HEADS, HEAD, HEAD), F32)],
        scratch_shapes=[pltpu.VMEM((N_HEADS, HEAD, HEAD), F32)],
        compiler_params=_params("parallel", "arbitrary"),
        name="gated_delta",
    )(qkvc, yb, yb, alog_row, dtb_row, nw, s0)


def _ret_kernel(lg_ref, q_ref, k_ref, v_ref, g_ref, cos_ref, sin_ref, nw_ref, s0_ref,
                o_ref, sout_ref, st_ref, *, cs):
    h = pl.program_id(1)
    c = pl.program_id(2)

    @pl.when(c == 0)
    def _():
        st_ref[...] = s0_ref[0, 0]

    lg = lg_ref[h]
    cos = cos_ref[...]
    sin = sin_ref[...]

    def rot(x):
        return x * cos + pltpu.roll(x, HEAD // 2, 1) * sin

    q = rot(q_ref[...])
    k = rot(k_ref[...]) * (HEAD ** -0.5)
    v = v_ref[...]
    idx = lax.broadcasted_iota(jnp.int32, (cs, HEAD), 0).astype(F32)
    q_dec = jnp.exp((idx + 1.0) * lg)
    k_dec = jnp.exp((cs - 1.0 - idx) * lg)
    c_dec = jnp.exp(jnp.zeros((1, HEAD), F32) + cs * lg)
    rel = (lax.broadcasted_iota(jnp.int32, (cs, cs), 0)
           - lax.broadcasted_iota(jnp.int32, (cs, cs), 1))
    intra = jnp.where(rel >= 0, jnp.exp(jnp.maximum(rel, 0).astype(F32) * lg), 0.0)
    st = st_ref[...]
    att = _bdot_nt(q, k) * intra
    o = _bdot(q, st) * q_dec + _bdot(att, v)
    st_ref[...] = c_dec * st + _bdot_tn(k * k_dec, v)
    mu = jnp.mean(o, axis=-1, keepdims=True)
    var = jnp.mean(jnp.square(o - mu), axis=-1, keepdims=True)
    o = (o - mu) * lax.rsqrt(var + NORM_EPS) * nw_ref[...]
    o_ref[...] = (o * _silu(g_ref[...])).astype(o_ref.dtype)

    @pl.when(c == pl.num_programs(2) - 1)
    def _():
        sout_ref[0, 0] = st_ref[...]


def _retention(acg, lg, cos, sin, nw, s0, b, t):
    cs = min(t, 512)
    nt = t // cs
    col_spec = lambda part: pl.BlockSpec(
        (cs, HEAD), lambda bi, h, c, lg_ref: (bi * nt + c, (4 + part) * N_HEADS + h))
    tab_spec = pl.BlockSpec((cs, HEAD), lambda bi, h, c, lg_ref: (c, 0))
    st_spec = pl.BlockSpec((1, 1, HEAD, HEAD), lambda bi, h, c, lg_ref: (bi, h, 0, 0))
    return pl.pallas_call(
        functools.partial(_ret_kernel, cs=cs),
        grid_spec=pltpu.PrefetchScalarGridSpec(
            num_scalar_prefetch=1,
            grid=(b, N_HEADS, nt),
            in_specs=[col_spec(0), col_spec(1), col_spec(2), col_spec(3), tab_spec, tab_spec,
                      pl.BlockSpec((1, HEAD), lambda bi, h, c, lg_ref: (0, 0)), st_spec],
            out_specs=[pl.BlockSpec((cs, HEAD), lambda bi, h, c, lg_ref: (bi * nt + c, h)), st_spec],
            scratch_shapes=[pltpu.VMEM((HEAD, HEAD), F32)]),
        out_shape=[jax.ShapeDtypeStruct((b * t, BRANCH_W), BF16),
                   jax.ShapeDtypeStruct((b, N_HEADS, HEAD, HEAD), F32)],
        compiler_params=_params("parallel", "parallel", "arbitrary"),
        name="retention",
    )(lg, acg, acg, acg, acg, cos, sin, nw, s0)


def _merge_kernel(oa_ref, ob_ref, oc_ref, wb_ref, ga_ref, gb_ref, gc_ref, o_ref):
    acc = _sigmoid(ga_ref[...]) * jnp.dot(oa_ref[...], wb_ref[0], preferred_element_type=F32)
    acc += _sigmoid(gb_ref[...]) * jnp.dot(ob_ref[...], wb_ref[1], preferred_element_type=F32)
    acc += _sigmoid(gc_ref[...]) * jnp.dot(oc_ref[...], wb_ref[2], preferred_element_type=F32)
    o_ref[...] = acc.astype(o_ref.dtype)


def _merge(oa, ob, oc, wb, acg, tm):
    n = oa.shape[0]
    tn = 512
    o_spec = pl.BlockSpec((tm, BRANCH_W), lambda i, j: (i, 0))
    gate_spec = lambda br: pl.BlockSpec(
        (tm, tn), lambda i, j: (i, (GATE_COL0 + br * D_MODEL) // tn + j))
    return pl.pallas_call(
        _merge_kernel,
        grid=(n // tm, D_MODEL // tn),
        in_specs=[o_spec, o_spec, o_spec,
                  pl.BlockSpec((N_BRANCH, BRANCH_W, tn), lambda i, j: (0, 0, j)),
                  gate_spec(0), gate_spec(1), gate_spec(2)],
        out_specs=pl.BlockSpec((tm, tn), lambda i, j: (i, j)),
        out_shape=jax.ShapeDtypeStruct((n, D_MODEL), BF16),
        compiler_params=_params("parallel", "parallel"),
        name="branch_merge",
    )(oa, ob, oc, wb, acg, acg, acg)


def _layer_norm(y, g, b):
    mu = jnp.mean(y, axis=-1, keepdims=True)
    var = jnp.mean(jnp.square(y - mu), axis=-1, keepdims=True)
    return (y - mu) * lax.rsqrt(var + LN_EPS) * g + b


def _outproj_kernel(m_ref, x_ref, w_ref, g_ref, b_ref, wrh_ref, wrl_ref, xo_ref, xb_ref, sc_ref):
    h = jnp.dot(m_ref[...], w_ref[...], preferred_element_type=F32)
    x1 = _layer_norm(DN_ALPHA * x_ref[...] + h, g_ref[...], b_ref[...])
    xo_ref[...] = x1
    xh, xl = _split2(x1)
    xb_ref[...] = xh
    d = lambda a, b_: lax.dot_general(a, b_, (((1,), (1,)), ((), ())), preferred_element_type=F32)
    logits = d(wrh_ref[...], xh) + d(wrh_ref[...], xl) + d(wrl_ref[...], xh)
    sc_ref[...] = _sigmoid(logits)


def _outproj(merged, x, w_out, g, b, wr_hi, wr_lo, tm):
    n = x.shape[0]
    row = pl.BlockSpec((tm, D_MODEL), lambda i: (i, 0))
    vec = pl.BlockSpec((1, D_MODEL), lambda i: (0, 0))
    wr = pl.BlockSpec((N_EXPERTS, D_MODEL), lambda i: (0, 0))
    return pl.pallas_call(
        _outproj_kernel,
        grid=(n // tm,),
        in_specs=[row, row, pl.BlockSpec((D_MODEL, D_MODEL), lambda i: (0, 0)), vec, vec, wr, wr],
        out_specs=[row, row, pl.BlockSpec((N_EXPERTS, tm), lambda i: (0, i))],
        out_shape=[jax.ShapeDtypeStruct((n, D_MODEL), F32), jax.ShapeDtypeStruct((n, D_MODEL), BF16),
                   jax.ShapeDtypeStruct((N_EXPERTS, n), F32)],
        compiler_params=_params("parallel"),
        name="outproj_ln_router",
    )(merged, x, w_out, g, b, wr_hi, wr_lo)


def _route_kernel(sc_ref, bias_ref, e_ref, pos_ref, w_ref, cnt_ref, carry_ref, *, tm):
    i = pl.program_id(0)
    per_group = N_EXPERTS // N_GROUPS

    @pl.when(i == 0)
    def _():
        carry_ref[...] = jnp.zeros_like(carry_ref)

    s = sc_ref[...]
    c3 = (s + bias_ref[...]).reshape(N_GROUPS, per_group, tm)
    midx = lax.broadcasted_iota(jnp.int32, (N_GROUPS, per_group, tm), 1)
    gidx = lax.broadcasted_iota(jnp.int32, (N_GROUPS, per_group, tm), 0)

    def beats(other, mine, other_first):
        return jnp.where((other > mine) | ((other == mine) & other_first), 1, 0)

    r1 = jnp.zeros((N_GROUPS, per_group, tm), jnp.int32)
    for m in range(per_group):
        r1 = r1 + beats(c3[:, m:m + 1, :], c3, m < midx)
    gs = jnp.sum(jnp.where(r1 < 2, c3, 0.0), axis=1, keepdims=True)
    gs = jnp.broadcast_to(gs, (N_GROUPS, per_group, tm))
    r2 = jnp.zeros((N_GROUPS, per_group, tm), jnp.int32)
    for g in range(N_GROUPS):
        r2 = r2 + beats(gs[g:g + 1], gs, g < gidx)
    masked = jnp.where(r2 < TOPK_GROUPS, c3, -jnp.inf).reshape(N_EXPERTS, tm)
    eidx = lax.broadcasted_iota(jnp.int32, (N_EXPERTS, tm), 0)
    r3 = jnp.zeros((N_EXPERTS, tm), jnp.int32)
    for e in range(N_EXPERTS):
        r3 = r3 + beats(masked[e:e + 1, :], masked, e < eidx)
    sel = r3 < TOP_K
    w = jnp.where(sel, s, 0.0)
    w = w / jnp.sum(w, axis=0, keepdims=True) * ROUTE_SCALE
    sel_bf = jnp.where(sel, 1.0, 0.0).astype(BF16)
    tr = lax.broadcasted_iota(jnp.int32, (tm, tm), 0)
    tc_ = lax.broadcasted_iota(jnp.int32, (tm, tm), 1)
    before = jnp.where(tr < tc_, 1.0, 0.0).astype(BF16)
    carry = carry_ref[...]
    pos = (jnp.dot(sel_bf, before, preferred_element_type=F32)
           + jnp.concatenate([carry] * (tm // HEAD), axis=1))
    carry_ref[...] = carry + jnp.dot(sel_bf, jnp.ones((tm, HEAD), BF16), preferred_element_type=F32)
    er = lax.broadcasted_iota(jnp.int32, (N_EXPERTS, N_EXPERTS), 0)
    ec = lax.broadcasted_iota(jnp.int32, (N_EXPERTS, N_EXPERTS), 1)
    lower_e = jnp.where(ec < er, 1.0, 0.0).astype(BF16)
    jidx = jnp.dot(lower_e, sel_bf, preferred_element_type=F32)
    eidx_f = eidx.astype(F32)
    e_rows, p_rows, w_rows = [], [], []
    for j in range(TOP_K):
        oh = sel & (jidx == float(j))
        e_rows.append(jnp.sum(jnp.where(oh, eidx_f, 0.0), axis=0, keepdims=True))
        p_rows.append(jnp.sum(jnp.where(oh, pos, 0.0), axis=0, keepdims=True))
        w_rows.append(jnp.sum(jnp.where(oh, w, 0.0), axis=0, keepdims=True))
    zpad = [jnp.zeros((8 - TOP_K, tm), F32)]
    e_ref[...] = jnp.concatenate(e_rows + zpad, axis=0).astype(jnp.int32)
    pos_ref[...] = jnp.concatenate(p_rows + zpad, axis=0).astype(jnp.int32)
    w_ref[...] = jnp.concatenate(w_rows + zpad, axis=0)

    @pl.when(i == pl.num_programs(0) - 1)
    def _():
        cnt_ref[...] = carry_ref[...]


def _route(scores_t, bias_b, tm):
    n = scores_t.shape[1]
    blk = pl.BlockSpec((8, tm), lambda i: (0, i))
    return pl.pallas_call(
        functools.partial(_route_kernel, tm=tm),
        grid=(n // tm,),
        in_specs=[pl.BlockSpec((N_EXPERTS, tm), lambda i: (0, i)),
                  pl.BlockSpec((N_EXPERTS, tm), lambda i: (0, 0))],
        out_specs=[blk, blk, blk, pl.BlockSpec((N_EXPERTS, HEAD), lambda i: (0, 0))],
        out_shape=[jax.ShapeDtypeStruct((8, n), jnp.int32), jax.ShapeDtypeStruct((8, n), jnp.int32),
                   jax.ShapeDtypeStruct((8, n), F32), jax.ShapeDtypeStruct((N_EXPERTS, HEAD), F32)],
        scratch_shapes=[pltpu.VMEM((N_EXPERTS, HEAD), F32)],
        compiler_params=_params("arbitrary"),
        name="route_topk",
    )(scores_t, bias_b)


def _expert_kernel(be_ref, s_m2, s_m1, s_cur, s_p1, x_hbm, wg_ref, wu_ref, wd_ref,
                   out_hbm, xbuf, ybuf, sem_in, sem_out, *, n_tok):
    i = pl.program_id(0)
    last = pl.num_programs(0) - 1
    slot = i % 2
    other = 1 - slot

    def in_copy(slots_ref, buf, k):
        tok = slots_ref[0, 0, k] >> 3
        return pltpu.make_async_copy(x_hbm.at[pl.ds(tok, 1)], xbuf.at[buf, pl.ds(k, 1)], sem_in.at[buf, k])

    def out_copy(slots_ref, buf, k):
        v = slots_ref[0, 0, k]
        dst = (v & 7) * n_tok + (v >> 3)
        return pltpu.make_async_copy(ybuf.at[buf, pl.ds(k, 1)], out_hbm.at[pl.ds(dst, 1)],
                                     sem_out.at[buf, k])

    def rolled(fn):
        def body(k, carry):
            fn(k)
            return carry
        lax.fori_loop(0, MOE_BLOCK, body, 0, unroll=8)

    @pl.when(i == 0)
    def _():
        rolled(lambda k: in_copy(s_cur, 0, k).start())

    rolled(lambda k: in_copy(s_cur, slot, k).wait())

    @pl.when(i >= 2)
    def _():
        rolled(lambda k: out_copy(s_m2, slot, k).wait())

    def block(gather_next, scatter_prev):
        xb = xbuf[slot].astype(BF16)
        for k in range(MOE_BLOCK):
            if gather_next:
                in_copy(s_p1, other, k).start()
            if scatter_prev:
                out_copy(s_m1, other, k).start()
        hg = jnp.dot(xb, wg_ref[0], preferred_element_type=F32)
        hu = jnp.dot(xb, wu_ref[0], preferred_element_type=F32)
        hb = (_silu(hg) * hu).astype(BF16)
        ybuf[slot] = jnp.dot(hb, wd_ref[0], preferred_element_type=F32)

    @pl.when(i == 0)
    def _():
        block(True, False)

    @pl.when((i > 0) & (i < last))
    def _():
        block(True, True)

    @pl.when(i == last)
    def _():
        block(False, True)
        rolled(lambda k: out_copy(s_cur, slot, k).start())
        rolled(lambda k: out_copy(s_m1, other, k).wait())
        rolled(lambda k: out_copy(s_cur, slot, k).wait())


def _experts(x1, block_e, slots, wg, wu, wd):
    n = x1.shape[0]
    n_blocks = block_e.shape[0]
    slot_spec = lambda off: pl.BlockSpec(
        (1, 1, MOE_BLOCK), lambda i, be: (jnp.clip(i + off, 0, n_blocks - 1), 0, 0), memory_space=pltpu.SMEM)
    return pl.pallas_call(
        functools.partial(_expert_kernel, n_tok=n),
        grid_spec=pltpu.PrefetchScalarGridSpec(
            num_scalar_prefetch=1,
            grid=(n_blocks,),
            in_specs=[slot_spec(-2), slot_spec(-1), slot_spec(0), slot_spec(1),
                      pl.BlockSpec(memory_space=pl.ANY),
                      pl.BlockSpec((1, D_MODEL, EXPERT_FF), lambda i, be: (be[i], 0, 0)),
                      pl.BlockSpec((1, D_MODEL, EXPERT_FF), lambda i, be: (be[i], 0, 0)),
                      pl.BlockSpec((1, EXPERT_FF, D_MODEL), lambda i, be: (be[i], 0, 0))],
            out_specs=pl.BlockSpec(memory_space=pl.ANY),
            scratch_shapes=[pltpu.VMEM((2, MOE_BLOCK, D_MODEL), F32), pltpu.VMEM((2, MOE_BLOCK, D_MODEL), F32),
                            pltpu.SemaphoreType.DMA((2, MOE_BLOCK)), pltpu.SemaphoreType.DMA((2, MOE_BLOCK))]),
        out_shape=jax.ShapeDtypeStruct((TOP_K * n + 2 * MOE_BLOCK, D_MODEL), F32),
        compiler_params=_params("arbitrary"),
        name="routed_experts",
    )(block_e, slots, slots, slots, slots, x1, wg, wu, wd)


def _slot_tables(e6, pos6, counts):
    n = e6.shape[1]
    n_assign = n * TOP_K
    n_blocks = -(-n_assign // MOE_BLOCK) + N_EXPERTS
    padded = (counts + MOE_BLOCK - 1) // MOE_BLOCK * MOE_BLOCK
    pad_end = jnp.cumsum(padded)
    pad_start = pad_end - padded
    onehot = e6[:TOP_K, :, None] == jnp.arange(N_EXPERTS, dtype=jnp.int32)
    dest = jnp.sum(jnp.where(onehot, pad_start, 0), axis=-1) + pos6[:TOP_K]
    packed = (jnp.arange(n, dtype=jnp.int32)[None, :] * 8
              + jnp.arange(TOP_K, dtype=jnp.int32)[:, None])
    pad_slots = (jnp.arange(n_blocks * MOE_BLOCK, dtype=jnp.int32) % (2 * MOE_BLOCK)) * 8 + TOP_K
    slots = pad_slots.at[dest.reshape(-1)].set(packed.reshape(-1), unique_indices=True)
    blk_start = jnp.arange(n_blocks, dtype=jnp.int32) * MOE_BLOCK
    block_e = jnp.minimum(jnp.sum((pad_end[None, :] <= blk_start[:, None]).astype(jnp.int32), axis=1),
                          N_EXPERTS - 1)
    return block_e, slots.reshape(n_blocks, 1, MOE_BLOCK)


def _combine_kernel(*refs):
    r_refs = refs[:TOP_K]
    w6_ref, x_ref, xb_ref, wg_ref, wu_ref, wd_ref, g_ref, b_ref, xo_ref, xob_ref = refs[TOP_K:]
    xb = xb_ref[...]
    hb = (_silu(jnp.dot(xb, wg_ref[...], preferred_element_type=F32))
          * jnp.dot(xb, wu_ref[...], preferred_element_type=F32)).astype(BF16)
    shared = jnp.dot(hb, wd_ref[...], preferred_element_type=F32)
    w6 = w6_ref[...]
    routed = r_refs[0][...] * w6[:, 0:1]
    for j in range(1, TOP_K):
        routed = routed + r_refs[j][...] * w6[:, j:j + 1]
    x2 = _layer_norm(DN_ALPHA * x_ref[...] + (routed + shared), g_ref[...], b_ref[...])
    xo_ref[...] = x2
    xob_ref[...] = x2.astype(BF16)


def _combine(routed, w6, x1, x1b, wg, wu, wd, g, b, tm):
    n = x1.shape[0]
    nt = n // tm
    row = pl.BlockSpec((tm, D_MODEL), lambda i: (i, 0))
    vec = pl.BlockSpec((1, D_MODEL), lambda i: (0, 0))
    w_in = pl.BlockSpec((D_MODEL, EXPERT_FF), lambda i: (0, 0))
    r_specs = [pl.BlockSpec((tm, D_MODEL), functools.partial(lambda i, j: (j * nt + i, 0), j=j))
               for j in range(TOP_K)]
    return pl.pallas_call(
        _combine_kernel,
        grid=(nt,),
        in_specs=r_specs + [pl.BlockSpec((tm, 8), lambda i: (i, 0)), row, row, w_in, w_in,
                            pl.BlockSpec((EXPERT_FF, D_MODEL), lambda i: (0, 0)), vec, vec],
        out_specs=[row, row],
        out_shape=[jax.ShapeDtypeStruct((n, D_MODEL), F32), jax.ShapeDtypeStruct((n, D_MODEL), BF16)],
        compiler_params=_params("parallel"),
        name="combine_shared_ln",
    )(*([routed] * TOP_K), w6, x1, x1b, wg, wu, wd, g, b)


def _ple_kernel(x_ref, xb_ref, p_ref, wg_ref, wp_ref, xo_ref, xob_ref):
    gate = _sigmoid(jnp.dot(xb_ref[...], wg_ref[...], preferred_element_type=F32))
    pe = jnp.dot(p_ref[...].astype(BF16), wp_ref[...], preferred_element_type=F32)
    x3 = x_ref[...] + pe * gate
    xo_ref[...] = x3
    xob_ref[...] = x3.astype(BF16)


def _ple(x2, x2b, p, wg, wp, tm):
    n = x2.shape[0]
    row = pl.BlockSpec((tm, D_MODEL), lambda i: (i, 0))
    return pl.pallas_call(
        _ple_kernel,
        grid=(n // tm,),
        in_specs=[row, row, pl.BlockSpec((tm, PLE_DIM), lambda i: (i, 0)),
                  pl.BlockSpec((D_MODEL, D_MODEL), lambda i: (0, 0)),
                  pl.BlockSpec((PLE_DIM, D_MODEL), lambda i: (0, 0))],
        out_specs=[row, row],
        out_shape=[jax.ShapeDtypeStruct((n, D_MODEL), F32), jax.ShapeDtypeStruct((n, D_MODEL), BF16)],
        compiler_params=_params("parallel"),
        name="ple",
    )(x2, x2b, p, wg, wp)


def _prep_weights(w_in, hgrn_lb_logits, gdn_a_log, gdn_dt_bias, w_branch, w_out, w_router,
                  w_exp_gate, w_exp_up, w_exp_down, w_sh_gate, w_sh_up, w_sh_down, w_ple_gate, w_ple_proj):
    pts = np.cumsum([1024, 1024, 1024, 1024, 512, 512, 1024, 1024, 8, 8, 1024, 1024, 1024, 1024]).tolist()
    a_end, b_end, c_end = pts[3], pts[9], pts[13]
    w_acg = jnp.concatenate([w_in[:, :, :a_end], w_in[:, :, b_end:c_end], w_in[:, :, c_end:]], axis=-1)
    w_b = jnp.concatenate([w_in[:, :, a_end:b_end],
                           jnp.zeros((DEPTH, D_MODEL, B_COLS - (b_end - a_end)), w_in.dtype)], axis=-1)
    lb_p = jax.nn.softmax(hgrn_lb_logits.astype(F32), axis=0)
    lb = jnp.maximum(jnp.cumsum(lb_p, axis=0) - lb_p[:1], 0.0)
    pad_row = lambda v: jnp.zeros((DEPTH, 1, HEAD), F32).at[:, 0, N_HEADS:2 * N_HEADS].set(v.astype(F32))
    wr = jnp.swapaxes(w_router.astype(F32), -1, -2)
    wr_hi = wr.astype(BF16)
    return dict(
        w_acg=w_acg.astype(BF16), w_b=w_b.astype(BF16),
        llb=jnp.log(lb)[:, None, :], l1m=jnp.log1p(-lb)[:, None, :], oml=(1.0 - lb)[:, None, :],
        alog_row=pad_row(gdn_a_log), dtb_row=pad_row(gdn_dt_bias),
        w_branch=w_branch.astype(BF16), w_out=w_out.astype(BF16),
        wr_hi=wr_hi, wr_lo=(wr - wr_hi.astype(F32)).astype(BF16),
        wg=w_exp_gate.astype(BF16), wu=w_exp_up.astype(BF16), wd=w_exp_down.astype(BF16),
        sg=w_sh_gate.astype(BF16), su=w_sh_up.astype(BF16), sd=w_sh_down.astype(BF16),
        pg=w_ple_gate.astype(BF16), pp=w_ple_proj.astype(BF16))


def _run_trunk(x, p, pos0, states, pw, small):
    (hgrn_norm, gdn_conv, gdn_norm, ret_norm, ln1_g, ln1_b, ln2_g, ln2_b, router_bias) = small
    s_hgrn, s_gdn, s_conv, s_ret = states
    b, t, d = x.shape
    n = b * t
    tm = min(n, 512)
    xf = x.reshape(n, d)
    xb = xf.astype(BF16)
    half = HEAD // 2
    inv = ROPE_BASE ** (-jnp.arange(half, dtype=F32) / half)
    ang = (pos0 + jnp.arange(t, dtype=F32))[:, None] * inv[None, :]
    cos_t = jnp.concatenate([jnp.cos(ang), jnp.cos(ang)], axis=-1)
    sin_t = jnp.concatenate([-jnp.sin(ang), jnp.sin(ang)], axis=-1)
    lg = jnp.log1p(-jnp.exp2(-5.0 - jnp.arange(N_HEADS, dtype=F32)))
    row = lambda v: v.reshape(1, -1).astype(F32)
    new = ([], [], [], [])
    for i in range(DEPTH):
        acg = _matmul(xb, pw['w_acg'][i], min(n, 2048), 1024)
        yb = _matmul(xb, pw['w_b'][i], min(n, 2048), 640)
        o_a, st_a = _hgrn(acg, pw['llb'][i], pw['l1m'][i], pw['oml'][i], row(hgrn_norm[i]),
                          jnp.swapaxes(s_hgrn[i], -1, -2), b, t)
        buf8 = jnp.concatenate([jnp.zeros((b, 8 - (CONV_W - 1), B_CONV_CH), F32), s_conv[i]], axis=1)
        qkvc = _gdn_conv(yb, gdn_conv[i], buf8, b, t)
        o_b, st_b = _gdn(qkvc, yb, pw['alog_row'][i], pw['dtb_row'][i], row(gdn_norm[i]), s_gdn[i], b, t)
        o_c, st_c = _retention(acg, lg, cos_t, sin_t, row(ret_norm[i]), s_ret[i], b, t)
        merged = _merge(o_a, o_b, o_c, pw['w_branch'][i], acg, min(n, 1024))
        x1, x1b, scores = _outproj(merged, xf, pw['w_out'][i], row(ln1_g[i]), row(ln1_b[i]),
                                   pw['wr_hi'][i], pw['wr_lo'][i], tm)
        bias_b = jnp.broadcast_to(router_bias[i].astype(F32)[:, None], (N_EXPERTS, tm))
        e6, pos6, w6, cnt = _route(scores, bias_b, tm)
        block_e, slots = _slot_tables(e6, pos6, cnt[:, 0].astype(jnp.int32))
        routed = _experts(x1, block_e, slots, pw['wg'][i], pw['wu'][i], pw['wd'][i])
        x2, x2b = _combine(routed, w6.T, x1, x1b, pw['sg'][i], pw['su'][i],
                           pw['sd'][i], row(ln2_g[i]), row(ln2_b[i]), min(n, 128))
        xf, xb = _ple(x2, x2b, p[i].reshape(n, PLE_DIM), pw['pg'][i], pw['pp'][i], tm)
        new[0].append(jnp.swapaxes(st_a, -1, -2))
        new[1].append(st_b)
        new[2].append(yb.reshape(b, t, B_COLS)[:, t - (CONV_W - 1):, :B_CONV_CH])
        new[3].append(st_c)
    return xf.reshape(b, t, d), tuple(jnp.stack(a) for a in new)


def kernel(x_prompt, x_sample, p_prompt, p_sample, state_hgrn, state_gdn, state_gdn_conv, state_ret,
           w_in, hgrn_lb_logits, hgrn_norm, gdn_conv, gdn_a_log, gdn_dt_bias, gdn_norm, ret_norm,
           w_branch, w_out, ln1_g, ln1_b, ln2_g, ln2_b, w_router, router_bias,
           w_exp_gate, w_exp_up, w_exp_down, w_sh_gate, w_sh_up, w_sh_down, w_ple_gate, w_ple_proj):
    pw = _prep_weights(w_in, hgrn_lb_logits, gdn_a_log, gdn_dt_bias, w_branch, w_out, w_router,
                       w_exp_gate, w_exp_up, w_exp_down, w_sh_gate, w_sh_up, w_sh_down,
                       w_ple_gate, w_ple_proj)
    small = (hgrn_norm, gdn_conv, gdn_norm, ret_norm, ln1_g, ln1_b, ln2_g, ln2_b, router_bias)
    bp = x_prompt.shape[0]
    zero_states = (jnp.zeros((DEPTH, bp, N_HEADS, HEAD, HEAD), F32),
                   jnp.zeros((DEPTH, bp, N_HEADS, HEAD, HEAD), F32),
                   jnp.zeros((DEPTH, bp, CONV_W - 1, B_CONV_CH), F32),
                   jnp.zeros((DEPTH, bp, N_HEADS, HEAD, HEAD), F32))
    y_p, (hp, gp, cp, rp) = _run_trunk(x_prompt, p_prompt, 0.0, zero_states, pw, small)
    y_s, (hs, gs, cs, rs) = _run_trunk(x_sample, p_sample, float(PAST_LEN),
                                       (state_hgrn, state_gdn, state_gdn_conv, state_ret), pw, small)
    return (y_p, y_s, hp, gp, cp, rp, hs, gs, cs, rs)
```

```python
import functools
import math

import jax
import jax.numpy as jnp
import numpy as np
from jax import lax
from jax.experimental import pallas as pl
from jax.experimental.pallas import tpu as pltpu

F32 = jnp.float32
BF16 = jnp.bfloat16

D_MODEL = 2048
DEPTH = 4
PAST_LEN = 4096
HEAD = 128
N_HEADS = 8
B_QK_HEADS = 4
CONV_W = 4
BRANCH_W = N_HEADS * HEAD
B_CONV_CH = 2 * B_QK_HEADS * HEAD + N_HEADS * HEAD
ROPE_BASE = 10000.0
N_BRANCH = 3
N_EXPERTS = 64
N_GROUPS = 8
TOPK_GROUPS = 4
TOP_K = 6
EXPERT_FF = 512
ROUTE_SCALE = 2.5
MOE_BLOCK = 256
ROW_TILES = D_MODEL // HEAD
ROW_PITCH = 24
PLE_DIM = 256
DN_ALPHA = (2.0 * DEPTH) ** 0.25
LN_EPS = 1e-5
NORM_EPS = 1e-6

HGRN_SUB = 16
HGRN_HPB = 8
GDN_CHUNK = 64
GDN_APPLY_PASSES = (3, 1, 1, 1, 1, 1)
GDN_SQUARE_PASSES = (3, 1, 1, 1, 1)
ACG_COLS = 4 * BRANCH_W + 4 * BRANCH_W + N_BRANCH * D_MODEL
GATE_COL0 = 8 * BRANCH_W
B_COLS = 3200
B_SMALL_COL = 3072

VMEM_LIMIT = 56 * 1024 * 1024


def _params(*sem):
    return pltpu.CompilerParams(dimension_semantics=sem, vmem_limit_bytes=VMEM_LIMIT)


def _bdot(a, b):
    return jnp.dot(a.astype(BF16), b.astype(BF16), preferred_element_type=F32)


def _bdot_nt(a, b):
    return lax.dot_general(a.astype(BF16), b.astype(BF16), (((1,), (1,)), ((), ())),
                           preferred_element_type=F32)


def _bdot_tn(a, b):
    return lax.dot_general(a.astype(BF16), b.astype(BF16), (((0,), (0,)), ((), ())),
                           preferred_element_type=F32)


def _split2(x):
    hi = x.astype(BF16)
    lo = (x - hi.astype(F32)).astype(BF16)
    return hi, lo


def _split3(x):
    hi = x.astype(BF16)
    r = x - hi.astype(F32)
    mid = r.astype(BF16)
    lo = (r - mid.astype(F32)).astype(BF16)
    return hi, mid, lo


def _mask_dot(mask_bf, x):
    hi, mid, lo = _split3(x)
    d = functools.partial(jnp.dot, preferred_element_type=F32)
    return d(mask_bf, hi) + d(mask_bf, mid) + d(mask_bf, lo)


def _dot_x3(a, b):
    ah, al = _split2(a)
    bh, bl = _split2(b)
    d = functools.partial(jnp.dot, preferred_element_type=F32)
    return d(ah, bh) + d(ah, bl) + d(al, bh)


def _dot_p(a, b, passes):
    return _dot_x3(a, b) if passes == 3 else _bdot(a, b)


def _sigmoid(x):
    return 1.0 / (1.0 + jnp.exp(-x))


def _silu(x):
    return x * _sigmoid(x)


def _softplus(x):
    return jnp.maximum(x, 0.0) + jnp.log1p(jnp.exp(-jnp.abs(x)))


def _log_sigmoid(x):
    return jnp.minimum(x, 0.0) - jnp.log1p(jnp.exp(-jnp.abs(x)))


def _mm_kernel(x_ref, w_ref, o_ref):
    o_ref[...] = jnp.dot(x_ref[...], w_ref[...], preferred_element_type=F32).astype(o_ref.dtype)


def _matmul(x, w, tm, tn):
    m, k = x.shape
    n = w.shape[1]
    return pl.pallas_call(
        _mm_kernel,
        grid=(m // tm, n // tn),
        in_specs=[pl.BlockSpec((tm, k), lambda i, j: (i, 0)),
                  pl.BlockSpec((k, tn), lambda i, j: (0, j))],
        out_specs=pl.BlockSpec((tm, tn), lambda i, j: (i, j)),
        out_shape=jax.ShapeDtypeStruct((m, n), F32),
        compiler_params=_params("parallel", "parallel"),
        name="inproj_matmul",
    )(x, w)


def _hgrn_kernel(q_ref, f_ref, i_ref, g_ref, llb_ref, l1m_ref, oml_ref, nw_ref, s0_ref,
                 o_ref, sout_ref, st_ref, *, tc):
    c = pl.program_id(2)

    @pl.when(c == 0)
    def _():
        st_ref[...] = s0_ref[0]

    nb = tc // HGRN_SUB
    heads = range(HGRN_HPB)
    lanes = [slice(h * HEAD, (h + 1) * HEAD) for h in heads]
    shp = (nb, HGRN_SUB, HEAD)
    z = [f_ref[:, lanes[h]] for h in heads]
    bterm = [l1m_ref[:, lanes[h]] + _log_sigmoid(z[h]) for h in heads]
    log_f = [jnp.maximum(llb_ref[:, lanes[h]], bterm[h])
             + jnp.log1p(jnp.exp(-jnp.abs(llb_ref[:, lanes[h]] - bterm[h]))) for h in heads]
    key3 = [(oml_ref[:, lanes[h]] * (1.0 / (1.0 + jnp.exp(z[h])))).reshape(shp) for h in heads]
    r = lax.broadcasted_iota(jnp.int32, (tc, tc), 0)
    s = lax.broadcasted_iota(jnp.int32, (tc, tc), 1)
    seg_mask = jnp.where((r // HGRN_SUB == s // HGRN_SUB) & (s <= r), 1.0, 0.0).astype(BF16)
    cum3 = [_mask_dot(seg_mask, log_f[h]).reshape(shp) for h in heads]
    q3 = [q_ref[:, lanes[h]].reshape(shp) for h in heads]
    v3 = [i_ref[:, lanes[h]].reshape(shp) for h in heads]
    rows = lax.broadcasted_iota(jnp.int32, shp, 1)
    ones_bf = jnp.ones((HEAD, HEAD), BF16)
    att = []
    for h in heads:
        parts = []
        for si in range(HGRN_SUB):
            arg = jnp.where(rows >= si, cum3[h] - cum3[h][:, si:si + 1, :], -1e30)
            parts.append((q3[h] * (key3[h][:, si:si + 1, :] * jnp.exp(arg))).reshape(tc, HEAD))
        p_hi, p_lo = _split2(jnp.concatenate(parts, axis=0))
        att.append(jnp.dot(p_hi, ones_bf, preferred_element_type=F32)
                   + jnp.dot(p_lo, ones_bf, preferred_element_type=F32))
    o = []
    for h in heads:
        acc = jnp.zeros((tc, HEAD), F32)
        for si in range(HGRN_SUB):
            vs = jnp.broadcast_to(v3[h][:, si:si + 1, :], shp).reshape(tc, HEAD)
            acc = acc + att[h][si * tc:(si + 1) * tc, :] * vs
        o.append(acc)
    last3 = [cum3[h][:, HGRN_SUB - 1:HGRN_SUB, :] for h in heads]
    qtil = [q3[h] * jnp.exp(cum3[h]) for h in heads]
    ktil = [key3[h] * jnp.exp(last3[h] - cum3[h]) for h in heads]
    elast = [jnp.exp(last3[h]) for h in heads]
    kv = [[_bdot_tn(v3[h][j], ktil[h][j]) for j in range(nb)] for h in heads]
    st = [st_ref[h] for h in heads]
    outs = [[] for _ in heads]
    for j in range(nb):
        for h in heads:
            outs[h].append(o[h][j * HGRN_SUB:(j + 1) * HGRN_SUB, :] + _bdot_nt(qtil[h][j], st[h]))
            st[h] = elast[h][j] * st[h] + kv[h][j]
    for h in heads:
        st_ref[h] = st[h]
        oh = jnp.concatenate(outs[h], axis=0)
        oh = oh * lax.rsqrt(jnp.mean(oh * oh, axis=-1, keepdims=True) + NORM_EPS) * nw_ref[...]
        o_ref[:, lanes[h]] = (oh * _silu(g_ref[:, lanes[h]])).astype(o_ref.dtype)

    @pl.when(c == pl.num_programs(2) - 1)
    def _():
        sout_ref[0] = st_ref[...]


def _hgrn(acg, llb, l1m, oml, nw, s0t, b, t):
    tc = min(t, 128)
    nt = t // tc
    hw = HGRN_HPB * HEAD
    ng = N_HEADS // HGRN_HPB
    col_spec = lambda part: pl.BlockSpec((tc, hw), lambda bi, hg, c: (bi * nt + c, part * ng + hg))
    vec_spec = pl.BlockSpec((1, hw), lambda bi, hg, c: (0, hg))
    st_spec = pl.BlockSpec((1, HGRN_HPB, HEAD, HEAD), lambda bi, hg, c: (bi, hg, 0, 0))
    return pl.pallas_call(
        functools.partial(_hgrn_kernel, tc=tc),
        grid=(b, ng, nt),
        in_specs=[col_spec(0), col_spec(1), col_spec(2), col_spec(3),
                  vec_spec, vec_spec, vec_spec, pl.BlockSpec((1, HEAD), lambda bi, hg, c: (0, 0)), st_spec],
        out_specs=[pl.BlockSpec((tc, hw), lambda bi, hg, c: (bi * nt + c, hg)), st_spec],
        out_shape=[jax.ShapeDtypeStruct((b * t, BRANCH_W), BF16),
                   jax.ShapeDtypeStruct((b, N_HEADS, HEAD, HEAD), F32)],
        scratch_shapes=[pltpu.VMEM((HGRN_HPB, HEAD, HEAD), F32)],
        compiler_params=_params("parallel", "parallel", "arbitrary"),
        name="hgrn2",
    )(acg, acg, acg, acg, llb, l1m, oml, nw, s0t)


def _conv_kernel(u_ref, w_ref, buf_ref, o_ref, tail_ref, *, tc):
    c = pl.program_id(1)

    @pl.when(c == 0)
    def _():
        tail_ref[...] = buf_ref[0]

    u = u_ref[...]
    ext = jnp.concatenate([tail_ref[...], u], axis=0)
    acc = u * w_ref[CONV_W - 1:CONV_W, :]
    for j in range(CONV_W - 1):
        acc = acc + ext[5 + j:5 + j + tc, :] * w_ref[j:j + 1, :]
    tail_ref[...] = u[tc - 8:, :]
    y = _silu(acc)
    nqk = B_QK_HEADS * HEAD
    for hh in range(2 * B_QK_HEADS):
        blk = y[:, hh * HEAD:(hh + 1) * HEAD]
        nrm = blk * lax.rsqrt(jnp.sum(blk * blk, axis=-1, keepdims=True) + NORM_EPS)
        if hh < B_QK_HEADS:
            nrm = nrm * (HEAD ** -0.5)
        o_ref[:, hh * HEAD:(hh + 1) * HEAD] = nrm
    o_ref[:, 2 * nqk:] = y[:, 2 * nqk:]


def _gdn_conv(yb, w, buf8, b, t):
    tc = min(t, 256)
    nt = t // tc
    return pl.pallas_call(
        functools.partial(_conv_kernel, tc=tc),
        grid=(b, nt),
        in_specs=[pl.BlockSpec((tc, B_CONV_CH), lambda bi, c: (bi * nt + c, 0)),
                  pl.BlockSpec((CONV_W, B_CONV_CH), lambda bi, c: (0, 0)),
                  pl.BlockSpec((1, 8, B_CONV_CH), lambda bi, c: (bi, 0, 0))],
        out_specs=pl.BlockSpec((tc, B_CONV_CH), lambda bi, c: (bi * nt + c, 0)),
        out_shape=jax.ShapeDtypeStruct((b * t, B_CONV_CH), F32),
        scratch_shapes=[pltpu.VMEM((8, B_CONV_CH), F32)],
        compiler_params=_params("parallel", "arbitrary"),
        name="gdn_conv",
    )(yb, w, buf8)


def _gdn_kernel(qkv_ref, z_ref, sm_ref, alog_ref, dtb_ref, nw_ref, s0_ref, o_ref, sout_ref, st_ref):
    c = pl.program_id(1)
    cs = GDN_CHUNK

    @pl.when(c == 0)
    def _():
        st_ref[...] = s0_ref[0]

    sm = sm_ref[...]
    beta_all = _sigmoid(sm)
    g_all = -jnp.exp(alog_ref[...]) * _softplus(sm + dtb_ref[...])

    r = lax.broadcasted_iota(jnp.int32, (cs, cs), 0)
    s = lax.broadcasted_iota(jnp.int32, (cs, cs), 1)
    lower = r >= s
    strict = r > s
    tril_bf = jnp.where(lower, 1.0, 0.0).astype(BF16)
    triu_bf = jnp.where(r <= s, 1.0, 0.0).astype(BF16)
    above_bf = jnp.where(r < s, 1.0, 0.0).astype(BF16)
    nw = nw_ref[...]
    cum_all = _mask_dot(tril_bf, g_all)
    rem_all = _mask_dot(above_bf, g_all)
    d_tn = lambda a: lax.dot_general(a, triu_bf, (((0,), (0,)), ((), ())), preferred_element_type=F32)
    g_hi, g_mid, g_lo = _split3(g_all)
    cum_rows = d_tn(g_hi) + d_tn(g_mid) + d_tn(g_lo)

    rep = N_HEADS // B_QK_HEADS
    heads = range(N_HEADS)
    q = [qkv_ref[:, qh * HEAD:(qh + 1) * HEAD] for qh in range(B_QK_HEADS)]
    k = [qkv_ref[:, (B_QK_HEADS + qh) * HEAD:(B_QK_HEADS + qh + 1) * HEAD] for qh in range(B_QK_HEADS)]
    kk = [_bdot_nt(k[qh], k[qh]) for qh in range(B_QK_HEADS)]
    qk = [_bdot_nt(q[qh], k[qh]) for qh in range(B_QK_HEADS)]
    cum, rem, dec, x, y = [], [], [], [], []
    for h in heads:
        col = N_HEADS + h
        v = qkv_ref[:, (2 * B_QK_HEADS + h) * HEAD:(2 * B_QK_HEADS + h + 1) * HEAD]
        beta = beta_all[:, h:h + 1]
        cum.append(jnp.broadcast_to(cum_all[:, col:col + 1], (cs, HEAD)))
        rem.append(jnp.broadcast_to(rem_all[:, col:col + 1], (cs, HEAD)))
        seg = cum_all[:, col:col + 1] - cum_rows[col:col + 1, :]
        dec.append(jnp.where(lower, jnp.exp(jnp.where(lower, seg, 0.0)), 0.0))
        x.append(jnp.where(strict, -(kk[h // rep] * beta) * dec[h], 0.0))
        y.append(jnp.concatenate([v * beta, k[h // rep] * beta * jnp.exp(cum[h])], axis=1))
    for lvl in range(len(GDN_APPLY_PASSES)):
        y = [y[h] + _dot_p(x[h], y[h], GDN_APPLY_PASSES[lvl]) for h in heads]
        if lvl < len(GDN_SQUARE_PASSES):
            x = [_dot_p(x[h], x[h], GDN_SQUARE_PASSES[lvl]) for h in heads]
    st = [st_ref[h] for h in heads]
    u = [y[h][:, :HEAD] - _bdot(y[h][:, HEAD:], st[h]) for h in heads]
    o = [_bdot(q[h // rep] * jnp.exp(cum[h]), st[h]) + _bdot(qk[h // rep] * dec[h], u[h]) for h in heads]
    for h in heads:
        st_ref[h] = jnp.exp(cum[h][cs - 1:cs, :]) * st[h] + _bdot_tn(k[h // rep] * jnp.exp(rem[h]), u[h])
    for h in heads:
        on = o[h] * lax.rsqrt(jnp.mean(o[h] * o[h], axis=-1, keepdims=True) + NORM_EPS) * nw
        zg = z_ref[:, h * HEAD:(h + 1) * HEAD]
        o_ref[:, h * HEAD:(h + 1) * HEAD] = (on * _silu(zg)).astype(o_ref.dtype)

    @pl.when(c == pl.num_programs(1) - 1)
    def _():
        sout_ref[0] = st_ref[...]


def _gdn(qkvc, yb, alog_row, dtb_row, nw, s0, b, t):
    cs = GDN_CHUNK
    nt = t // cs
    st_spec = pl.BlockSpec((1, N_HEADS, HEAD, HEAD), lambda bi, c: (bi, 0, 0, 0))
    row_spec = pl.BlockSpec((1, HEAD), lambda bi, c: (0, 0))
    return pl.pallas_call(
        _gdn_kernel,
        grid=(b, nt),
        in_specs=[pl.BlockSpec((cs, B_CONV_CH), lambda bi, c: (bi * nt + c, 0)),
                  pl.BlockSpec((cs, BRANCH_W), lambda bi, c: (bi * nt + c, B_CONV_CH // BRANCH_W)),
                  pl.BlockSpec((cs, HEAD), lambda bi, c: (bi * nt + c, B_SMALL_COL // HEAD)),
                  row_spec, row_spec, row_spec, st_spec],
        out_specs=[pl.BlockSpec((cs, BRANCH_W), lambda bi, c: (bi * nt + c, 0)), st_spec],
        out_shape=[jax.ShapeDtypeStruct((b * t, BRANCH_W), BF16),
                   jax.ShapeDtypeStruct((b, N_HEADS, HEAD, HEAD), F32)],
        scratch_shapes=[pltpu.VMEM((N_HEADS, HEAD, HEAD), F32)],
        compiler_params=_params("parallel", "arbitrary"),
        name="gated_delta",
    )(qkvc, yb, yb, alog_row, dtb_row, nw, s0)


def _ret_kernel(lg_ref, q_ref, k_ref, v_ref, g_ref, cos_ref, sin_ref, nw_ref, s0_ref,
                o_ref, sout_ref, st_ref, *, cs):
    h = pl.program_id(1)
    c = pl.program_id(2)

    @pl.when(c == 0)
    def _():
        st_ref[...] = s0_ref[0, 0]

    lg = lg_ref[h]
    cos = cos_ref[...]
    sin = sin_ref[...]

    def rot(x):
        return x * cos + pltpu.roll(x, HEAD // 2, 1) * sin

    q = rot(q_ref[...])
    k = rot(k_ref[...]) * (HEAD ** -0.5)
    v = v_ref[...]
    idx = lax.broadcasted_iota(jnp.int32, (cs, HEAD), 0).astype(F32)
    q_dec = jnp.exp((idx + 1.0) * lg)
    k_dec = jnp.exp((cs - 1.0 - idx) * lg)
    c_dec = jnp.exp(jnp.zeros((1, HEAD), F32) + cs * lg)
    rel = (lax.broadcasted_iota(jnp.int32, (cs, cs), 0)
           - lax.broadcasted_iota(jnp.int32, (cs, cs), 1))
    intra = jnp.where(rel >= 0, jnp.exp(jnp.maximum(rel, 0).astype(F32) * lg), 0.0)
    st = st_ref[...]
    att = _bdot_nt(q, k) * intra
    o = _bdot(q, st) * q_dec + _bdot(att, v)
    st_ref[...] = c_dec * st + _bdot_tn(k * k_dec, v)
    mu = jnp.mean(o, axis=-1, keepdims=True)
    var = jnp.mean(jnp.square(o - mu), axis=-1, keepdims=True)
    o = (o - mu) * lax.rsqrt(var + NORM_EPS) * nw_ref[...]
    o_ref[...] = (o * _silu(g_ref[...])).astype(o_ref.dtype)

    @pl.when(c == pl.num_programs(2) - 1)
    def _():
        sout_ref[0, 0] = st_ref[...]


def _retention(acg, lg, cos, sin, nw, s0, b, t):
    cs = min(t, 512)
    nt = t // cs
    col_spec = lambda part: pl.BlockSpec(
        (cs, HEAD), lambda bi, h, c, lg_ref: (bi * nt + c, (4 + part) * N_HEADS + h))
    tab_spec = pl.BlockSpec((cs, HEAD), lambda bi, h, c, lg_ref: (c, 0))
    st_spec = pl.BlockSpec((1, 1, HEAD, HEAD), lambda bi, h, c, lg_ref: (bi, h, 0, 0))
    return pl.pallas_call(
        functools.partial(_ret_kernel, cs=cs),
        grid_spec=pltpu.PrefetchScalarGridSpec(
            num_scalar_prefetch=1,
            grid=(b, N_HEADS, nt),
            in_specs=[col_spec(0), col_spec(1), col_spec(2), col_spec(3), tab_spec, tab_spec,
                      pl.BlockSpec((1, HEAD), lambda bi, h, c, lg_ref: (0, 0)), st_spec],
            out_specs=[pl.BlockSpec((cs, HEAD), lambda bi, h, c, lg_ref: (bi * nt + c, h)), st_spec],
            scratch_shapes=[pltpu.VMEM((HEAD, HEAD), F32)]),
        out_shape=[jax.ShapeDtypeStruct((b * t, BRANCH_W), BF16),
                   jax.ShapeDtypeStruct((b, N_HEADS, HEAD, HEAD), F32)],
        compiler_params=_params("parallel", "parallel", "arbitrary"),
        name="retention",
    )(lg, acg, acg, acg, acg, cos, sin, nw, s0)


def _merge_kernel(oa_ref, ob_ref, oc_ref, wb_ref, ga_ref, gb_ref, gc_ref, o_ref):
    acc = _sigmoid(ga_ref[...]) * jnp.dot(oa_ref[...], wb_ref[0], preferred_element_type=F32)
    acc += _sigmoid(gb_ref[...]) * jnp.dot(ob_ref[...], wb_ref[1], preferred_element_type=F32)
    acc += _sigmoid(gc_ref[...]) * jnp.dot(oc_ref[...], wb_ref[2], preferred_element_type=F32)
    o_ref[...] = acc.astype(o_ref.dtype)


def _merge(oa, ob, oc, wb, acg, tm):
    n = oa.shape[0]
    tn = 512
    o_spec = pl.BlockSpec((tm, BRANCH_W), lambda i, j: (i, 0))
    gate_spec = lambda br: pl.BlockSpec(
        (tm, tn), lambda i, j: (i, (GATE_COL0 + br * D_MODEL) // tn + j))
    return pl.pallas_call(
        _merge_kernel,
        grid=(n // tm, D_MODEL // tn),
        in_specs=[o_spec, o_spec, o_spec,
                  pl.BlockSpec((N_BRANCH, BRANCH_W, tn), lambda i, j: (0, 0, j)),
                  gate_spec(0), gate_spec(1), gate_spec(2)],
        out_specs=pl.BlockSpec((tm, tn), lambda i, j: (i, j)),
        out_shape=jax.ShapeDtypeStruct((n, D_MODEL), BF16),
        compiler_params=_params("parallel", "parallel"),
        name="branch_merge",
    )(oa, ob, oc, wb, acg, acg, acg)


def _layer_norm(y, g, b):
    mu = jnp.mean(y, axis=-1, keepdims=True)
    var = jnp.mean(jnp.square(y - mu), axis=-1, keepdims=True)
    return (y - mu) * lax.rsqrt(var + LN_EPS) * g + b


def _outproj_kernel(m_ref, x_ref, w_ref, g_ref, b_ref, wrh_ref, wrl_ref, xo_ref, xb_ref, sc_ref):
    h = jnp.dot(m_ref[...], w_ref[...], preferred_element_type=F32)
    x1 = _layer_norm(DN_ALPHA * x_ref[...] + h, g_ref[...], b_ref[...])
    xo_ref[...] = x1
    xh, xl = _split2(x1)
    xb_ref[...] = xh
    d = lambda a, b_: lax.dot_general(a, b_, (((1,), (1,)), ((), ())), preferred_element_type=F32)
    logits = d(wrh_ref[...], xh) + d(wrh_ref[...], xl) + d(wrl_ref[...], xh)
    sc_ref[...] = _sigmoid(logits)


def _outproj(merged, x, w_out, g, b, wr_hi, wr_lo, tm):
    n = x.shape[0]
    row = pl.BlockSpec((tm, D_MODEL), lambda i: (i, 0))
    vec = pl.BlockSpec((1, D_MODEL), lambda i: (0, 0))
    wr = pl.BlockSpec((N_EXPERTS, D_MODEL), lambda i: (0, 0))
    return pl.pallas_call(
        _outproj_kernel,
        grid=(n // tm,),
        in_specs=[row, row, pl.BlockSpec((D_MODEL, D_MODEL), lambda i: (0, 0)), vec, vec, wr, wr],
        out_specs=[row, row, pl.BlockSpec((N_EXPERTS, tm), lambda i: (0, i))],
        out_shape=[jax.ShapeDtypeStruct((n, D_MODEL), F32), jax.ShapeDtypeStruct((n, D_MODEL), BF16),
                   jax.ShapeDtypeStruct((N_EXPERTS, n), F32)],
        compiler_params=_params("parallel"),
        name="outproj_ln_router",
    )(merged, x, w_out, g, b, wr_hi, wr_lo)


def _route_kernel(sc_ref, bias_ref, e_ref, pos_ref, w_ref, cnt_ref, carry_ref, *, tm):
    i = pl.program_id(0)
    per_group = N_EXPERTS // N_GROUPS

    @pl.when(i == 0)
    def _():
        carry_ref[...] = jnp.zeros_like(carry_ref)

    s = sc_ref[...]
    c3 = (s + bias_ref[...]).reshape(N_GROUPS, per_group, tm)
    midx = lax.broadcasted_iota(jnp.int32, (N_GROUPS, per_group, tm), 1)
    gidx = lax.broadcasted_iota(jnp.int32, (N_GROUPS, per_group, tm), 0)

    def beats(other, mine, other_first):
        return jnp.where((other > mine) | ((other == mine) & other_first), 1, 0)

    r1 = jnp.zeros((N_GROUPS, per_group, tm), jnp.int32)
    for m in range(per_group):
        r1 = r1 + beats(c3[:, m:m + 1, :], c3, m < midx)
    gs = jnp.sum(jnp.where(r1 < 2, c3, 0.0), axis=1, keepdims=True)
    gs = jnp.broadcast_to(gs, (N_GROUPS, per_group, tm))
    r2 = jnp.zeros((N_GROUPS, per_group, tm), jnp.int32)
    for g in range(N_GROUPS):
        r2 = r2 + beats(gs[g:g + 1], gs, g < gidx)
    masked = jnp.where(r2 < TOPK_GROUPS, c3, -jnp.inf).reshape(N_EXPERTS, tm)
    eidx = lax.broadcasted_iota(jnp.int32, (N_EXPERTS, tm), 0)
    r3 = jnp.zeros((N_EXPERTS, tm), jnp.int32)
    for e in range(N_EXPERTS):
        r3 = r3 + beats(masked[e:e + 1, :], masked, e < eidx)
    sel = r3 < TOP_K
    w = jnp.where(sel, s, 0.0)
    w = w / jnp.sum(w, axis=0, keepdims=True) * ROUTE_SCALE
    sel_bf = jnp.where(sel, 1.0, 0.0).astype(BF16)
    tr = lax.broadcasted_iota(jnp.int32, (tm, tm), 0)
    tc_ = lax.broadcasted_iota(jnp.int32, (tm, tm), 1)
    before = jnp.where(tr < tc_, 1.0, 0.0).astype(BF16)
    carry = carry_ref[...]
    pos = (jnp.dot(sel_bf, before, preferred_element_type=F32)
           + jnp.concatenate([carry] * (tm // HEAD), axis=1))
    carry_ref[...] = carry + jnp.dot(sel_bf, jnp.ones((tm, HEAD), BF16), preferred_element_type=F32)
    er = lax.broadcasted_iota(jnp.int32, (N_EXPERTS, N_EXPERTS), 0)
    ec = lax.broadcasted_iota(jnp.int32, (N_EXPERTS, N_EXPERTS), 1)
    lower_e = jnp.where(ec < er, 1.0, 0.0).astype(BF16)
    jidx = jnp.dot(lower_e, sel_bf, preferred_element_type=F32)
    eidx_f = eidx.astype(F32)
    e_rows, p_rows, w_rows = [], [], []
    for j in range(TOP_K):
        oh = sel & (jidx == float(j))
        e_rows.append(jnp.sum(jnp.where(oh, eidx_f, 0.0), axis=0, keepdims=True))
        p_rows.append(jnp.sum(jnp.where(oh, pos, 0.0), axis=0, keepdims=True))
        w_rows.append(jnp.sum(jnp.where(oh, w, 0.0), axis=0, keepdims=True))
    zpad = [jnp.zeros((8 - TOP_K, tm), F32)]
    e_ref[...] = jnp.concatenate(e_rows + zpad, axis=0).astype(jnp.int32)
    pos_ref[...] = jnp.concatenate(p_rows + zpad, axis=0).astype(jnp.int32)
    w_ref[...] = jnp.concatenate(w_rows + zpad, axis=0)

    @pl.when(i == pl.num_programs(0) - 1)
    def _():
        cnt_ref[...] = carry_ref[...]


def _route(scores_t, bias_b, tm):
    n = scores_t.shape[1]
    blk = pl.BlockSpec((8, tm), lambda i: (0, i))
    return pl.pallas_call(
        functools.partial(_route_kernel, tm=tm),
        grid=(n // tm,),
        in_specs=[pl.BlockSpec((N_EXPERTS, tm), lambda i: (0, i)),
                  pl.BlockSpec((N_EXPERTS, tm), lambda i: (0, 0))],
        out_specs=[blk, blk, blk, pl.BlockSpec((N_EXPERTS, HEAD), lambda i: (0, 0))],
        out_shape=[jax.ShapeDtypeStruct((8, n), jnp.int32), jax.ShapeDtypeStruct((8, n), jnp.int32),
                   jax.ShapeDtypeStruct((8, n), F32), jax.ShapeDtypeStruct((N_EXPERTS, HEAD), F32)],
        scratch_shapes=[pltpu.VMEM((N_EXPERTS, HEAD), F32)],
        compiler_params=_params("arbitrary"),
        name="route_topk",
    )(scores_t, bias_b)


def _expert_kernel(be_ref, s_m2, s_m1, s_cur, s_p1, x_hbm, wg_ref, wu_ref, wd_ref,
                   out_hbm, xbuf, ybuf, sem_in, sem_out, *, n_tok):
    i = pl.program_id(0)
    last = pl.num_programs(0) - 1
    slot = i % 2
    other = 1 - slot

    def buf_row(k):
        return k * ROW_PITCH if isinstance(k, int) else pl.multiple_of(k * ROW_PITCH, 8)

    def in_copy(slots_ref, buf, k):
        src = pl.multiple_of((slots_ref[0, 0, k] >> 3) * ROW_TILES, ROW_TILES)
        return pltpu.make_async_copy(x_hbm.at[pl.ds(src, ROW_TILES)],
                                     xbuf.at[buf, pl.ds(buf_row(k), ROW_TILES)], sem_in.at[buf, k])

    def out_copy(slots_ref, buf, k):
        v = slots_ref[0, 0, k]
        dst = pl.multiple_of(((v & 7) * n_tok + (v >> 3)) * ROW_TILES, ROW_TILES)
        return pltpu.make_async_copy(ybuf.at[buf, pl.ds(buf_row(k), ROW_TILES)],
                                     out_hbm.at[pl.ds(dst, ROW_TILES)], sem_out.at[buf, k])

    def rolled(fn):
        def body(k, carry):
            fn(k)
            return carry
        lax.fori_loop(0, MOE_BLOCK, body, 0, unroll=8)

    @pl.when(i == 0)
    def _():
        rolled(lambda k: in_copy(s_cur, 0, k).start())

    rolled(lambda k: in_copy(s_cur, slot, k).wait())

    @pl.when(i >= 2)
    def _():
        rolled(lambda k: out_copy(s_m2, slot, k).wait())

    def block(gather_next, scatter_prev):
        col = lambda ref, s_: ref.at[slot][pl.ds(s_, MOE_BLOCK, stride=ROW_PITCH), :]
        xb = jnp.concatenate([col(xbuf, s_) for s_ in range(ROW_TILES)], axis=1).astype(BF16)
        for k in range(MOE_BLOCK):
            if gather_next:
                in_copy(s_p1, other, k).start()
            if scatter_prev:
                out_copy(s_m1, other, k).start()
        hg = jnp.dot(xb, wg_ref[0], preferred_element_type=F32)
        hu = jnp.dot(xb, wu_ref[0], preferred_element_type=F32)
        hb = (_silu(hg) * hu).astype(BF16)
        y = jnp.dot(hb, wd_ref[0], preferred_element_type=F32)
        for s_ in range(ROW_TILES):
            ybuf.at[slot][pl.ds(s_, MOE_BLOCK, stride=ROW_PITCH), :] = y[:, s_ * HEAD:(s_ + 1) * HEAD]

    @pl.when(i == 0)
    def _():
        block(True, False)

    @pl.when((i > 0) & (i < last))
    def _():
        block(True, True)

    @pl.when(i == last)
    def _():
        block(False, True)
        rolled(lambda k: out_copy(s_cur, slot, k).start())
        rolled(lambda k: out_copy(s_m1, other, k).wait())
        rolled(lambda k: out_copy(s_cur, slot, k).wait())


def _experts(x1r, block_e, slots, wg, wu, wd):
    n = x1r.shape[0] // ROW_TILES
    n_blocks = block_e.shape[0]
    slot_spec = lambda off: pl.BlockSpec(
        (1, 1, MOE_BLOCK), lambda i, be: (jnp.clip(i + off, 0, n_blocks - 1), 0, 0), memory_space=pltpu.SMEM)
    return pl.pallas_call(
        functools.partial(_expert_kernel, n_tok=n),
        grid_spec=pltpu.PrefetchScalarGridSpec(
            num_scalar_prefetch=1,
            grid=(n_blocks,),
            in_specs=[slot_spec(-2), slot_spec(-1), slot_spec(0), slot_spec(1),
                      pl.BlockSpec(memory_space=pl.ANY),
                      pl.BlockSpec((1, D_MODEL, EXPERT_FF), lambda i, be: (be[i], 0, 0)),
                      pl.BlockSpec((1, D_MODEL, EXPERT_FF), lambda i, be: (be[i], 0, 0)),
                      pl.BlockSpec((1, EXPERT_FF, D_MODEL), lambda i, be: (be[i], 0, 0))],
            out_specs=pl.BlockSpec(memory_space=pl.ANY),
            scratch_shapes=[pltpu.VMEM((2, MOE_BLOCK * ROW_PITCH, HEAD), F32),
                            pltpu.VMEM((2, MOE_BLOCK * ROW_PITCH, HEAD), F32),
                            pltpu.SemaphoreType.DMA((2, MOE_BLOCK)), pltpu.SemaphoreType.DMA((2, MOE_BLOCK))]),
        out_shape=jax.ShapeDtypeStruct(((TOP_K * n + 2 * MOE_BLOCK) * ROW_TILES, HEAD), F32),
        compiler_params=_params("arbitrary"),
        name="routed_experts",
    )(block_e, slots, slots, slots, slots, x1r, wg, wu, wd)


def _slot_tables(e6, pos6, counts):
    n = e6.shape[1]
    n_assign = n * TOP_K
    n_blocks = -(-n_assign // MOE_BLOCK) + N_EXPERTS
    padded = (counts + MOE_BLOCK - 1) // MOE_BLOCK * MOE_BLOCK
    pad_end = jnp.cumsum(padded)
    pad_start = pad_end - padded
    onehot = e6[:TOP_K, :, None] == jnp.arange(N_EXPERTS, dtype=jnp.int32)
    dest = jnp.sum(jnp.where(onehot, pad_start, 0), axis=-1) + pos6[:TOP_K]
    packed = (jnp.arange(n, dtype=jnp.int32)[None, :] * 8
              + jnp.arange(TOP_K, dtype=jnp.int32)[:, None])
    pad_slots = (jnp.arange(n_blocks * MOE_BLOCK, dtype=jnp.int32) % (2 * MOE_BLOCK)) * 8 + TOP_K
    slots = pad_slots.at[dest.reshape(-1)].set(packed.reshape(-1), unique_indices=True)
    blk_start = jnp.arange(n_blocks, dtype=jnp.int32) * MOE_BLOCK
    block_e = jnp.minimum(jnp.sum((pad_end[None, :] <= blk_start[:, None]).astype(jnp.int32), axis=1),
                          N_EXPERTS - 1)
    return block_e, slots.reshape(n_blocks, 1, MOE_BLOCK)


def _combine_kernel(*refs):
    r_refs = refs[:TOP_K]
    w6_ref, x_ref, xb_ref, wg_ref, wu_ref, wd_ref, g_ref, b_ref, xo_ref, xob_ref = refs[TOP_K:]
    xb = xb_ref[...]
    hb = (_silu(jnp.dot(xb, wg_ref[...], preferred_element_type=F32))
          * jnp.dot(xb, wu_ref[...], preferred_element_type=F32)).astype(BF16)
    shared = jnp.dot(hb, wd_ref[...], preferred_element_type=F32)
    w6 = w6_ref[...]
    tm = w6.shape[0]
    rows2d = lambda ref: jnp.concatenate(
        [ref[pl.ds(s_, tm, stride=ROW_TILES), :] for s_ in range(ROW_TILES)], axis=1)
    routed = rows2d(r_refs[0]) * w6[:, 0:1]
    for j in range(1, TOP_K):
        routed = routed + rows2d(r_refs[j]) * w6[:, j:j + 1]
    x2 = _layer_norm(DN_ALPHA * x_ref[...] + (routed + shared), g_ref[...], b_ref[...])
    xo_ref[...] = x2
    xob_ref[...] = x2.astype(BF16)


def _combine(routed, w6, x1, x1b, wg, wu, wd, g, b, tm):
    n = x1.shape[0]
    nt = n // tm
    row = pl.BlockSpec((tm, D_MODEL), lambda i: (i, 0))
    vec = pl.BlockSpec((1, D_MODEL), lambda i: (0, 0))
    w_in = pl.BlockSpec((D_MODEL, EXPERT_FF), lambda i: (0, 0))
    r_specs = [pl.BlockSpec((tm * ROW_TILES, HEAD), functools.partial(lambda i, j: (j * nt + i, 0), j=j))
               for j in range(TOP_K)]
    return pl.pallas_call(
        _combine_kernel,
        grid=(nt,),
        in_specs=r_specs + [pl.BlockSpec((tm, 8), lambda i: (i, 0)), row, row, w_in, w_in,
                            pl.BlockSpec((EXPERT_FF, D_MODEL), lambda i: (0, 0)), vec, vec],
        out_specs=[row, row],
        out_shape=[jax.ShapeDtypeStruct((n, D_MODEL), F32), jax.ShapeDtypeStruct((n, D_MODEL), BF16)],
        compiler_params=_params("parallel"),
        name="combine_shared_ln",
    )(*([routed] * TOP_K), w6, x1, x1b, wg, wu, wd, g, b)


def _ple_kernel(x_ref, xb_ref, p_ref, wg_ref, wp_ref, xo_ref, xob_ref):
    gate = _sigmoid(jnp.dot(xb_ref[...], wg_ref[...], preferred_element_type=F32))
    pe = jnp.dot(p_ref[...].astype(BF16), wp_ref[...], preferred_element_type=F32)
    x3 = x_ref[...] + pe * gate
    xo_ref[...] = x3
    xob_ref[...] = x3.astype(BF16)


def _ple(x2, x2b, p, wg, wp, tm):
    n = x2.shape[0]
    row = pl.BlockSpec((tm, D_MODEL), lambda i: (i, 0))
    return pl.pallas_call(
        _ple_kernel,
        grid=(n // tm,),
        in_specs=[row, row, pl.BlockSpec((tm, PLE_DIM), lambda i: (i, 0)),
                  pl.BlockSpec((D_MODEL, D_MODEL), lambda i: (0, 0)),
                  pl.BlockSpec((PLE_DIM, D_MODEL), lambda i: (0, 0))],
        out_specs=[row, row],
        out_shape=[jax.ShapeDtypeStruct((n, D_MODEL), F32), jax.ShapeDtypeStruct((n, D_MODEL), BF16)],
        compiler_params=_params("parallel"),
        name="ple",
    )(x2, x2b, p, wg, wp)


def _prep_weights(w_in, hgrn_lb_logits, gdn_a_log, gdn_dt_bias, w_branch, w_out, w_router,
                  w_exp_gate, w_exp_up, w_exp_down, w_sh_gate, w_sh_up, w_sh_down, w_ple_gate, w_ple_proj):
    pts = np.cumsum([1024, 1024, 1024, 1024, 512, 512, 1024, 1024, 8, 8, 1024, 1024, 1024, 1024]).tolist()
    a_end, b_end, c_end = pts[3], pts[9], pts[13]
    w_acg = jnp.concatenate([w_in[:, :, :a_end], w_in[:, :, b_end:c_end], w_in[:, :, c_end:]], axis=-1)
    w_b = jnp.concatenate([w_in[:, :, a_end:b_end],
                           jnp.zeros((DEPTH, D_MODEL, B_COLS - (b_end - a_end)), w_in.dtype)], axis=-1)
    lb_p = jax.nn.softmax(hgrn_lb_logits.astype(F32), axis=0)
    lb = jnp.maximum(jnp.cumsum(lb_p, axis=0) - lb_p[:1], 0.0)
    pad_row = lambda v: jnp.zeros((DEPTH, 1, HEAD), F32).at[:, 0, N_HEADS:2 * N_HEADS].set(v.astype(F32))
    wr = jnp.swapaxes(w_router.astype(F32), -1, -2)
    wr_hi = wr.astype(BF16)
    return dict(
        w_acg=w_acg.astype(BF16), w_b=w_b.astype(BF16),
        llb=jnp.log(lb)[:, None, :], l1m=jnp.log1p(-lb)[:, None, :], oml=(1.0 - lb)[:, None, :],
        alog_row=pad_row(gdn_a_log), dtb_row=pad_row(gdn_dt_bias),
        w_branch=w_branch.astype(BF16), w_out=w_out.astype(BF16),
        wr_hi=wr_hi, wr_lo=(wr - wr_hi.astype(F32)).astype(BF16),
        wg=w_exp_gate.astype(BF16), wu=w_exp_up.astype(BF16), wd=w_exp_down.astype(BF16),
        sg=w_sh_gate.astype(BF16), su=w_sh_up.astype(BF16), sd=w_sh_down.astype(BF16),
        pg=w_ple_gate.astype(BF16), pp=w_ple_proj.astype(BF16))


def _run_trunk(x, p, pos0, states, pw, small):
    (hgrn_norm, gdn_conv, gdn_norm, ret_norm, ln1_g, ln1_b, ln2_g, ln2_b, router_bias) = small
    s_hgrn, s_gdn, s_conv, s_ret = states
    b, t, d = x.shape
    n = b * t
    tm = min(n, 512)
    xf = x.reshape(n, d)
    xb = xf.astype(BF16)
    half = HEAD // 2
    inv = ROPE_BASE ** (-jnp.arange(half, dtype=F32) / half)
    ang = (pos0 + jnp.arange(t, dtype=F32))[:, None] * inv[None, :]
    cos_t = jnp.concatenate([jnp.cos(ang), jnp.cos(ang)], axis=-1)
    sin_t = jnp.concatenate([-jnp.sin(ang), jnp.sin(ang)], axis=-1)
    lg = jnp.log1p(-jnp.exp2(-5.0 - jnp.arange(N_HEADS, dtype=F32)))
    row = lambda v: v.reshape(1, -1).astype(F32)
    new = ([], [], [], [])
    for i in range(DEPTH):
        acg = _matmul(xb, pw['w_acg'][i], min(n, 2048), 1024)
        yb = _matmul(xb, pw['w_b'][i], min(n, 2048), 640)
        o_a, st_a = _hgrn(acg, pw['llb'][i], pw['l1m'][i], pw['oml'][i], row(hgrn_norm[i]),
                          jnp.swapaxes(s_hgrn[i], -1, -2), b, t)
        buf8 = jnp.concatenate([jnp.zeros((b, 8 - (CONV_W - 1), B_CONV_CH), F32), s_conv[i]], axis=1)
        qkvc = _gdn_conv(yb, gdn_conv[i], buf8, b, t)
        o_b, st_b = _gdn(qkvc, yb, pw['alog_row'][i], pw['dtb_row'][i], row(gdn_norm[i]), s_gdn[i], b, t)
        o_c, st_c = _retention(acg, lg, cos_t, sin_t, row(ret_norm[i]), s_ret[i], b, t)
        merged = _merge(o_a, o_b, o_c, pw['w_branch'][i], acg, min(n, 1024))
        x1, x1b, scores = _outproj(merged, xf, pw['w_out'][i], row(ln1_g[i]), row(ln1_b[i]),
                                   pw['wr_hi'][i], pw['wr_lo'][i], tm)
        bias_b = jnp.broadcast_to(router_bias[i].astype(F32)[:, None], (N_EXPERTS, tm))
        e6, pos6, w6, cnt = _route(scores, bias_b, tm)
        block_e, slots = _slot_tables(e6, pos6, cnt[:, 0].astype(jnp.int32))
        routed = _experts(x1.reshape(n * ROW_TILES, HEAD), block_e, slots,
                          pw['wg'][i], pw['wu'][i], pw['wd'][i])
        x2, x2b = _combine(routed, w6.T, x1, x1b, pw['sg'][i], pw['su'][i],
                           pw['sd'][i], row(ln2_g[i]), row(ln2_b[i]), min(n, 128))
        xf, xb = _ple(x2, x2b, p[i].reshape(n, PLE_DIM), pw['pg'][i], pw['pp'][i], tm)
        new[0].append(jnp.swapaxes(st_a, -1, -2))
        new[1].append(st_b)
        new[2].append(yb.reshape(b, t, B_COLS)[:, t - (CONV_W - 1):, :B_CONV_CH])
        new[3].append(st_c)
    return xf.reshape(b, t, d), tuple(jnp.stack(a) for a in new)


def kernel(x_prompt, x_sample, p_prompt, p_sample, state_hgrn, state_gdn, state_gdn_conv, state_ret,
           w_in, hgrn_lb_logits, hgrn_norm, gdn_conv, gdn_a_log, gdn_dt_bias, gdn_norm, ret_norm,
           w_branch, w_out, ln1_g, ln1_b, ln2_g, ln2_b, w_router, router_bias,
           w_exp_gate, w_exp_up, w_exp_down, w_sh_gate, w_sh_up, w_sh_down, w_ple_gate, w_ple_proj):
    pw = _prep_weights(w_in, hgrn_lb_logits, gdn_a_log, gdn_dt_bias, w_branch, w_out, w_router,
                       w_exp_gate, w_exp_up, w_exp_down, w_sh_gate, w_sh_up, w_sh_down,
                       w_ple_gate, w_ple_proj)
    small = (hgrn_norm, gdn_conv, gdn_norm, ret_norm, ln1_g, ln1_b, ln2_g, ln2_b, router_bias)
    bp = x_prompt.shape[0]
    zero_states = (jnp.zeros((DEPTH, bp, N_HEADS, HEAD, HEAD), F32),
                   jnp.zeros((DEPTH, bp, N_HEADS, HEAD, HEAD), F32),
                   jnp.zeros((DEPTH, bp, CONV_W - 1, B_CONV_CH), F32),
                   jnp.zeros((DEPTH, bp, N_HEADS, HEAD, HEAD), F32))
    y_p, (hp, gp, cp, rp) = _run_trunk(x_prompt, p_prompt, 0.0, zero_states, pw, small)
    y_s, (hs, gs, cs, rs) = _run_trunk(x_sample, p_sample, float(PAST_LEN),
                                       (state_hgrn, state_gdn, state_gdn_conv, state_ret), pw, small)
    return (y_p, y_s, hp, gp, cp, rp, hs, gs, cs, rs)
```

```python
import functools
import math

import jax
import jax.numpy as jnp
import numpy as np
from jax import lax
from jax.experimental import pallas as pl
from jax.experimental.pallas import tpu as pltpu

F32 = jnp.float32
BF16 = jnp.bfloat16

D_MODEL = 2048
DEPTH = 4
PAST_LEN = 4096
HEAD = 128
N_HEADS = 8
B_QK_HEADS = 4
CONV_W = 4
BRANCH_W = N_HEADS * HEAD
B_CONV_CH = 2 * B_QK_HEADS * HEAD + N_HEADS * HEAD
ROPE_BASE = 10000.0
N_BRANCH = 3
N_EXPERTS = 64
N_GROUPS = 8
TOPK_GROUPS = 4
TOP_K = 6
EXPERT_FF = 512
ROUTE_SCALE = 2.5
MOE_BLOCK = 256
ROW_TILES = D_MODEL // HEAD
EXPERT_BUFS = 3
ROW_PITCH = 24
PLE_DIM = 256
DN_ALPHA = (2.0 * DEPTH) ** 0.25
LN_EPS = 1e-5
NORM_EPS = 1e-6

HGRN_SUB = 16
HGRN_HPB = 8
GDN_CHUNK = 64
GDN_APPLY_PASSES = (3, 1, 1, 1, 1, 1)
GDN_SQUARE_PASSES = (3, 1, 1, 1, 1)
ACG_COLS = 4 * BRANCH_W + 4 * BRANCH_W + N_BRANCH * D_MODEL
GATE_COL0 = 8 * BRANCH_W
B_COLS = 3200
B_SMALL_COL = 3072

VMEM_LIMIT = 56 * 1024 * 1024


def _params(*sem):
    return pltpu.CompilerParams(dimension_semantics=sem, vmem_limit_bytes=VMEM_LIMIT)


def _bdot(a, b):
    return jnp.dot(a.astype(BF16), b.astype(BF16), preferred_element_type=F32)


def _bdot_nt(a, b):
    return lax.dot_general(a.astype(BF16), b.astype(BF16), (((1,), (1,)), ((), ())),
                           preferred_element_type=F32)


def _bdot_tn(a, b):
    return lax.dot_general(a.astype(BF16), b.astype(BF16), (((0,), (0,)), ((), ())),
                           preferred_element_type=F32)


def _split2(x):
    hi = x.astype(BF16)
    lo = (x - hi.astype(F32)).astype(BF16)
    return hi, lo


def _split3(x):
    hi = x.astype(BF16)
    r = x - hi.astype(F32)
    mid = r.astype(BF16)
    lo = (r - mid.astype(F32)).astype(BF16)
    return hi, mid, lo


def _mask_dot(mask_bf, x):
    hi, mid, lo = _split3(x)
    d = functools.partial(jnp.dot, preferred_element_type=F32)
    return d(mask_bf, hi) + d(mask_bf, mid) + d(mask_bf, lo)


def _dot_x3(a, b):
    ah, al = _split2(a)
    bh, bl = _split2(b)
    d = functools.partial(jnp.dot, preferred_element_type=F32)
    return d(ah, bh) + d(ah, bl) + d(al, bh)


def _dot_p(a, b, passes):
    return _dot_x3(a, b) if passes == 3 else _bdot(a, b)


def _sigmoid(x):
    return 1.0 / (1.0 + jnp.exp(-x))


def _silu(x):
    return x * _sigmoid(x)


def _softplus(x):
    return jnp.maximum(x, 0.0) + jnp.log1p(jnp.exp(-jnp.abs(x)))


def _log_sigmoid(x):
    return jnp.minimum(x, 0.0) - jnp.log1p(jnp.exp(-jnp.abs(x)))


def _mm_kernel(x_ref, w_ref, o_ref):
    o_ref[...] = jnp.dot(x_ref[...], w_ref[...], preferred_element_type=F32).astype(o_ref.dtype)


def _matmul(x, w, tm, tn):
    m, k = x.shape
    n = w.shape[1]
    return pl.pallas_call(
        _mm_kernel,
        grid=(m // tm, n // tn),
        in_specs=[pl.BlockSpec((tm, k), lambda i, j: (i, 0)),
                  pl.BlockSpec((k, tn), lambda i, j: (0, j))],
        out_specs=pl.BlockSpec((tm, tn), lambda i, j: (i, j)),
        out_shape=jax.ShapeDtypeStruct((m, n), F32),
        compiler_params=_params("parallel", "parallel"),
        name="inproj_matmul",
    )(x, w)


def _hgrn_kernel(q_ref, f_ref, i_ref, g_ref, llb_ref, l1m_ref, oml_ref, nw_ref, s0_ref,
                 o_ref, sout_ref, st_ref, *, tc):
    c = pl.program_id(2)

    @pl.when(c == 0)
    def _():
        st_ref[...] = s0_ref[0]

    nb = tc // HGRN_SUB
    heads = range(HGRN_HPB)
    lanes = [slice(h * HEAD, (h + 1) * HEAD) for h in heads]
    shp = (nb, HGRN_SUB, HEAD)
    z = [f_ref[:, lanes[h]] for h in heads]
    bterm = [l1m_ref[:, lanes[h]] + _log_sigmoid(z[h]) for h in heads]
    log_f = [jnp.maximum(llb_ref[:, lanes[h]], bterm[h])
             + jnp.log1p(jnp.exp(-jnp.abs(llb_ref[:, lanes[h]] - bterm[h]))) for h in heads]
    key3 = [(oml_ref[:, lanes[h]] * (1.0 / (1.0 + jnp.exp(z[h])))).reshape(shp) for h in heads]
    r = lax.broadcasted_iota(jnp.int32, (tc, tc), 0)
    s = lax.broadcasted_iota(jnp.int32, (tc, tc), 1)
    seg_mask = jnp.where((r // HGRN_SUB == s // HGRN_SUB) & (s <= r), 1.0, 0.0).astype(BF16)
    cum3 = [_mask_dot(seg_mask, log_f[h]).reshape(shp) for h in heads]
    q3 = [q_ref[:, lanes[h]].reshape(shp) for h in heads]
    v3 = [i_ref[:, lanes[h]].reshape(shp) for h in heads]
    rows = lax.broadcasted_iota(jnp.int32, shp, 1)
    ones_bf = jnp.ones((HEAD, HEAD), BF16)
    att = []
    for h in heads:
        parts = []
        for si in range(HGRN_SUB):
            arg = jnp.where(rows >= si, cum3[h] - cum3[h][:, si:si + 1, :], -1e30)
            parts.append((q3[h] * (key3[h][:, si:si + 1, :] * jnp.exp(arg))).reshape(tc, HEAD))
        p_hi, p_lo = _split2(jnp.concatenate(parts, axis=0))
        att.append(jnp.dot(p_hi, ones_bf, preferred_element_type=F32)
                   + jnp.dot(p_lo, ones_bf, preferred_element_type=F32))
    o = []
    for h in heads:
        acc = jnp.zeros((tc, HEAD), F32)
        for si in range(HGRN_SUB):
            vs = jnp.broadcast_to(v3[h][:, si:si + 1, :], shp).reshape(tc, HEAD)
            acc = acc + att[h][si * tc:(si + 1) * tc, :] * vs
        o.append(acc)
    last3 = [cum3[h][:, HGRN_SUB - 1:HGRN_SUB, :] for h in heads]
    qtil = [q3[h] * jnp.exp(cum3[h]) for h in heads]
    ktil = [key3[h] * jnp.exp(last3[h] - cum3[h]) for h in heads]
    elast = [jnp.exp(last3[h]) for h in heads]
    kv = [[_bdot_tn(v3[h][j], ktil[h][j]) for j in range(nb)] for h in heads]
    st = [st_ref[h] for h in heads]
    outs = [[] for _ in heads]
    for j in range(nb):
        for h in heads:
            outs[h].append(o[h][j * HGRN_SUB:(j + 1) * HGRN_SUB, :] + _bdot_nt(qtil[h][j], st[h]))
            st[h] = elast[h][j] * st[h] + kv[h][j]
    for h in heads:
        st_ref[h] = st[h]
        oh = jnp.concatenate(outs[h], axis=0)
        oh = oh * lax.rsqrt(jnp.mean(oh * oh, axis=-1, keepdims=True) + NORM_EPS) * nw_ref[...]
        o_ref[:, lanes[h]] = (oh * _silu(g_ref[:, lanes[h]])).astype(o_ref.dtype)

    @pl.when(c == pl.num_programs(2) - 1)
    def _():
        sout_ref[0] = st_ref[...]


def _hgrn(acg, llb, l1m, oml, nw, s0t, b, t):
    tc = min(t, 128)
    nt = t // tc
    hw = HGRN_HPB * HEAD
    ng = N_HEADS // HGRN_HPB
    col_spec = lambda part: pl.BlockSpec((tc, hw), lambda bi, hg, c: (bi * nt + c, part * ng + hg))
    vec_spec = pl.BlockSpec((1, hw), lambda bi, hg, c: (0, hg))
    st_spec = pl.BlockSpec((1, HGRN_HPB, HEAD, HEAD), lambda bi, hg, c: (bi, hg, 0, 0))
    return pl.pallas_call(
        functools.partial(_hgrn_kernel, tc=tc),
        grid=(b, ng, nt),
        in_specs=[col_spec(0), col_spec(1), col_spec(2), col_spec(3),
                  vec_spec, vec_spec, vec_spec, pl.BlockSpec((1, HEAD), lambda bi, hg, c: (0, 0)), st_spec],
        out_specs=[pl.BlockSpec((tc, hw), lambda bi, hg, c: (bi * nt + c, hg)), st_spec],
        out_shape=[jax.ShapeDtypeStruct((b * t, BRANCH_W), BF16),
                   jax.ShapeDtypeStruct((b, N_HEADS, HEAD, HEAD), F32)],
        scratch_shapes=[pltpu.VMEM((HGRN_HPB, HEAD, HEAD), F32)],
        compiler_params=_params("parallel", "parallel", "arbitrary"),
        name="hgrn2",
    )(acg, acg, acg, acg, llb, l1m, oml, nw, s0t)


def _conv_kernel(u_ref, w_ref, buf_ref, o_ref, tail_ref, *, tc):
    c = pl.program_id(1)

    @pl.when(c == 0)
    def _():
        tail_ref[...] = buf_ref[0]

    u = u_ref[...]
    ext = jnp.concatenate([tail_ref[...], u], axis=0)
    acc = u * w_ref[CONV_W - 1:CONV_W, :]
    for j in range(CONV_W - 1):
        acc = acc + ext[5 + j:5 + j + tc, :] * w_ref[j:j + 1, :]
    tail_ref[...] = u[tc - 8:, :]
    y = _silu(acc)
    nqk = B_QK_HEADS * HEAD
    for hh in range(2 * B_QK_HEADS):
        blk = y[:, hh * HEAD:(hh + 1) * HEAD]
        nrm = blk * lax.rsqrt(jnp.sum(blk * blk, axis=-1, keepdims=True) + NORM_EPS)
        if hh < B_QK_HEADS:
            nrm = nrm * (HEAD ** -0.5)
        o_ref[:, hh * HEAD:(hh + 1) * HEAD] = nrm
    o_ref[:, 2 * nqk:] = y[:, 2 * nqk:]


def _gdn_conv(yb, w, buf8, b, t):
    tc = min(t, 256)
    nt = t // tc
    return pl.pallas_call(
        functools.partial(_conv_kernel, tc=tc),
        grid=(b, nt),
        in_specs=[pl.BlockSpec((tc, B_CONV_CH), lambda bi, c: (bi * nt + c, 0)),
                  pl.BlockSpec((CONV_W, B_CONV_CH), lambda bi, c: (0, 0)),
                  pl.BlockSpec((1, 8, B_CONV_CH), lambda bi, c: (bi, 0, 0))],
        out_specs=pl.BlockSpec((tc, B_CONV_CH), lambda bi, c: (bi * nt + c, 0)),
        out_shape=jax.ShapeDtypeStruct((b * t, B_CONV_CH), F32),
        scratch_shapes=[pltpu.VMEM((8, B_CONV_CH), F32)],
        compiler_params=_params("parallel", "arbitrary"),
        name="gdn_conv",
    )(yb, w, buf8)


def _gdn_kernel(qkv_ref, z_ref, sm_ref, alog_ref, dtb_ref, nw_ref, s0_ref, o_ref, sout_ref, st_ref):
    c = pl.program_id(1)
    cs = GDN_CHUNK

    @pl.when(c == 0)
    def _():
        st_ref[...] = s0_ref[0]

    sm = sm_ref[...]
    beta_all = _sigmoid(sm)
    g_all = -jnp.exp(alog_ref[...]) * _softplus(sm + dtb_ref[...])

    r = lax.broadcasted_iota(jnp.int32, (cs, cs), 0)
    s = lax.broadcasted_iota(jnp.int32, (cs, cs), 1)
    lower = r >= s
    strict = r > s
    tril_bf = jnp.where(lower, 1.0, 0.0).astype(BF16)
    triu_bf = jnp.where(r <= s, 1.0, 0.0).astype(BF16)
    above_bf = jnp.where(r < s, 1.0, 0.0).astype(BF16)
    nw = nw_ref[...]
    cum_all = _mask_dot(tril_bf, g_all)
    rem_all = _mask_dot(above_bf, g_all)
    d_tn = lambda a: lax.dot_general(a, triu_bf, (((0,), (0,)), ((), ())), preferred_element_type=F32)
    g_hi, g_mid, g_lo = _split3(g_all)
    cum_rows = d_tn(g_hi) + d_tn(g_mid) + d_tn(g_lo)

    rep = N_HEADS // B_QK_HEADS
    heads = range(N_HEADS)
    q = [qkv_ref[:, qh * HEAD:(qh + 1) * HEAD] for qh in range(B_QK_HEADS)]
    k = [qkv_ref[:, (B_QK_HEADS + qh) * HEAD:(B_QK_HEADS + qh + 1) * HEAD] for qh in range(B_QK_HEADS)]
    kk = [_bdot_nt(k[qh], k[qh]) for qh in range(B_QK_HEADS)]
    qk = [_bdot_nt(q[qh], k[qh]) for qh in range(B_QK_HEADS)]
    cum, rem, dec, x, y = [], [], [], [], []
    for h in heads:
        col = N_HEADS + h
        v = qkv_ref[:, (2 * B_QK_HEADS + h) * HEAD:(2 * B_QK_HEADS + h + 1) * HEAD]
        beta = beta_all[:, h:h + 1]
        cum.append(jnp.broadcast_to(cum_all[:, col:col + 1], (cs, HEAD)))
        rem.append(jnp.broadcast_to(rem_all[:, col:col + 1], (cs, HEAD)))
        seg = cum_all[:, col:col + 1] - cum_rows[col:col + 1, :]
        dec.append(jnp.where(lower, jnp.exp(jnp.where(lower, seg, 0.0)), 0.0))
        x.append(jnp.where(strict, -(kk[h // rep] * beta) * dec[h], 0.0))
        y.append(jnp.concatenate([v * beta, k[h // rep] * beta * jnp.exp(cum[h])], axis=1))
    for lvl in range(len(GDN_APPLY_PASSES)):
        y = [y[h] + _dot_p(x[h], y[h], GDN_APPLY_PASSES[lvl]) for h in heads]
        if lvl < len(GDN_SQUARE_PASSES):
            x = [_dot_p(x[h], x[h], GDN_SQUARE_PASSES[lvl]) for h in heads]
    st = [st_ref[h] for h in heads]
    u = [y[h][:, :HEAD] - _bdot(y[h][:, HEAD:], st[h]) for h in heads]
    o = [_bdot(q[h // rep] * jnp.exp(cum[h]), st[h]) + _bdot(qk[h // rep] * dec[h], u[h]) for h in heads]
    for h in heads:
        st_ref[h] = jnp.exp(cum[h][cs - 1:cs, :]) * st[h] + _bdot_tn(k[h // rep] * jnp.exp(rem[h]), u[h])
    for h in heads:
        on = o[h] * lax.rsqrt(jnp.mean(o[h] * o[h], axis=-1, keepdims=True) + NORM_EPS) * nw
        zg = z_ref[:, h * HEAD:(h + 1) * HEAD]
        o_ref[:, h * HEAD:(h + 1) * HEAD] = (on * _silu(zg)).astype(o_ref.dtype)

    @pl.when(c == pl.num_programs(1) - 1)
    def _():
        sout_ref[0] = st_ref[...]


def _gdn(qkvc, yb, alog_row, dtb_row, nw, s0, b, t):
    cs = GDN_CHUNK
    nt = t // cs
    st_spec = pl.BlockSpec((1, N_HEADS, HEAD, HEAD), lambda bi, c: (bi, 0, 0, 0))
    row_spec = pl.BlockSpec((1, HEAD), lambda bi, c: (0, 0))
    return pl.pallas_call(
        _gdn_kernel,
        grid=(b, nt),
        in_specs=[pl.BlockSpec((cs, B_CONV_CH), lambda bi, c: (bi * nt + c, 0)),
                  pl.BlockSpec((cs, BRANCH_W), lambda bi, c: (bi * nt + c, B_CONV_CH // BRANCH_W)),
                  pl.BlockSpec((cs, HEAD), lambda bi, c: (bi * nt + c, B_SMALL_COL // HEAD)),
                  row_spec, row_spec, row_spec, st_spec],
        out_specs=[pl.BlockSpec((cs, BRANCH_W), lambda bi, c: (bi * nt + c, 0)), st_spec],
        out_shape=[jax.ShapeDtypeStruct((b * t, BRANCH_W), BF16),
                   jax.ShapeDtypeStruct((b, N_HEADS, HEAD, HEAD), F32)],
        scratch_shapes=[pltpu.VMEM((N_HEADS, HEAD, HEAD), F32)],
        compiler_params=_params("parallel", "arbitrary"),
        name="gated_delta",
    )(qkvc, yb, yb, alog_row, dtb_row, nw, s0)


def _ret_kernel(lg_ref, q_ref, k_ref, v_ref, g_ref, cos_ref, sin_ref, nw_ref, s0_ref,
                o_ref, sout_ref, st_ref, *, cs):
    c = pl.program_id(1)

    @pl.when(c == 0)
    def _():
        st_ref[...] = s0_ref[0]

    heads = range(N_HEADS)
    lanes = [slice(h * HEAD, (h + 1) * HEAD) for h in heads]
    cos = cos_ref[...]
    sin = sin_ref[...]

    def rot(x):
        return x * cos + pltpu.roll(x, HEAD // 2, 1) * sin

    idx = lax.broadcasted_iota(jnp.int32, (cs, HEAD), 0).astype(F32)
    rel = (lax.broadcasted_iota(jnp.int32, (cs, cs), 0)
           - lax.broadcasted_iota(jnp.int32, (cs, cs), 1))
    relf = jnp.maximum(rel, 0).astype(F32)
    lg = [lg_ref[h] for h in heads]
    q = [rot(q_ref[:, lanes[h]]) for h in heads]
    k = [rot(k_ref[:, lanes[h]]) * (HEAD ** -0.5) for h in heads]
    att = [_bdot_nt(q[h], k[h]) * jnp.where(rel >= 0, jnp.exp(relf * lg[h]), 0.0) for h in heads]
    st = [st_ref[h] for h in heads]
    o = [_bdot(q[h], st[h]) * jnp.exp((idx + 1.0) * lg[h]) + _bdot(att[h], v_ref[:, lanes[h]]) for h in heads]
    for h in heads:
        c_dec = jnp.exp(jnp.zeros((1, HEAD), F32) + cs * lg[h])
        st_ref[h] = c_dec * st[h] + _bdot_tn(k[h] * jnp.exp((cs - 1.0 - idx) * lg[h]), v_ref[:, lanes[h]])
    for h in heads:
        mu = jnp.mean(o[h], axis=-1, keepdims=True)
        var = jnp.mean(jnp.square(o[h] - mu), axis=-1, keepdims=True)
        on = (o[h] - mu) * lax.rsqrt(var + NORM_EPS) * nw_ref[...]
        o_ref[:, lanes[h]] = (on * _silu(g_ref[:, lanes[h]])).astype(o_ref.dtype)

    @pl.when(c == pl.num_programs(1) - 1)
    def _():
        sout_ref[0] = st_ref[...]


def _retention(acg, lg, cos, sin, nw, s0, b, t):
    cs = min(t, 512)
    nt = t // cs
    col_spec = lambda part: pl.BlockSpec((cs, BRANCH_W), lambda bi, c, lg_ref: (bi * nt + c, 4 + part))
    tab_spec = pl.BlockSpec((cs, HEAD), lambda bi, c, lg_ref: (c, 0))
    st_spec = pl.BlockSpec((1, N_HEADS, HEAD, HEAD), lambda bi, c, lg_ref: (bi, 0, 0, 0))
    return pl.pallas_call(
        functools.partial(_ret_kernel, cs=cs),
        grid_spec=pltpu.PrefetchScalarGridSpec(
            num_scalar_prefetch=1,
            grid=(b, nt),
            in_specs=[col_spec(0), col_spec(1), col_spec(2), col_spec(3), tab_spec, tab_spec,
                      pl.BlockSpec((1, HEAD), lambda bi, c, lg_ref: (0, 0)), st_spec],
            out_specs=[pl.BlockSpec((cs, BRANCH_W), lambda bi, c, lg_ref: (bi * nt + c, 0)), st_spec],
            scratch_shapes=[pltpu.VMEM((N_HEADS, HEAD, HEAD), F32)]),
        out_shape=[jax.ShapeDtypeStruct((b * t, BRANCH_W), BF16),
                   jax.ShapeDtypeStruct((b, N_HEADS, HEAD, HEAD), F32)],
        compiler_params=_params("parallel", "arbitrary"),
        name="retention",
    )(lg, acg, acg, acg, acg, cos, sin, nw, s0)


def _merge_kernel(oa_ref, ob_ref, oc_ref, wb_ref, ga_ref, gb_ref, gc_ref, o_ref):
    acc = _sigmoid(ga_ref[...]) * jnp.dot(oa_ref[...], wb_ref[0], preferred_element_type=F32)
    acc += _sigmoid(gb_ref[...]) * jnp.dot(ob_ref[...], wb_ref[1], preferred_element_type=F32)
    acc += _sigmoid(gc_ref[...]) * jnp.dot(oc_ref[...], wb_ref[2], preferred_element_type=F32)
    o_ref[...] = acc.astype(o_ref.dtype)


def _merge(oa, ob, oc, wb, acg, tm):
    n = oa.shape[0]
    tn = 512
    o_spec = pl.BlockSpec((tm, BRANCH_W), lambda i, j: (i, 0))
    gate_spec = lambda br: pl.BlockSpec(
        (tm, tn), lambda i, j: (i, (GATE_COL0 + br * D_MODEL) // tn + j))
    return pl.pallas_call(
        _merge_kernel,
        grid=(n // tm, D_MODEL // tn),
        in_specs=[o_spec, o_spec, o_spec,
                  pl.BlockSpec((N_BRANCH, BRANCH_W, tn), lambda i, j: (0, 0, j)),
                  gate_spec(0), gate_spec(1), gate_spec(2)],
        out_specs=pl.BlockSpec((tm, tn), lambda i, j: (i, j)),
        out_shape=jax.ShapeDtypeStruct((n, D_MODEL), BF16),
        compiler_params=_params("parallel", "parallel"),
        name="branch_merge",
    )(oa, ob, oc, wb, acg, acg, acg)


def _layer_norm(y, g, b):
    mu = jnp.mean(y, axis=-1, keepdims=True)
    var = jnp.mean(jnp.square(y - mu), axis=-1, keepdims=True)
    return (y - mu) * lax.rsqrt(var + LN_EPS) * g + b


def _outproj_kernel(m_ref, x_ref, w_ref, g_ref, b_ref, wrh_ref, wrl_ref, xo_ref, xb_ref, sc_ref):
    h = jnp.dot(m_ref[...], w_ref[...], preferred_element_type=F32)
    x1 = _layer_norm(DN_ALPHA * x_ref[...] + h, g_ref[...], b_ref[...])
    xo_ref[...] = x1
    xh, xl = _split2(x1)
    xb_ref[...] = xh
    d = lambda a, b_: lax.dot_general(a, b_, (((1,), (1,)), ((), ())), preferred_element_type=F32)
    logits = d(wrh_ref[...], xh) + d(wrh_ref[...], xl) + d(wrl_ref[...], xh)
    sc_ref[...] = _sigmoid(logits)


def _outproj(merged, x, w_out, g, b, wr_hi, wr_lo, tm):
    n = x.shape[0]
    row = pl.BlockSpec((tm, D_MODEL), lambda i: (i, 0))
    vec = pl.BlockSpec((1, D_MODEL), lambda i: (0, 0))
    wr = pl.BlockSpec((N_EXPERTS, D_MODEL), lambda i: (0, 0))
    return pl.pallas_call(
        _outproj_kernel,
        grid=(n // tm,),
        in_specs=[row, row, pl.BlockSpec((D_MODEL, D_MODEL), lambda i: (0, 0)), vec, vec, wr, wr],
        out_specs=[row, row, pl.BlockSpec((N_EXPERTS, tm), lambda i: (0, i))],
        out_shape=[jax.ShapeDtypeStruct((n, D_MODEL), F32), jax.ShapeDtypeStruct((n, D_MODEL), BF16),
                   jax.ShapeDtypeStruct((N_EXPERTS, n), F32)],
        compiler_params=_params("parallel"),
        name="outproj_ln_router",
    )(merged, x, w_out, g, b, wr_hi, wr_lo)


def _route_kernel(sc_ref, bias_ref, e_ref, pos_ref, w_ref, cnt_ref, carry_ref, *, tm):
    i = pl.program_id(0)
    per_group = N_EXPERTS // N_GROUPS

    @pl.when(i == 0)
    def _():
        carry_ref[...] = jnp.zeros_like(carry_ref)

    s = sc_ref[...]
    c3 = (s + bias_ref[...]).reshape(N_GROUPS, per_group, tm)
    midx = lax.broadcasted_iota(jnp.int32, (N_GROUPS, per_group, tm), 1)
    gidx = lax.broadcasted_iota(jnp.int32, (N_GROUPS, per_group, tm), 0)

    def beats(other, mine, other_first):
        return jnp.where((other > mine) | ((other == mine) & other_first), 1, 0)

    r1 = jnp.zeros((N_GROUPS, per_group, tm), jnp.int32)
    for m in range(per_group):
        r1 = r1 + beats(c3[:, m:m + 1, :], c3, m < midx)
    gs = jnp.sum(jnp.where(r1 < 2, c3, 0.0), axis=1, keepdims=True)
    gs = jnp.broadcast_to(gs, (N_GROUPS, per_group, tm))
    r2 = jnp.zeros((N_GROUPS, per_group, tm), jnp.int32)
    for g in range(N_GROUPS):
        r2 = r2 + beats(gs[g:g + 1], gs, g < gidx)
    masked = jnp.where(r2 < TOPK_GROUPS, c3, -jnp.inf).reshape(N_EXPERTS, tm)
    eidx = lax.broadcasted_iota(jnp.int32, (N_EXPERTS, tm), 0)
    r3 = jnp.zeros((N_EXPERTS, tm), jnp.int32)
    for e in range(N_EXPERTS):
        r3 = r3 + beats(masked[e:e + 1, :], masked, e < eidx)
    sel = r3 < TOP_K
    w = jnp.where(sel, s, 0.0)
    w = w / jnp.sum(w, axis=0, keepdims=True) * ROUTE_SCALE
    sel_bf = jnp.where(sel, 1.0, 0.0).astype(BF16)
    tr = lax.broadcasted_iota(jnp.int32, (tm, tm), 0)
    tc_ = lax.broadcasted_iota(jnp.int32, (tm, tm), 1)
    before = jnp.where(tr < tc_, 1.0, 0.0).astype(BF16)
    carry = carry_ref[...]
    pos = (jnp.dot(sel_bf, before, preferred_element_type=F32)
           + jnp.concatenate([carry] * (tm // HEAD), axis=1))
    carry_ref[...] = carry + jnp.dot(sel_bf, jnp.ones((tm, HEAD), BF16), preferred_element_type=F32)
    er = lax.broadcasted_iota(jnp.int32, (N_EXPERTS, N_EXPERTS), 0)
    ec = lax.broadcasted_iota(jnp.int32, (N_EXPERTS, N_EXPERTS), 1)
    lower_e = jnp.where(ec < er, 1.0, 0.0).astype(BF16)
    jidx = jnp.dot(lower_e, sel_bf, preferred_element_type=F32)
    eidx_f = eidx.astype(F32)
    e_rows, p_rows, w_rows = [], [], []
    for j in range(TOP_K):
        oh = sel & (jidx == float(j))
        e_rows.append(jnp.sum(jnp.where(oh, eidx_f, 0.0), axis=0, keepdims=True))
        p_rows.append(jnp.sum(jnp.where(oh, pos, 0.0), axis=0, keepdims=True))
        w_rows.append(jnp.sum(jnp.where(oh, w, 0.0), axis=0, keepdims=True))
    zpad = [jnp.zeros((8 - TOP_K, tm), F32)]
    e_ref[...] = jnp.concatenate(e_rows + zpad, axis=0).astype(jnp.int32)
    pos_ref[...] = jnp.concatenate(p_rows + zpad, axis=0).astype(jnp.int32)
    w_ref[...] = jnp.concatenate(w_rows + zpad, axis=0)

    @pl.when(i == pl.num_programs(0) - 1)
    def _():
        cnt_ref[...] = carry_ref[...]


def _route(scores_t, bias_b, tm):
    n = scores_t.shape[1]
    blk = pl.BlockSpec((8, tm), lambda i: (0, i))
    return pl.pallas_call(
        functools.partial(_route_kernel, tm=tm),
        grid=(n // tm,),
        in_specs=[pl.BlockSpec((N_EXPERTS, tm), lambda i: (0, i)),
                  pl.BlockSpec((N_EXPERTS, tm), lambda i: (0, 0))],
        out_specs=[blk, blk, blk, pl.BlockSpec((N_EXPERTS, HEAD), lambda i: (0, 0))],
        out_shape=[jax.ShapeDtypeStruct((8, n), jnp.int32), jax.ShapeDtypeStruct((8, n), jnp.int32),
                   jax.ShapeDtypeStruct((8, n), F32), jax.ShapeDtypeStruct((N_EXPERTS, HEAD), F32)],
        scratch_shapes=[pltpu.VMEM((N_EXPERTS, HEAD), F32)],
        compiler_params=_params("arbitrary"),
        name="route_topk",
    )(scores_t, bias_b)


def _expert_kernel(be_ref, s_m3, s_m2, s_m1, s_cur, s_p1, s_p2, x_hbm, wg_ref, wu_ref, wd_ref,
                   out_hbm, xbuf, ybuf, sem_in, sem_out, *, n_tok):
    i = pl.program_id(0)
    last = pl.num_programs(0) - 1
    slot = i % EXPERT_BUFS
    nxt = (i + 2) % EXPERT_BUFS
    prv = nxt

    def buf_row(k):
        return k * ROW_PITCH if isinstance(k, int) else pl.multiple_of(k * ROW_PITCH, 8)

    def in_copy(slots_ref, buf, k):
        src = pl.multiple_of((slots_ref[0, 0, k] >> 3) * ROW_TILES, ROW_TILES)
        return pltpu.make_async_copy(x_hbm.at[pl.ds(src, ROW_TILES)],
                                     xbuf.at[buf, pl.ds(buf_row(k), ROW_TILES)], sem_in.at[buf, k])

    def out_copy(slots_ref, buf, k):
        v = slots_ref[0, 0, k]
        dst = pl.multiple_of(((v & 7) * n_tok + (v >> 3)) * ROW_TILES, ROW_TILES)
        return pltpu.make_async_copy(ybuf.at[buf, pl.ds(buf_row(k), ROW_TILES)],
                                     out_hbm.at[pl.ds(dst, ROW_TILES)], sem_out.at[buf, k])

    def rolled(fn):
        def body(k, carry):
            fn(k)
            return carry
        lax.fori_loop(0, MOE_BLOCK, body, 0, unroll=8)

    @pl.when(i == 0)
    def _():
        rolled(lambda k: in_copy(s_cur, 0, k).start())
        rolled(lambda k: in_copy(s_p1, 1, k).start())

    rolled(lambda k: in_copy(s_cur, slot, k).wait())

    @pl.when(i >= EXPERT_BUFS)
    def _():
        rolled(lambda k: out_copy(s_m3, slot, k).wait())

    def block(gather_ahead, scatter_prev):
        col = lambda ref, s_: ref.at[slot][pl.ds(s_, MOE_BLOCK, stride=ROW_PITCH), :]
        xb = jnp.concatenate([col(xbuf, s_) for s_ in range(ROW_TILES)], axis=1).astype(BF16)
        for k in range(MOE_BLOCK):
            if gather_ahead:
                in_copy(s_p2, nxt, k).start()
            if scatter_prev:
                out_copy(s_m1, prv, k).start()
        hg = jnp.dot(xb, wg_ref[0], preferred_element_type=F32)
        hu = jnp.dot(xb, wu_ref[0], preferred_element_type=F32)
        hb = (_silu(hg) * hu).astype(BF16)
        y = jnp.dot(hb, wd_ref[0], preferred_element_type=F32)
        for s_ in range(ROW_TILES):
            ybuf.at[slot][pl.ds(s_, MOE_BLOCK, stride=ROW_PITCH), :] = y[:, s_ * HEAD:(s_ + 1) * HEAD]

    @pl.when(i == 0)
    def _():
        block(True, False)

    @pl.when((i > 0) & (i < last - 1))
    def _():
        block(True, True)

    @pl.when(i == last - 1)
    def _():
        block(False, True)

    @pl.when(i == last)
    def _():
        block(False, True)
        rolled(lambda k: out_copy(s_m2, (i + 1) % EXPERT_BUFS, k).wait())
        rolled(lambda k: out_copy(s_cur, slot, k).start())
        rolled(lambda k: out_copy(s_m1, prv, k).wait())
        rolled(lambda k: out_copy(s_cur, slot, k).wait())


def _experts(x1r, block_e, slots, wg, wu, wd):
    n = x1r.shape[0] // ROW_TILES
    n_blocks = block_e.shape[0]
    slot_spec = lambda off: pl.BlockSpec(
        (1, 1, MOE_BLOCK), lambda i, be: (jnp.clip(i + off, 0, n_blocks - 1), 0, 0), memory_space=pltpu.SMEM)
    buf = pltpu.VMEM((EXPERT_BUFS, MOE_BLOCK * ROW_PITCH, HEAD), F32)
    sem = pltpu.SemaphoreType.DMA((EXPERT_BUFS, MOE_BLOCK))
    return pl.pallas_call(
        functools.partial(_expert_kernel, n_tok=n),
        grid_spec=pltpu.PrefetchScalarGridSpec(
            num_scalar_prefetch=1,
            grid=(n_blocks,),
            in_specs=[slot_spec(off) for off in (-3, -2, -1, 0, 1, 2)] + [
                pl.BlockSpec(memory_space=pl.ANY),
                pl.BlockSpec((1, D_MODEL, EXPERT_FF), lambda i, be: (be[i], 0, 0)),
                pl.BlockSpec((1, D_MODEL, EXPERT_FF), lambda i, be: (be[i], 0, 0)),
                pl.BlockSpec((1, EXPERT_FF, D_MODEL), lambda i, be: (be[i], 0, 0))],
            out_specs=pl.BlockSpec(memory_space=pl.ANY),
            scratch_shapes=[buf, buf, sem, sem]),
        out_shape=jax.ShapeDtypeStruct(((TOP_K * n + 2 * MOE_BLOCK) * ROW_TILES, HEAD), F32),
        compiler_params=_params("arbitrary"),
        name="routed_experts",
    )(block_e, *([slots] * 6), x1r, wg, wu, wd)


def _slot_tables(e6, pos6, counts):
    n = e6.shape[1]
    n_assign = n * TOP_K
    n_blocks = -(-n_assign // MOE_BLOCK) + N_EXPERTS
    padded = (counts + MOE_BLOCK - 1) // MOE_BLOCK * MOE_BLOCK
    pad_end = jnp.cumsum(padded)
    pad_start = pad_end - padded
    onehot = e6[:TOP_K, :, None] == jnp.arange(N_EXPERTS, dtype=jnp.int32)
    dest = jnp.sum(jnp.where(onehot, pad_start, 0), axis=-1) + pos6[:TOP_K]
    packed = (jnp.arange(n, dtype=jnp.int32)[None, :] * 8
              + jnp.arange(TOP_K, dtype=jnp.int32)[:, None])
    pad_slots = (jnp.arange(n_blocks * MOE_BLOCK, dtype=jnp.int32) % (2 * MOE_BLOCK)) * 8 + TOP_K
    slots = pad_slots.at[dest.reshape(-1)].set(packed.reshape(-1), unique_indices=True)
    blk_start = jnp.arange(n_blocks, dtype=jnp.int32) * MOE_BLOCK
    block_e = jnp.minimum(jnp.sum((pad_end[None, :] <= blk_start[:, None]).astype(jnp.int32), axis=1),
                          N_EXPERTS - 1)
    return block_e, slots.reshape(n_blocks, 1, MOE_BLOCK)


def _combine_kernel(*refs):
    r_refs = refs[:TOP_K]
    w6_ref, x_ref, xb_ref, wg_ref, wu_ref, wd_ref, g_ref, b_ref, xo_ref, xob_ref = refs[TOP_K:]
    xb = xb_ref[...]
    hb = (_silu(jnp.dot(xb, wg_ref[...], preferred_element_type=F32))
          * jnp.dot(xb, wu_ref[...], preferred_element_type=F32)).astype(BF16)
    shared = jnp.dot(hb, wd_ref[...], preferred_element_type=F32)
    w6 = w6_ref[...]
    tm = w6.shape[0]
    rows2d = lambda ref: jnp.concatenate(
        [ref[pl.ds(s_, tm, stride=ROW_TILES), :] for s_ in range(ROW_TILES)], axis=1)
    routed = rows2d(r_refs[0]) * w6[:, 0:1]
    for j in range(1, TOP_K):
        routed = routed + rows2d(r_refs[j]) * w6[:, j:j + 1]
    x2 = _layer_norm(DN_ALPHA * x_ref[...] + (routed + shared), g_ref[...], b_ref[...])
    xo_ref[...] = x2
    xob_ref[...] = x2.astype(BF16)


def _combine(routed, w6, x1, x1b, wg, wu, wd, g, b, tm):
    n = x1.shape[0]
    nt = n // tm
    row = pl.BlockSpec((tm, D_MODEL), lambda i: (i, 0))
    vec = pl.BlockSpec((1, D_MODEL), lambda i: (0, 0))
    w_in = pl.BlockSpec((D_MODEL, EXPERT_FF), lambda i: (0, 0))
    r_specs = [pl.BlockSpec((tm * ROW_TILES, HEAD), functools.partial(lambda i, j: (j * nt + i, 0), j=j))
               for j in range(TOP_K)]
    return pl.pallas_call(
        _combine_kernel,
        grid=(nt,),
        in_specs=r_specs + [pl.BlockSpec((tm, 8), lambda i: (i, 0)), row, row, w_in, w_in,
                            pl.BlockSpec((EXPERT_FF, D_MODEL), lambda i: (0, 0)), vec, vec],
        out_specs=[row, row],
        out_shape=[jax.ShapeDtypeStruct((n, D_MODEL), F32), jax.ShapeDtypeStruct((n, D_MODEL), BF16)],
        compiler_params=_params("parallel"),
        name="combine_shared_ln",
    )(*([routed] * TOP_K), w6, x1, x1b, wg, wu, wd, g, b)


def _ple_kernel(x_ref, xb_ref, p_ref, wg_ref, wp_ref, xo_ref, xob_ref):
    gate = _sigmoid(jnp.dot(xb_ref[...], wg_ref[...], preferred_element_type=F32))
    pe = jnp.dot(p_ref[...].astype(BF16), wp_ref[...], preferred_element_type=F32)
    x3 = x_ref[...] + pe * gate
    xo_ref[...] = x3
    xob_ref[...] = x3.astype(BF16)


def _ple(x2, x2b, p, wg, wp, tm):
    n = x2.shape[0]
    row = pl.BlockSpec((tm, D_MODEL), lambda i: (i, 0))
    return pl.pallas_call(
        _ple_kernel,
        grid=(n // tm,),
        in_specs=[row, row, pl.BlockSpec((tm, PLE_DIM), lambda i: (i, 0)),
                  pl.BlockSpec((D_MODEL, D_MODEL), lambda i: (0, 0)),
                  pl.BlockSpec((PLE_DIM, D_MODEL), lambda i: (0, 0))],
        out_specs=[row, row],
        out_shape=[jax.ShapeDtypeStruct((n, D_MODEL), F32), jax.ShapeDtypeStruct((n, D_MODEL), BF16)],
        compiler_params=_params("parallel"),
        name="ple",
    )(x2, x2b, p, wg, wp)


def _prep_weights(w_in, hgrn_lb_logits, gdn_a_log, gdn_dt_bias, w_branch, w_out, w_router,
                  w_exp_gate, w_exp_up, w_exp_down, w_sh_gate, w_sh_up, w_sh_down, w_ple_gate, w_ple_proj):
    pts = np.cumsum([1024, 1024, 1024, 1024, 512, 512, 1024, 1024, 8, 8, 1024, 1024, 1024, 1024]).tolist()
    a_end, b_end, c_end = pts[3], pts[9], pts[13]
    w_acg = jnp.concatenate([w_in[:, :, :a_end], w_in[:, :, b_end:c_end], w_in[:, :, c_end:]], axis=-1)
    w_b = jnp.concatenate([w_in[:, :, a_end:b_end],
                           jnp.zeros((DEPTH, D_MODEL, B_COLS - (b_end - a_end)), w_in.dtype)], axis=-1)
    lb_p = jax.nn.softmax(hgrn_lb_logits.astype(F32), axis=0)
    lb = jnp.maximum(jnp.cumsum(lb_p, axis=0) - lb_p[:1], 0.0)
    pad_row = lambda v: jnp.zeros((DEPTH, 1, HEAD), F32).at[:, 0, N_HEADS:2 * N_HEADS].set(v.astype(F32))
    wr = jnp.swapaxes(w_router.astype(F32), -1, -2)
    wr_hi = wr.astype(BF16)
    return dict(
        w_acg=w_acg.astype(BF16), w_b=w_b.astype(BF16),
        llb=jnp.log(lb)[:, None, :], l1m=jnp.log1p(-lb)[:, None, :], oml=(1.0 - lb)[:, None, :],
        alog_row=pad_row(gdn_a_log), dtb_row=pad_row(gdn_dt_bias),
        w_branch=w_branch.astype(BF16), w_out=w_out.astype(BF16),
        wr_hi=wr_hi, wr_lo=(wr - wr_hi.astype(F32)).astype(BF16),
        wg=w_exp_gate.astype(BF16), wu=w_exp_up.astype(BF16), wd=w_exp_down.astype(BF16),
        sg=w_sh_gate.astype(BF16), su=w_sh_up.astype(BF16), sd=w_sh_down.astype(BF16),
        pg=w_ple_gate.astype(BF16), pp=w_ple_proj.astype(BF16))


def _run_trunk(x, p, pos0, states, pw, small):
    (hgrn_norm, gdn_conv, gdn_norm, ret_norm, ln1_g, ln1_b, ln2_g, ln2_b, router_bias) = small
    s_hgrn, s_gdn, s_conv, s_ret = states
    b, t, d = x.shape
    n = b * t
    tm = min(n, 512)
    xf = x.reshape(n, d)
    xb = xf.astype(BF16)
    half = HEAD // 2
    inv = ROPE_BASE ** (-jnp.arange(half, dtype=F32) / half)
    ang = (pos0 + jnp.arange(t, dtype=F32))[:, None] * inv[None, :]
    cos_t = jnp.concatenate([jnp.cos(ang), jnp.cos(ang)], axis=-1)
    sin_t = jnp.concatenate([-jnp.sin(ang), jnp.sin(ang)], axis=-1)
    lg = jnp.log1p(-jnp.exp2(-5.0 - jnp.arange(N_HEADS, dtype=F32)))
    row = lambda v: v.reshape(1, -1).astype(F32)
    new = ([], [], [], [])
    for i in range(DEPTH):
        acg = _matmul(xb, pw['w_acg'][i], min(n, 2048), 1024)
        yb = _matmul(xb, pw['w_b'][i], min(n, 2048), 640)
        o_a, st_a = _hgrn(acg, pw['llb'][i], pw['l1m'][i], pw['oml'][i], row(hgrn_norm[i]),
                          jnp.swapaxes(s_hgrn[i], -1, -2), b, t)
        buf8 = jnp.concatenate([jnp.zeros((b, 8 - (CONV_W - 1), B_CONV_CH), F32), s_conv[i]], axis=1)
        qkvc = _gdn_conv(yb, gdn_conv[i], buf8, b, t)
        o_b, st_b = _gdn(qkvc, yb, pw['alog_row'][i], pw['dtb_row'][i], row(gdn_norm[i]), s_gdn[i], b, t)
        o_c, st_c = _retention(acg, lg, cos_t, sin_t, row(ret_norm[i]), s_ret[i], b, t)
        merged = _merge(o_a, o_b, o_c, pw['w_branch'][i], acg, min(n, 1024))
        x1, x1b, scores = _outproj(merged, xf, pw['w_out'][i], row(ln1_g[i]), row(ln1_b[i]),
                                   pw['wr_hi'][i], pw['wr_lo'][i], tm)
        bias_b = jnp.broadcast_to(router_bias[i].astype(F32)[:, None], (N_EXPERTS, tm))
        e6, pos6, w6, cnt = _route(scores, bias_b, tm)
        block_e, slots = _slot_tables(e6, pos6, cnt[:, 0].astype(jnp.int32))
        routed = _experts(x1.reshape(n * ROW_TILES, HEAD), block_e, slots,
                          pw['wg'][i], pw['wu'][i], pw['wd'][i])
        x2, x2b = _combine(routed, w6.T, x1, x1b, pw['sg'][i], pw['su'][i],
                           pw['sd'][i], row(ln2_g[i]), row(ln2_b[i]), min(n, 128))
        xf, xb = _ple(x2, x2b, p[i].reshape(n, PLE_DIM), pw['pg'][i], pw['pp'][i], tm)
        new[0].append(jnp.swapaxes(st_a, -1, -2))
        new[1].append(st_b)
        new[2].append(yb.reshape(b, t, B_COLS)[:, t - (CONV_W - 1):, :B_CONV_CH])
        new[3].append(st_c)
    return xf.reshape(b, t, d), tuple(jnp.stack(a) for a in new)


def kernel(x_prompt, x_sample, p_prompt, p_sample, state_hgrn, state_gdn, state_gdn_conv, state_ret,
           w_in, hgrn_lb_logits, hgrn_norm, gdn_conv, gdn_a_log, gdn_dt_bias, gdn_norm, ret_norm,
           w_branch, w_out, ln1_g, ln1_b, ln2_g, ln2_b, w_router, router_bias,
           w_exp_gate, w_exp_up, w_exp_down, w_sh_gate, w_sh_up, w_sh_down, w_ple_gate, w_ple_proj):
    pw = _prep_weights(w_in, hgrn_lb_logits, gdn_a_log, gdn_dt_bias, w_branch, w_out, w_router,
                       w_exp_gate, w_exp_up, w_exp_down, w_sh_gate, w_sh_up, w_sh_down,
                       w_ple_gate, w_ple_proj)
    small = (hgrn_norm, gdn_conv, gdn_norm, ret_norm, ln1_g, ln1_b, ln2_g, ln2_b, router_bias)
    bp = x_prompt.shape[0]
    zero_states = (jnp.zeros((DEPTH, bp, N_HEADS, HEAD, HEAD), F32),
                   jnp.zeros((DEPTH, bp, N_HEADS, HEAD, HEAD), F32),
                   jnp.zeros((DEPTH, bp, CONV_W - 1, B_CONV_CH), F32),
                   jnp.zeros((DEPTH, bp, N_HEADS, HEAD, HEAD), F32))
    y_p, (hp, gp, cp, rp) = _run_trunk(x_prompt, p_prompt, 0.0, zero_states, pw, small)
    y_s, (hs, gs, cs, rs) = _run_trunk(x_sample, p_sample, float(PAST_LEN),
                                       (state_hgrn, state_gdn, state_gdn_conv, state_ret), pw, small)
    return (y_p, y_s, hp, gp, cp, rp, hs, gs, cs, rs)
```

```python
import functools
import math

import jax
import jax.numpy as jnp
import numpy as np
from jax import lax
from jax.experimental import pallas as pl
from jax.experimental.pallas import tpu as pltpu

F32 = jnp.float32
BF16 = jnp.bfloat16

D_MODEL = 2048
DEPTH = 4
PAST_LEN = 4096
HEAD = 128
N_HEADS = 8
B_QK_HEADS = 4
CONV_W = 4
BRANCH_W = N_HEADS * HEAD
B_CONV_CH = 2 * B_QK_HEADS * HEAD + N_HEADS * HEAD
ROPE_BASE = 10000.0
N_BRANCH = 3
N_EXPERTS = 64
N_GROUPS = 8
TOPK_GROUPS = 4
TOP_K = 6
EXPERT_FF = 512
ROUTE_SCALE = 2.5
MOE_BLOCK = 256
ROW_TILES = D_MODEL // HEAD
EXPERT_BUFS = 3
ROW_PITCH = 24
PLE_DIM = 256
DN_ALPHA = (2.0 * DEPTH) ** 0.25
LN_EPS = 1e-5
NORM_EPS = 1e-6

HGRN_SUB = 16
HGRN_HPB = 8
GDN_CHUNK = 64
GDN_APPLY_PASSES = (3, 1, 1, 1, 1, 1)
GDN_SQUARE_PASSES = (3, 1, 1, 1, 1)
ACG_COLS = 4 * BRANCH_W + 4 * BRANCH_W + N_BRANCH * D_MODEL
GATE_COL0 = 8 * BRANCH_W
B_COLS = 3200
B_SMALL_COL = 3072

VMEM_LIMIT = 56 * 1024 * 1024


def _params(*sem):
    return pltpu.CompilerParams(dimension_semantics=sem, vmem_limit_bytes=VMEM_LIMIT)


def _bdot(a, b):
    return jnp.dot(a.astype(BF16), b.astype(BF16), preferred_element_type=F32)


def _bdot_nt(a, b):
    return lax.dot_general(a.astype(BF16), b.astype(BF16), (((1,), (1,)), ((), ())),
                           preferred_element_type=F32)


def _bdot_tn(a, b):
    return lax.dot_general(a.astype(BF16), b.astype(BF16), (((0,), (0,)), ((), ())),
                           preferred_element_type=F32)


def _split2(x):
    hi = x.astype(BF16)
    lo = (x - hi.astype(F32)).astype(BF16)
    return hi, lo


def _split3(x):
    hi = x.astype(BF16)
    r = x - hi.astype(F32)
    mid = r.astype(BF16)
    lo = (r - mid.astype(F32)).astype(BF16)
    return hi, mid, lo


def _mask_dot(mask_bf, x):
    hi, mid, lo = _split3(x)
    d = functools.partial(jnp.dot, preferred_element_type=F32)
    return d(mask_bf, hi) + d(mask_bf, mid) + d(mask_bf, lo)


def _dot_x3(a, b):
    ah, al = _split2(a)
    bh, bl = _split2(b)
    d = functools.partial(jnp.dot, preferred_element_type=F32)
    return d(ah, bh) + d(ah, bl) + d(al, bh)


def _dot_p(a, b, passes):
    return _dot_x3(a, b) if passes == 3 else _bdot(a, b)


def _sigmoid(x):
    return 1.0 / (1.0 + jnp.exp(-x))


def _silu(x):
    return x * _sigmoid(x)


def _softplus(x):
    return jnp.maximum(x, 0.0) + jnp.log1p(jnp.exp(-jnp.abs(x)))


def _log_sigmoid(x):
    return jnp.minimum(x, 0.0) - jnp.log1p(jnp.exp(-jnp.abs(x)))


def _mm_kernel(x_ref, w_ref, o_ref):
    o_ref[...] = jnp.dot(x_ref[...], w_ref[...], preferred_element_type=F32).astype(o_ref.dtype)


def _matmul(x, w, tm, tn):
    m, k = x.shape
    n = w.shape[1]
    return pl.pallas_call(
        _mm_kernel,
        grid=(m // tm, n // tn),
        in_specs=[pl.BlockSpec((tm, k), lambda i, j: (i, 0)),
                  pl.BlockSpec((k, tn), lambda i, j: (0, j))],
        out_specs=pl.BlockSpec((tm, tn), lambda i, j: (i, j)),
        out_shape=jax.ShapeDtypeStruct((m, n), F32),
        compiler_params=_params("parallel", "parallel"),
        name="inproj_matmul",
    )(x, w)


def _hgrn_kernel(q_ref, f_ref, i_ref, g_ref, llb_ref, l1m_ref, oml_ref, nw_ref, s0_ref,
                 o_ref, sout_ref, st_ref, *, tc):
    c = pl.program_id(2)

    @pl.when(c == 0)
    def _():
        st_ref[...] = s0_ref[0]

    nb = tc // HGRN_SUB
    heads = range(HGRN_HPB)
    lanes = [slice(h * HEAD, (h + 1) * HEAD) for h in heads]
    shp = (nb, HGRN_SUB, HEAD)
    z = [f_ref[:, lanes[h]] for h in heads]
    bterm = [l1m_ref[:, lanes[h]] + _log_sigmoid(z[h]) for h in heads]
    log_f = [jnp.maximum(llb_ref[:, lanes[h]], bterm[h])
             + jnp.log1p(jnp.exp(-jnp.abs(llb_ref[:, lanes[h]] - bterm[h]))) for h in heads]
    key3 = [(oml_ref[:, lanes[h]] * (1.0 / (1.0 + jnp.exp(z[h])))).reshape(shp) for h in heads]
    r = lax.broadcasted_iota(jnp.int32, (tc, tc), 0)
    s = lax.broadcasted_iota(jnp.int32, (tc, tc), 1)
    seg_mask = jnp.where((r // HGRN_SUB == s // HGRN_SUB) & (s <= r), 1.0, 0.0).astype(BF16)
    cum3 = [_mask_dot(seg_mask, log_f[h]).reshape(shp) for h in heads]
    q3 = [q_ref[:, lanes[h]].reshape(shp) for h in heads]
    v3 = [i_ref[:, lanes[h]].reshape(shp) for h in heads]
    rows = lax.broadcasted_iota(jnp.int32, shp, 1)
    ones_bf = jnp.ones((HEAD, HEAD), BF16)
    att = []
    for h in heads:
        parts = []
        for si in range(HGRN_SUB):
            arg = jnp.where(rows >= si, cum3[h] - cum3[h][:, si:si + 1, :], -1e30)
            parts.append((q3[h] * (key3[h][:, si:si + 1, :] * jnp.exp(arg))).reshape(tc, HEAD))
        p_hi, p_lo = _split2(jnp.concatenate(parts, axis=0))
        att.append(jnp.dot(p_hi, ones_bf, preferred_element_type=F32)
                   + jnp.dot(p_lo, ones_bf, preferred_element_type=F32))
    o = []
    for h in heads:
        acc = jnp.zeros((tc, HEAD), F32)
        for si in range(HGRN_SUB):
            vs = jnp.broadcast_to(v3[h][:, si:si + 1, :], shp).reshape(tc, HEAD)
            acc = acc + att[h][si * tc:(si + 1) * tc, :] * vs
        o.append(acc)
    last3 = [cum3[h][:, HGRN_SUB - 1:HGRN_SUB, :] for h in heads]
    qtil = [q3[h] * jnp.exp(cum3[h]) for h in heads]
    ktil = [key3[h] * jnp.exp(last3[h] - cum3[h]) for h in heads]
    elast = [jnp.exp(last3[h]) for h in heads]
    kv = [[_bdot_tn(v3[h][j], ktil[h][j]) for j in range(nb)] for h in heads]
    st = [st_ref[h] for h in heads]
    outs = [[] for _ in heads]
    for j in range(nb):
        for h in heads:
            outs[h].append(o[h][j * HGRN_SUB:(j + 1) * HGRN_SUB, :] + _bdot_nt(qtil[h][j], st[h]))
            st[h] = elast[h][j] * st[h] + kv[h][j]
    for h in heads:
        st_ref[h] = st[h]
        oh = jnp.concatenate(outs[h], axis=0)
        oh = oh * lax.rsqrt(jnp.mean(oh * oh, axis=-1, keepdims=True) + NORM_EPS) * nw_ref[...]
        o_ref[:, lanes[h]] = (oh * _silu(g_ref[:, lanes[h]])).astype(o_ref.dtype)

    @pl.when(c == pl.num_programs(2) - 1)
    def _():
        sout_ref[0] = st_ref[...]


def _hgrn(acg, llb, l1m, oml, nw, s0t, b, t):
    tc = min(t, 128)
    nt = t // tc
    hw = HGRN_HPB * HEAD
    ng = N_HEADS // HGRN_HPB
    col_spec = lambda part: pl.BlockSpec((tc, hw), lambda bi, hg, c: (bi * nt + c, part * ng + hg))
    vec_spec = pl.BlockSpec((1, hw), lambda bi, hg, c: (0, hg))
    st_spec = pl.BlockSpec((1, HGRN_HPB, HEAD, HEAD), lambda bi, hg, c: (bi, hg, 0, 0))
    return pl.pallas_call(
        functools.partial(_hgrn_kernel, tc=tc),
        grid=(b, ng, nt),
        in_specs=[col_spec(0), col_spec(1), col_spec(2), col_spec(3),
                  vec_spec, vec_spec, vec_spec, pl.BlockSpec((1, HEAD), lambda bi, hg, c: (0, 0)), st_spec],
        out_specs=[pl.BlockSpec((tc, hw), lambda bi, hg, c: (bi * nt + c, hg)), st_spec],
        out_shape=[jax.ShapeDtypeStruct((b * t, BRANCH_W), BF16),
                   jax.ShapeDtypeStruct((b, N_HEADS, HEAD, HEAD), F32)],
        scratch_shapes=[pltpu.VMEM((HGRN_HPB, HEAD, HEAD), F32)],
        compiler_params=_params("parallel", "parallel", "arbitrary"),
        name="hgrn2",
    )(acg, acg, acg, acg, llb, l1m, oml, nw, s0t)


def _conv_kernel(u_ref, w_ref, buf_ref, o_ref, tail_ref, *, tc):
    c = pl.program_id(1)

    @pl.when(c == 0)
    def _():
        tail_ref[...] = buf_ref[0]

    u = u_ref[...]
    ext = jnp.concatenate([tail_ref[...], u], axis=0)
    acc = u * w_ref[CONV_W - 1:CONV_W, :]
    for j in range(CONV_W - 1):
        acc = acc + ext[5 + j:5 + j + tc, :] * w_ref[j:j + 1, :]
    tail_ref[...] = u[tc - 8:, :]
    y = _silu(acc)
    nqk = B_QK_HEADS * HEAD
    for hh in range(2 * B_QK_HEADS):
        blk = y[:, hh * HEAD:(hh + 1) * HEAD]
        nrm = blk * lax.rsqrt(jnp.sum(blk * blk, axis=-1, keepdims=True) + NORM_EPS)
        if hh < B_QK_HEADS:
            nrm = nrm * (HEAD ** -0.5)
        o_ref[:, hh * HEAD:(hh + 1) * HEAD] = nrm
    o_ref[:, 2 * nqk:] = y[:, 2 * nqk:]


def _gdn_conv(yb, w, buf8, b, t):
    tc = min(t, 256)
    nt = t // tc
    return pl.pallas_call(
        functools.partial(_conv_kernel, tc=tc),
        grid=(b, nt),
        in_specs=[pl.BlockSpec((tc, B_CONV_CH), lambda bi, c: (bi * nt + c, 0)),
                  pl.BlockSpec((CONV_W, B_CONV_CH), lambda bi, c: (0, 0)),
                  pl.BlockSpec((1, 8, B_CONV_CH), lambda bi, c: (bi, 0, 0))],
        out_specs=pl.BlockSpec((tc, B_CONV_CH), lambda bi, c: (bi * nt + c, 0)),
        out_shape=jax.ShapeDtypeStruct((b * t, B_CONV_CH), F32),
        scratch_shapes=[pltpu.VMEM((8, B_CONV_CH), F32)],
        compiler_params=_params("parallel", "arbitrary"),
        name="gdn_conv",
    )(yb, w, buf8)


def _gdn_kernel(qkv_ref, z_ref, sm_ref, alog_ref, dtb_ref, nw_ref, s0_ref, o_ref, sout_ref, st_ref):
    c = pl.program_id(1)
    cs = GDN_CHUNK

    @pl.when(c == 0)
    def _():
        st_ref[...] = s0_ref[0]

    sm = sm_ref[...]
    beta_all = _sigmoid(sm)
    g_all = -jnp.exp(alog_ref[...]) * _softplus(sm + dtb_ref[...])

    r = lax.broadcasted_iota(jnp.int32, (cs, cs), 0)
    s = lax.broadcasted_iota(jnp.int32, (cs, cs), 1)
    lower = r >= s
    strict = r > s
    tril_bf = jnp.where(lower, 1.0, 0.0).astype(BF16)
    triu_bf = jnp.where(r <= s, 1.0, 0.0).astype(BF16)
    above_bf = jnp.where(r < s, 1.0, 0.0).astype(BF16)
    nw = nw_ref[...]
    cum_all = _mask_dot(tril_bf, g_all)
    rem_all = _mask_dot(above_bf, g_all)
    d_tn = lambda a: lax.dot_general(a, triu_bf, (((0,), (0,)), ((), ())), preferred_element_type=F32)
    g_hi, g_mid, g_lo = _split3(g_all)
    cum_rows = d_tn(g_hi) + d_tn(g_mid) + d_tn(g_lo)

    rep = N_HEADS // B_QK_HEADS
    heads = range(N_HEADS)
    q = [qkv_ref[:, qh * HEAD:(qh + 1) * HEAD] for qh in range(B_QK_HEADS)]
    k = [qkv_ref[:, (B_QK_HEADS + qh) * HEAD:(B_QK_HEADS + qh + 1) * HEAD] for qh in range(B_QK_HEADS)]
    kk = [_bdot_nt(k[qh], k[qh]) for qh in range(B_QK_HEADS)]
    qk = [_bdot_nt(q[qh], k[qh]) for qh in range(B_QK_HEADS)]
    cum, rem, dec, x, y = [], [], [], [], []
    for h in heads:
        col = N_HEADS + h
        v = qkv_ref[:, (2 * B_QK_HEADS + h) * HEAD:(2 * B_QK_HEADS + h + 1) * HEAD]
        beta = beta_all[:, h:h + 1]
        cum.append(jnp.broadcast_to(cum_all[:, col:col + 1], (cs, HEAD)))
        rem.append(jnp.broadcast_to(rem_all[:, col:col + 1], (cs, HEAD)))
        seg = cum_all[:, col:col + 1] - cum_rows[col:col + 1, :]
        dec.append(jnp.where(lower, jnp.exp(jnp.where(lower, seg, 0.0)), 0.0))
        x.append(jnp.where(strict, -(kk[h // rep] * beta) * dec[h], 0.0))
        y.append(jnp.concatenate([v * beta, k[h // rep] * beta * jnp.exp(cum[h])], axis=1))
    for lvl in range(len(GDN_APPLY_PASSES)):
        y = [y[h] + _dot_p(x[h], y[h], GDN_APPLY_PASSES[lvl]) for h in heads]
        if lvl < len(GDN_SQUARE_PASSES):
            x = [_dot_p(x[h], x[h], GDN_SQUARE_PASSES[lvl]) for h in heads]
    st = [st_ref[h] for h in heads]
    u = [y[h][:, :HEAD] - _bdot(y[h][:, HEAD:], st[h]) for h in heads]
    o = [_bdot(q[h // rep] * jnp.exp(cum[h]), st[h]) + _bdot(qk[h // rep] * dec[h], u[h]) for h in heads]
    for h in heads:
        st_ref[h] = jnp.exp(cum[h][cs - 1:cs, :]) * st[h] + _bdot_tn(k[h // rep] * jnp.exp(rem[h]), u[h])
    for h in heads:
        on = o[h] * lax.rsqrt(jnp.mean(o[h] * o[h], axis=-1, keepdims=True) + NORM_EPS) * nw
        zg = z_ref[:, h * HEAD:(h + 1) * HEAD]
        o_ref[:, h * HEAD:(h + 1) * HEAD] = (on * _silu(zg)).astype(o_ref.dtype)

    @pl.when(c == pl.num_programs(1) - 1)
    def _():
        sout_ref[0] = st_ref[...]


def _gdn(qkvc, yb, alog_row, dtb_row, nw, s0, b, t):
    cs = GDN_CHUNK
    nt = t // cs
    st_spec = pl.BlockSpec((1, N_HEADS, HEAD, HEAD), lambda bi, c: (bi, 0, 0, 0))
    row_spec = pl.BlockSpec((1, HEAD), lambda bi, c: (0, 0))
    return pl.pallas_call(
        _gdn_kernel,
        grid=(b, nt),
        in_specs=[pl.BlockSpec((cs, B_CONV_CH), lambda bi, c: (bi * nt + c, 0)),
                  pl.BlockSpec((cs, BRANCH_W), lambda bi, c: (bi * nt + c, B_CONV_CH // BRANCH_W)),
                  pl.BlockSpec((cs, HEAD), lambda bi, c: (bi * nt + c, B_SMALL_COL // HEAD)),
                  row_spec, row_spec, row_spec, st_spec],
        out_specs=[pl.BlockSpec((cs, BRANCH_W), lambda bi, c: (bi * nt + c, 0)), st_spec],
        out_shape=[jax.ShapeDtypeStruct((b * t, BRANCH_W), BF16),
                   jax.ShapeDtypeStruct((b, N_HEADS, HEAD, HEAD), F32)],
        scratch_shapes=[pltpu.VMEM((N_HEADS, HEAD, HEAD), F32)],
        compiler_params=_params("parallel", "arbitrary"),
        name="gated_delta",
    )(qkvc, yb, yb, alog_row, dtb_row, nw, s0)


def _ret_kernel(lg_ref, q_ref, k_ref, v_ref, g_ref, cos_ref, sin_ref, nw_ref, s0_ref,
                o_ref, sout_ref, st_ref, *, cs):
    c = pl.program_id(1)

    @pl.when(c == 0)
    def _():
        st_ref[...] = s0_ref[0]

    heads = range(N_HEADS)
    lanes = [slice(h * HEAD, (h + 1) * HEAD) for h in heads]
    cos = cos_ref[...]
    sin = sin_ref[...]

    def rot(x):
        return x * cos + pltpu.roll(x, HEAD // 2, 1) * sin

    idx = lax.broadcasted_iota(jnp.int32, (cs, HEAD), 0).astype(F32)
    rel = (lax.broadcasted_iota(jnp.int32, (cs, cs), 0)
           - lax.broadcasted_iota(jnp.int32, (cs, cs), 1))
    relf = jnp.maximum(rel, 0).astype(F32)
    lg = [lg_ref[h] for h in heads]
    q = [rot(q_ref[:, lanes[h]]) for h in heads]
    k = [rot(k_ref[:, lanes[h]]) * (HEAD ** -0.5) for h in heads]
    att = [_bdot_nt(q[h], k[h]) * jnp.where(rel >= 0, jnp.exp(relf * lg[h]), 0.0) for h in heads]
    st = [st_ref[h] for h in heads]
    o = [_bdot(q[h], st[h]) * jnp.exp((idx + 1.0) * lg[h]) + _bdot(att[h], v_ref[:, lanes[h]]) for h in heads]
    for h in heads:
        c_dec = jnp.exp(jnp.zeros((1, HEAD), F32) + cs * lg[h])
        st_ref[h] = c_dec * st[h] + _bdot_tn(k[h] * jnp.exp((cs - 1.0 - idx) * lg[h]), v_ref[:, lanes[h]])
    for h in heads:
        mu = jnp.mean(o[h], axis=-1, keepdims=True)
        var = jnp.mean(jnp.square(o[h] - mu), axis=-1, keepdims=True)
        on = (o[h] - mu) * lax.rsqrt(var + NORM_EPS) * nw_ref[...]
        o_ref[:, lanes[h]] = (on * _silu(g_ref[:, lanes[h]])).astype(o_ref.dtype)

    @pl.when(c == pl.num_programs(1) - 1)
    def _():
        sout_ref[0] = st_ref[...]


def _retention(acg, lg, cos, sin, nw, s0, b, t):
    cs = min(t, 512)
    nt = t // cs
    col_spec = lambda part: pl.BlockSpec((cs, BRANCH_W), lambda bi, c, lg_ref: (bi * nt + c, 4 + part))
    tab_spec = pl.BlockSpec((cs, HEAD), lambda bi, c, lg_ref: (c, 0))
    st_spec = pl.BlockSpec((1, N_HEADS, HEAD, HEAD), lambda bi, c, lg_ref: (bi, 0, 0, 0))
    return pl.pallas_call(
        functools.partial(_ret_kernel, cs=cs),
        grid_spec=pltpu.PrefetchScalarGridSpec(
            num_scalar_prefetch=1,
            grid=(b, nt),
            in_specs=[col_spec(0), col_spec(1), col_spec(2), col_spec(3), tab_spec, tab_spec,
                      pl.BlockSpec((1, HEAD), lambda bi, c, lg_ref: (0, 0)), st_spec],
            out_specs=[pl.BlockSpec((cs, BRANCH_W), lambda bi, c, lg_ref: (bi * nt + c, 0)), st_spec],
            scratch_shapes=[pltpu.VMEM((N_HEADS, HEAD, HEAD), F32)]),
        out_shape=[jax.ShapeDtypeStruct((b * t, BRANCH_W), BF16),
                   jax.ShapeDtypeStruct((b, N_HEADS, HEAD, HEAD), F32)],
        compiler_params=_params("parallel", "arbitrary"),
        name="retention",
    )(lg, acg, acg, acg, acg, cos, sin, nw, s0)


def _merge_kernel(oa_ref, ob_ref, oc_ref, wb_ref, ga_ref, gb_ref, gc_ref, o_ref):
    acc = _sigmoid(ga_ref[...]) * jnp.dot(oa_ref[...], wb_ref[0], preferred_element_type=F32)
    acc += _sigmoid(gb_ref[...]) * jnp.dot(ob_ref[...], wb_ref[1], preferred_element_type=F32)
    acc += _sigmoid(gc_ref[...]) * jnp.dot(oc_ref[...], wb_ref[2], preferred_element_type=F32)
    o_ref[...] = acc.astype(o_ref.dtype)


def _merge(oa, ob, oc, wb, acg, tm):
    n = oa.shape[0]
    tn = 512
    o_spec = pl.BlockSpec((tm, BRANCH_W), lambda i, j: (i, 0))
    gate_spec = lambda br: pl.BlockSpec(
        (tm, tn), lambda i, j: (i, (GATE_COL0 + br * D_MODEL) // tn + j))
    return pl.pallas_call(
        _merge_kernel,
        grid=(n // tm, D_MODEL // tn),
        in_specs=[o_spec, o_spec, o_spec,
                  pl.BlockSpec((N_BRANCH, BRANCH_W, tn), lambda i, j: (0, 0, j)),
                  gate_spec(0), gate_spec(1), gate_spec(2)],
        out_specs=pl.BlockSpec((tm, tn), lambda i, j: (i, j)),
        out_shape=jax.ShapeDtypeStruct((n, D_MODEL), BF16),
        compiler_params=_params("parallel", "parallel"),
        name="branch_merge",
    )(oa, ob, oc, wb, acg, acg, acg)


def _layer_norm(y, g, b):
    mu = jnp.mean(y, axis=-1, keepdims=True)
    var = jnp.mean(jnp.square(y - mu), axis=-1, keepdims=True)
    return (y - mu) * lax.rsqrt(var + LN_EPS) * g + b


def _outproj_kernel(m_ref, x_ref, w_ref, g_ref, b_ref, wrh_ref, wrl_ref, xo_ref, xr_ref, xb_ref, sc_ref):
    h = jnp.dot(m_ref[...], w_ref[...], preferred_element_type=F32)
    x1 = _layer_norm(DN_ALPHA * x_ref[...] + h, g_ref[...], b_ref[...])
    xo_ref[...] = x1
    tm = x1.shape[0]
    for s_ in range(ROW_TILES):
        xr_ref[pl.ds(s_, tm, stride=ROW_TILES), :] = x1[:, s_ * HEAD:(s_ + 1) * HEAD]
    xh, xl = _split2(x1)
    xb_ref[...] = xh
    d = lambda a, b_: lax.dot_general(a, b_, (((1,), (1,)), ((), ())), preferred_element_type=F32)
    logits = d(wrh_ref[...], xh) + d(wrh_ref[...], xl) + d(wrl_ref[...], xh)
    sc_ref[...] = _sigmoid(logits)


def _outproj(merged, x, w_out, g, b, wr_hi, wr_lo, tm):
    n = x.shape[0]
    row = pl.BlockSpec((tm, D_MODEL), lambda i: (i, 0))
    vec = pl.BlockSpec((1, D_MODEL), lambda i: (0, 0))
    wr = pl.BlockSpec((N_EXPERTS, D_MODEL), lambda i: (0, 0))
    return pl.pallas_call(
        _outproj_kernel,
        grid=(n // tm,),
        in_specs=[row, row, pl.BlockSpec((D_MODEL, D_MODEL), lambda i: (0, 0)), vec, vec, wr, wr],
        out_specs=[row, pl.BlockSpec((tm * ROW_TILES, HEAD), lambda i: (i, 0)), row,
                   pl.BlockSpec((N_EXPERTS, tm), lambda i: (0, i))],
        out_shape=[jax.ShapeDtypeStruct((n, D_MODEL), F32), jax.ShapeDtypeStruct((n * ROW_TILES, HEAD), F32),
                   jax.ShapeDtypeStruct((n, D_MODEL), BF16), jax.ShapeDtypeStruct((N_EXPERTS, n), F32)],
        compiler_params=_params("parallel"),
        name="outproj_ln_router",
    )(merged, x, w_out, g, b, wr_hi, wr_lo)


def _route_kernel(sc_ref, bias_ref, e_ref, pos_ref, w_ref, cnt_ref, carry_ref, *, tm):
    i = pl.program_id(0)
    per_group = N_EXPERTS // N_GROUPS

    @pl.when(i == 0)
    def _():
        carry_ref[...] = jnp.zeros_like(carry_ref)

    s = sc_ref[...]
    c3 = (s + bias_ref[...]).reshape(N_GROUPS, per_group, tm)
    midx = lax.broadcasted_iota(jnp.int32, (N_GROUPS, per_group, tm), 1)
    gidx = lax.broadcasted_iota(jnp.int32, (N_GROUPS, per_group, tm), 0)

    def beats(other, mine, other_first):
        return jnp.where((other > mine) | ((other == mine) & other_first), 1, 0)

    r1 = jnp.zeros((N_GROUPS, per_group, tm), jnp.int32)
    for m in range(per_group):
        r1 = r1 + beats(c3[:, m:m + 1, :], c3, m < midx)
    gs = jnp.sum(jnp.where(r1 < 2, c3, 0.0), axis=1, keepdims=True)
    gs = jnp.broadcast_to(gs, (N_GROUPS, per_group, tm))
    r2 = jnp.zeros((N_GROUPS, per_group, tm), jnp.int32)
    for g in range(N_GROUPS):
        r2 = r2 + beats(gs[g:g + 1], gs, g < gidx)
    masked = jnp.where(r2 < TOPK_GROUPS, c3, -jnp.inf).reshape(N_EXPERTS, tm)
    eidx = lax.broadcasted_iota(jnp.int32, (N_EXPERTS, tm), 0)
    r3 = jnp.zeros((N_EXPERTS, tm), jnp.int32)
    for e in range(N_EXPERTS):
        r3 = r3 + beats(masked[e:e + 1, :], masked, e < eidx)
    sel = r3 < TOP_K
    w = jnp.where(sel, s, 0.0)
    w = w / jnp.sum(w, axis=0, keepdims=True) * ROUTE_SCALE
    sel_bf = jnp.where(sel, 1.0, 0.0).astype(BF16)
    tr = lax.broadcasted_iota(jnp.int32, (tm, tm), 0)
    tc_ = lax.broadcasted_iota(jnp.int32, (tm, tm), 1)
    before = jnp.where(tr < tc_, 1.0, 0.0).astype(BF16)
    carry = carry_ref[...]
    pos = (jnp.dot(sel_bf, before, preferred_element_type=F32)
           + jnp.concatenate([carry] * (tm // HEAD), axis=1))
    carry_ref[...] = carry + jnp.dot(sel_bf, jnp.ones((tm, HEAD), BF16), preferred_element_type=F32)
    er = lax.broadcasted_iota(jnp.int32, (N_EXPERTS, N_EXPERTS), 0)
    ec = lax.broadcasted_iota(jnp.int32, (N_EXPERTS, N_EXPERTS), 1)
    lower_e = jnp.where(ec < er, 1.0, 0.0).astype(BF16)
    jidx = jnp.dot(lower_e, sel_bf, preferred_element_type=F32)
    eidx_f = eidx.astype(F32)
    e_rows, p_rows, w_rows = [], [], []
    for j in range(TOP_K):
        oh = sel & (jidx == float(j))
        e_rows.append(jnp.sum(jnp.where(oh, eidx_f, 0.0), axis=0, keepdims=True))
        p_rows.append(jnp.sum(jnp.where(oh, pos, 0.0), axis=0, keepdims=True))
        w_rows.append(jnp.sum(jnp.where(oh, w, 0.0), axis=0, keepdims=True))
    zpad = [jnp.zeros((8 - TOP_K, tm), F32)]
    e_ref[...] = jnp.concatenate(e_rows + zpad, axis=0).astype(jnp.int32)
    pos_ref[...] = jnp.concatenate(p_rows + zpad, axis=0).astype(jnp.int32)
    w_ref[...] = jnp.concatenate(w_rows + zpad, axis=0)

    @pl.when(i == pl.num_programs(0) - 1)
    def _():
        cnt_ref[...] = carry_ref[...]


def _route(scores_t, bias_b, tm):
    n = scores_t.shape[1]
    blk = pl.BlockSpec((8, tm), lambda i: (0, i))
    return pl.pallas_call(
        functools.partial(_route_kernel, tm=tm),
        grid=(n // tm,),
        in_specs=[pl.BlockSpec((N_EXPERTS, tm), lambda i: (0, i)),
                  pl.BlockSpec((N_EXPERTS, tm), lambda i: (0, 0))],
        out_specs=[blk, blk, blk, pl.BlockSpec((N_EXPERTS, HEAD), lambda i: (0, 0))],
        out_shape=[jax.ShapeDtypeStruct((8, n), jnp.int32), jax.ShapeDtypeStruct((8, n), jnp.int32),
                   jax.ShapeDtypeStruct((8, n), F32), jax.ShapeDtypeStruct((N_EXPERTS, HEAD), F32)],
        scratch_shapes=[pltpu.VMEM((N_EXPERTS, HEAD), F32)],
        compiler_params=_params("arbitrary"),
        name="route_topk",
    )(scores_t, bias_b)


def _expert_kernel(be_ref, s_m3, s_m2, s_m1, s_cur, s_p1, s_p2, x_hbm, wg_ref, wu_ref, wd_ref,
                   out_hbm, xbuf, ybuf, sem_in, sem_out, *, n_tok):
    i = pl.program_id(0)
    last = pl.num_programs(0) - 1
    slot = i % EXPERT_BUFS
    nxt = (i + 2) % EXPERT_BUFS
    prv = nxt

    def buf_row(k):
        return k * ROW_PITCH if isinstance(k, int) else pl.multiple_of(k * ROW_PITCH, 8)

    def in_copy(slots_ref, buf, k):
        src = pl.multiple_of((slots_ref[0, 0, k] >> 3) * ROW_TILES, ROW_TILES)
        return pltpu.make_async_copy(x_hbm.at[pl.ds(src, ROW_TILES)],
                                     xbuf.at[buf, pl.ds(buf_row(k), ROW_TILES)], sem_in.at[buf, k])

    def out_copy(slots_ref, buf, k):
        v = slots_ref[0, 0, k]
        dst = pl.multiple_of(((v & 7) * n_tok + (v >> 3)) * ROW_TILES, ROW_TILES)
        return pltpu.make_async_copy(ybuf.at[buf, pl.ds(buf_row(k), ROW_TILES)],
                                     out_hbm.at[pl.ds(dst, ROW_TILES)], sem_out.at[buf, k])

    def rolled(fn):
        def body(k, carry):
            fn(k)
            return carry
        lax.fori_loop(0, MOE_BLOCK, body, 0, unroll=8)

    @pl.when(i == 0)
    def _():
        rolled(lambda k: in_copy(s_cur, 0, k).start())
        rolled(lambda k: in_copy(s_p1, 1, k).start())

    @pl.when(i < EXPERT_BUFS)
    def _():
        rolled(lambda k: in_copy(s_cur, slot, k).wait())

    @pl.when(i >= EXPERT_BUFS)
    def _():
        def both(k):
            in_copy(s_cur, slot, k).wait()
            out_copy(s_m3, slot, k).wait()
        rolled(both)

    def block(gather_ahead, scatter_prev):
        col = lambda ref, s_: ref.at[slot][pl.ds(s_, MOE_BLOCK, stride=ROW_PITCH), :]
        xb = jnp.concatenate([col(xbuf, s_) for s_ in range(ROW_TILES)], axis=1).astype(BF16)
        for k in range(MOE_BLOCK):
            if gather_ahead:
                in_copy(s_p2, nxt, k).start()
            if scatter_prev:
                out_copy(s_m1, prv, k).start()
        hg = jnp.dot(xb, wg_ref[0], preferred_element_type=F32)
        hu = jnp.dot(xb, wu_ref[0], preferred_element_type=F32)
        hb = (_silu(hg) * hu).astype(BF16)
        y = jnp.dot(hb, wd_ref[0], preferred_element_type=F32)
        for s_ in range(ROW_TILES):
            ybuf.at[slot][pl.ds(s_, MOE_BLOCK, stride=ROW_PITCH), :] = y[:, s_ * HEAD:(s_ + 1) * HEAD]

    @pl.when(i == 0)
    def _():
        block(True, False)

    @pl.when((i > 0) & (i < last - 1))
    def _():
        block(True, True)

    @pl.when(i == last - 1)
    def _():
        block(False, True)

    @pl.when(i == last)
    def _():
        block(False, True)
        rolled(lambda k: out_copy(s_m2, (i + 1) % EXPERT_BUFS, k).wait())
        rolled(lambda k: out_copy(s_cur, slot, k).start())
        rolled(lambda k: out_copy(s_m1, prv, k).wait())
        rolled(lambda k: out_copy(s_cur, slot, k).wait())


def _experts(x1r, block_e, slots, wg, wu, wd):
    n = x1r.shape[0] // ROW_TILES
    n_blocks = block_e.shape[0]
    slot_spec = lambda off: pl.BlockSpec(
        (1, 1, MOE_BLOCK), lambda i, be: (jnp.clip(i + off, 0, n_blocks - 1), 0, 0), memory_space=pltpu.SMEM)
    buf = pltpu.VMEM((EXPERT_BUFS, MOE_BLOCK * ROW_PITCH, HEAD), F32)
    sem = pltpu.SemaphoreType.DMA((EXPERT_BUFS, MOE_BLOCK))
    return pl.pallas_call(
        functools.partial(_expert_kernel, n_tok=n),
        grid_spec=pltpu.PrefetchScalarGridSpec(
            num_scalar_prefetch=1,
            grid=(n_blocks,),
            in_specs=[slot_spec(off) for off in (-3, -2, -1, 0, 1, 2)] + [
                pl.BlockSpec(memory_space=pl.ANY),
                pl.BlockSpec((1, D_MODEL, EXPERT_FF), lambda i, be: (be[i], 0, 0)),
                pl.BlockSpec((1, D_MODEL, EXPERT_FF), lambda i, be: (be[i], 0, 0)),
                pl.BlockSpec((1, EXPERT_FF, D_MODEL), lambda i, be: (be[i], 0, 0))],
            out_specs=pl.BlockSpec(memory_space=pl.ANY),
            scratch_shapes=[buf, buf, sem, sem]),
        out_shape=jax.ShapeDtypeStruct(((TOP_K * n + 2 * MOE_BLOCK) * ROW_TILES, HEAD), F32),
        compiler_params=_params("arbitrary"),
        name="routed_experts",
    )(block_e, *([slots] * 6), x1r, wg, wu, wd)


def _slot_tables(e6, pos6, counts):
    n = e6.shape[1]
    n_assign = n * TOP_K
    n_blocks = -(-n_assign // MOE_BLOCK) + N_EXPERTS
    padded = (counts + MOE_BLOCK - 1) // MOE_BLOCK * MOE_BLOCK
    pad_end = jnp.cumsum(padded)
    pad_start = pad_end - padded
    onehot = e6[:TOP_K, :, None] == jnp.arange(N_EXPERTS, dtype=jnp.int32)
    dest = jnp.sum(jnp.where(onehot, pad_start, 0), axis=-1) + pos6[:TOP_K]
    packed = (jnp.arange(n, dtype=jnp.int32)[None, :] * 8
              + jnp.arange(TOP_K, dtype=jnp.int32)[:, None])
    pad_slots = (jnp.arange(n_blocks * MOE_BLOCK, dtype=jnp.int32) % (2 * MOE_BLOCK)) * 8 + TOP_K
    slots = pad_slots.at[dest.reshape(-1)].set(packed.reshape(-1), unique_indices=True)
    blk_start = jnp.arange(n_blocks, dtype=jnp.int32) * MOE_BLOCK
    block_e = jnp.minimum(jnp.sum((pad_end[None, :] <= blk_start[:, None]).astype(jnp.int32), axis=1),
                          N_EXPERTS - 1)
    return block_e, slots.reshape(n_blocks, 1, MOE_BLOCK)


def _combine_kernel(*refs):
    r_refs = refs[:TOP_K]
    w6_ref, x_ref, xb_ref, wg_ref, wu_ref, wd_ref, g_ref, b_ref, xo_ref, xob_ref = refs[TOP_K:]
    xb = xb_ref[...]
    hb = (_silu(jnp.dot(xb, wg_ref[...], preferred_element_type=F32))
          * jnp.dot(xb, wu_ref[...], preferred_element_type=F32)).astype(BF16)
    shared = jnp.dot(hb, wd_ref[...], preferred_element_type=F32)
    w6 = w6_ref[...]
    tm = w6.shape[0]
    rows2d = lambda ref: jnp.concatenate(
        [ref[pl.ds(s_, tm, stride=ROW_TILES), :] for s_ in range(ROW_TILES)], axis=1)
    routed = rows2d(r_refs[0]) * w6[:, 0:1]
    for j in range(1, TOP_K):
        routed = routed + rows2d(r_refs[j]) * w6[:, j:j + 1]
    x2 = _layer_norm(DN_ALPHA * x_ref[...] + (routed + shared), g_ref[...], b_ref[...])
    xo_ref[...] = x2
    xob_ref[...] = x2.astype(BF16)


def _combine(routed, w6, x1, x1b, wg, wu, wd, g, b, tm):
    n = x1.shape[0]
    nt = n // tm
    row = pl.BlockSpec((tm, D_MODEL), lambda i: (i, 0))
    vec = pl.BlockSpec((1, D_MODEL), lambda i: (0, 0))
    w_in = pl.BlockSpec((D_MODEL, EXPERT_FF), lambda i: (0, 0))
    r_specs = [pl.BlockSpec((tm * ROW_TILES, HEAD), functools.partial(lambda i, j: (j * nt + i, 0), j=j))
               for j in range(TOP_K)]
    return pl.pallas_call(
        _combine_kernel,
        grid=(nt,),
        in_specs=r_specs + [pl.BlockSpec((tm, 8), lambda i: (i, 0)), row, row, w_in, w_in,
                            pl.BlockSpec((EXPERT_FF, D_MODEL), lambda i: (0, 0)), vec, vec],
        out_specs=[row, row],
        out_shape=[jax.ShapeDtypeStruct((n, D_MODEL), F32), jax.ShapeDtypeStruct((n, D_MODEL), BF16)],
        compiler_params=_params("parallel"),
        name="combine_shared_ln",
    )(*([routed] * TOP_K), w6, x1, x1b, wg, wu, wd, g, b)


def _ple_kernel(x_ref, xb_ref, p_ref, wg_ref, wp_ref, xo_ref, xob_ref):
    gate = _sigmoid(jnp.dot(xb_ref[...], wg_ref[...], preferred_element_type=F32))
    pe = jnp.dot(p_ref[...].astype(BF16), wp_ref[...], preferred_element_type=F32)
    x3 = x_ref[...] + pe * gate
    xo_ref[...] = x3
    xob_ref[...] = x3.astype(BF16)


def _ple(x2, x2b, p, wg, wp, tm):
    n = x2.shape[0]
    row = pl.BlockSpec((tm, D_MODEL), lambda i: (i, 0))
    return pl.pallas_call(
        _ple_kernel,
        grid=(n // tm,),
        in_specs=[row, row, pl.BlockSpec((tm, PLE_DIM), lambda i: (i, 0)),
                  pl.BlockSpec((D_MODEL, D_MODEL), lambda i: (0, 0)),
                  pl.BlockSpec((PLE_DIM, D_MODEL), lambda i: (0, 0))],
        out_specs=[row, row],
        out_shape=[jax.ShapeDtypeStruct((n, D_MODEL), F32), jax.ShapeDtypeStruct((n, D_MODEL), BF16)],
        compiler_params=_params("parallel"),
        name="ple",
    )(x2, x2b, p, wg, wp)


def _prep_weights(w_in, hgrn_lb_logits, gdn_a_log, gdn_dt_bias, w_branch, w_out, w_router,
                  w_exp_gate, w_exp_up, w_exp_down, w_sh_gate, w_sh_up, w_sh_down, w_ple_gate, w_ple_proj):
    pts = np.cumsum([1024, 1024, 1024, 1024, 512, 512, 1024, 1024, 8, 8, 1024, 1024, 1024, 1024]).tolist()
    a_end, b_end, c_end = pts[3], pts[9], pts[13]
    w_acg = jnp.concatenate([w_in[:, :, :a_end], w_in[:, :, b_end:c_end], w_in[:, :, c_end:]], axis=-1)
    w_b = jnp.concatenate([w_in[:, :, a_end:b_end],
                           jnp.zeros((DEPTH, D_MODEL, B_COLS - (b_end - a_end)), w_in.dtype)], axis=-1)
    lb_p = jax.nn.softmax(hgrn_lb_logits.astype(F32), axis=0)
    lb = jnp.maximum(jnp.cumsum(lb_p, axis=0) - lb_p[:1], 0.0)
    pad_row = lambda v: jnp.zeros((DEPTH, 1, HEAD), F32).at[:, 0, N_HEADS:2 * N_HEADS].set(v.astype(F32))
    wr = jnp.swapaxes(w_router.astype(F32), -1, -2)
    wr_hi = wr.astype(BF16)
    return dict(
        w_acg=w_acg.astype(BF16), w_b=w_b.astype(BF16),
        llb=jnp.log(lb)[:, None, :], l1m=jnp.log1p(-lb)[:, None, :], oml=(1.0 - lb)[:, None, :],
        alog_row=pad_row(gdn_a_log), dtb_row=pad_row(gdn_dt_bias),
        w_branch=w_branch.astype(BF16), w_out=w_out.astype(BF16),
        wr_hi=wr_hi, wr_lo=(wr - wr_hi.astype(F32)).astype(BF16),
        wg=w_exp_gate.astype(BF16), wu=w_exp_up.astype(BF16), wd=w_exp_down.astype(BF16),
        sg=w_sh_gate.astype(BF16), su=w_sh_up.astype(BF16), sd=w_sh_down.astype(BF16),
        pg=w_ple_gate.astype(BF16), pp=w_ple_proj.astype(BF16))


def _run_trunk(x, p, pos0, states, pw, small):
    (hgrn_norm, gdn_conv, gdn_norm, ret_norm, ln1_g, ln1_b, ln2_g, ln2_b, router_bias) = small
    s_hgrn, s_gdn, s_conv, s_ret = states
    b, t, d = x.shape
    n = b * t
    tm = min(n, 512)
    xf = x.reshape(n, d)
    xb = xf.astype(BF16)
    half = HEAD // 2
    inv = ROPE_BASE ** (-jnp.arange(half, dtype=F32) / half)
    ang = (pos0 + jnp.arange(t, dtype=F32))[:, None] * inv[None, :]
    cos_t = jnp.concatenate([jnp.cos(ang), jnp.cos(ang)], axis=-1)
    sin_t = jnp.concatenate([-jnp.sin(ang), jnp.sin(ang)], axis=-1)
    lg = jnp.log1p(-jnp.exp2(-5.0 - jnp.arange(N_HEADS, dtype=F32)))
    row = lambda v: v.reshape(1, -1).astype(F32)
    new = ([], [], [], [])
    for i in range(DEPTH):
        acg = _matmul(xb, pw['w_acg'][i], min(n, 2048), 1024)
        yb = _matmul(xb, pw['w_b'][i], min(n, 2048), 640)
        o_a, st_a = _hgrn(acg, pw['llb'][i], pw['l1m'][i], pw['oml'][i], row(hgrn_norm[i]),
                          jnp.swapaxes(s_hgrn[i], -1, -2), b, t)
        buf8 = jnp.concatenate([jnp.zeros((b, 8 - (CONV_W - 1), B_CONV_CH), F32), s_conv[i]], axis=1)
        qkvc = _gdn_conv(yb, gdn_conv[i], buf8, b, t)
        o_b, st_b = _gdn(qkvc, yb, pw['alog_row'][i], pw['dtb_row'][i], row(gdn_norm[i]), s_gdn[i], b, t)
        o_c, st_c = _retention(acg, lg, cos_t, sin_t, row(ret_norm[i]), s_ret[i], b, t)
        merged = _merge(o_a, o_b, o_c, pw['w_branch'][i], acg, min(n, 1024))
        x1, x1r, x1b, scores = _outproj(merged, xf, pw['w_out'][i], row(ln1_g[i]), row(ln1_b[i]),
                                        pw['wr_hi'][i], pw['wr_lo'][i], tm)
        bias_b = jnp.broadcast_to(router_bias[i].astype(F32)[:, None], (N_EXPERTS, tm))
        e6, pos6, w6, cnt = _route(scores, bias_b, tm)
        block_e, slots = _slot_tables(e6, pos6, cnt[:, 0].astype(jnp.int32))
        routed = _experts(x1r, block_e, slots, pw['wg'][i], pw['wu'][i], pw['wd'][i])
        x2, x2b = _combine(routed, w6.T, x1, x1b, pw['sg'][i], pw['su'][i],
                           pw['sd'][i], row(ln2_g[i]), row(ln2_b[i]), min(n, 256))
        xf, xb = _ple(x2, x2b, p[i].reshape(n, PLE_DIM), pw['pg'][i], pw['pp'][i], tm)
        new[0].append(jnp.swapaxes(st_a, -1, -2))
        new[1].append(st_b)
        new[2].append(yb.reshape(b, t, B_COLS)[:, t - (CONV_W - 1):, :B_CONV_CH])
        new[3].append(st_c)
    return xf.reshape(b, t, d), tuple(jnp.stack(a) for a in new)


def kernel(x_prompt, x_sample, p_prompt, p_sample, state_hgrn, state_gdn, state_gdn_conv, state_ret,
           w_in, hgrn_lb_logits, hgrn_norm, gdn_conv, gdn_a_log, gdn_dt_bias, gdn_norm, ret_norm,
           w_branch, w_out, ln1_g, ln1_b, ln2_g, ln2_b, w_router, router_bias,
           w_exp_gate, w_exp_up, w_exp_down, w_sh_gate, w_sh_up, w_sh_down, w_ple_gate, w_ple_proj):
    pw = _prep_weights(w_in, hgrn_lb_logits, gdn_a_log, gdn_dt_bias, w_branch, w_out, w_router,
                       w_exp_gate, w_exp_up, w_exp_down, w_sh_gate, w_sh_up, w_sh_down,
                       w_ple_gate, w_ple_proj)
    small = (hgrn_norm, gdn_conv, gdn_norm, ret_norm, ln1_g, ln1_b, ln2_g, ln2_b, router_bias)
    bp = x_prompt.shape[0]
    zero_states = (jnp.zeros((DEPTH, bp, N_HEADS, HEAD, HEAD), F32),
                   jnp.zeros((DEPTH, bp, N_HEADS, HEAD, HEAD), F32),
                   jnp.zeros((DEPTH, bp, CONV_W - 1, B_CONV_CH), F32),
                   jnp.zeros((DEPTH, bp, N_HEADS, HEAD, HEAD), F32))
    y_p, (hp, gp, cp, rp) = _run_trunk(x_prompt, p_prompt, 0.0, zero_states, pw, small)
    y_s, (hs, gs, cs, rs) = _run_trunk(x_sample, p_sample, float(PAST_LEN),
                                       (state_hgrn, state_gdn, state_gdn_conv, state_ret), pw, small)
    return (y_p, y_s, hp, gp, cp, rp, hs, gs, cs, rs)
```

```python
import functools
import math

import jax
import jax.numpy as jnp
import numpy as np
from jax import lax
from jax.experimental import pallas as pl
from jax.experimental.pallas import tpu as pltpu

F32 = jnp.float32
BF16 = jnp.bfloat16

D_MODEL = 2048
DEPTH = 4
PAST_LEN = 4096
HEAD = 128
N_HEADS = 8
B_QK_HEADS = 4
CONV_W = 4
BRANCH_W = N_HEADS * HEAD
B_CONV_CH = 2 * B_QK_HEADS * HEAD + N_HEADS * HEAD
ROPE_BASE = 10000.0
N_BRANCH = 3
N_EXPERTS = 64
N_GROUPS = 8
TOPK_GROUPS = 4
TOP_K = 6
EXPERT_FF = 512
ROUTE_SCALE = 2.5
MOE_BLOCK = 256
ROW_TILES = D_MODEL // HEAD
EXPERT_BUFS = 3
ROW_PITCH = 24
PLE_DIM = 256
DN_ALPHA = (2.0 * DEPTH) ** 0.25
LN_EPS = 1e-5
NORM_EPS = 1e-6

HGRN_SUB = 16
HGRN_HPB = 8
GDN_CHUNK = 64
GDN_CHUNKS_PER_STEP = 4
GDN_APPLY_PASSES = (3, 1, 1, 1, 1, 1)
GDN_SQUARE_PASSES = (3, 1, 1, 1, 1)
ACG_COLS = 4 * BRANCH_W + 4 * BRANCH_W + N_BRANCH * D_MODEL
GATE_COL0 = 8 * BRANCH_W
B_COLS = 3200
B_SMALL_COL = 3072

VMEM_LIMIT = 56 * 1024 * 1024


def _params(*sem):
    return pltpu.CompilerParams(dimension_semantics=sem, vmem_limit_bytes=VMEM_LIMIT)


def _bdot(a, b):
    return jnp.dot(a.astype(BF16), b.astype(BF16), preferred_element_type=F32)


def _bdot_nt(a, b):
    return lax.dot_general(a.astype(BF16), b.astype(BF16), (((1,), (1,)), ((), ())),
                           preferred_element_type=F32)


def _bdot_tn(a, b):
    return lax.dot_general(a.astype(BF16), b.astype(BF16), (((0,), (0,)), ((), ())),
                           preferred_element_type=F32)


def _split2(x):
    hi = x.astype(BF16)
    lo = (x - hi.astype(F32)).astype(BF16)
    return hi, lo


def _split3(x):
    hi = x.astype(BF16)
    r = x - hi.astype(F32)
    mid = r.astype(BF16)
    lo = (r - mid.astype(F32)).astype(BF16)
    return hi, mid, lo


def _mask_dot(mask_bf, x):
    hi, mid, lo = _split3(x)
    d = functools.partial(jnp.dot, preferred_element_type=F32)
    return d(mask_bf, hi) + d(mask_bf, mid) + d(mask_bf, lo)


def _dot_x3(a, b):
    ah, al = _split2(a)
    bh, bl = _split2(b)
    d = functools.partial(jnp.dot, preferred_element_type=F32)
    return d(ah, bh) + d(ah, bl) + d(al, bh)


def _dot_p(a, b, passes):
    return _dot_x3(a, b) if passes == 3 else _bdot(a, b)


def _sigmoid(x):
    return 1.0 / (1.0 + jnp.exp(-x))


def _silu(x):
    return x * _sigmoid(x)


def _softplus(x):
    return jnp.maximum(x, 0.0) + jnp.log1p(jnp.exp(-jnp.abs(x)))


def _log_sigmoid(x):
    return jnp.minimum(x, 0.0) - jnp.log1p(jnp.exp(-jnp.abs(x)))


def _mm_kernel(x_ref, w_ref, o_ref):
    o_ref[...] = jnp.dot(x_ref[...], w_ref[...], preferred_element_type=F32).astype(o_ref.dtype)


def _matmul(x, w, tm, tn):
    m, k = x.shape
    n = w.shape[1]
    return pl.pallas_call(
        _mm_kernel,
        grid=(m // tm, n // tn),
        in_specs=[pl.BlockSpec((tm, k), lambda i, j: (i, 0)),
                  pl.BlockSpec((k, tn), lambda i, j: (0, j))],
        out_specs=pl.BlockSpec((tm, tn), lambda i, j: (i, j)),
        out_shape=jax.ShapeDtypeStruct((m, n), F32),
        compiler_params=_params("parallel", "parallel"),
        name="inproj_matmul",
    )(x, w)


def _hgrn_kernel(q_ref, f_ref, i_ref, g_ref, llb_ref, l1m_ref, oml_ref, nw_ref, s0_ref,
                 o_ref, sout_ref, st_ref, *, tc):
    c = pl.program_id(2)

    @pl.when(c == 0)
    def _():
        st_ref[...] = s0_ref[0]

    nb = tc // HGRN_SUB
    heads = range(HGRN_HPB)
    lanes = [slice(h * HEAD, (h + 1) * HEAD) for h in heads]
    shp = (nb, HGRN_SUB, HEAD)
    z = [f_ref[:, lanes[h]] for h in heads]
    bterm = [l1m_ref[:, lanes[h]] + _log_sigmoid(z[h]) for h in heads]
    log_f = [jnp.maximum(llb_ref[:, lanes[h]], bterm[h])
             + jnp.log1p(jnp.exp(-jnp.abs(llb_ref[:, lanes[h]] - bterm[h]))) for h in heads]
    key3 = [(oml_ref[:, lanes[h]] * (1.0 / (1.0 + jnp.exp(z[h])))).reshape(shp) for h in heads]
    r = lax.broadcasted_iota(jnp.int32, (tc, tc), 0)
    s = lax.broadcasted_iota(jnp.int32, (tc, tc), 1)
    seg_mask = jnp.where((r // HGRN_SUB == s // HGRN_SUB) & (s <= r), 1.0, 0.0).astype(BF16)
    cum3 = [_mask_dot(seg_mask, log_f[h]).reshape(shp) for h in heads]
    q3 = [q_ref[:, lanes[h]].reshape(shp) for h in heads]
    v3 = [i_ref[:, lanes[h]].reshape(shp) for h in heads]
    rows = lax.broadcasted_iota(jnp.int32, shp, 1)
    ones_bf = jnp.ones((HEAD, HEAD), BF16)
    att = []
    for h in heads:
        parts = []
        for si in range(HGRN_SUB):
            arg = jnp.where(rows >= si, cum3[h] - cum3[h][:, si:si + 1, :], -1e30)
            parts.append((q3[h] * (key3[h][:, si:si + 1, :] * jnp.exp(arg))).reshape(tc, HEAD))
        p_hi, p_lo = _split2(jnp.concatenate(parts, axis=0))
        att.append(jnp.dot(p_hi, ones_bf, preferred_element_type=F32)
                   + jnp.dot(p_lo, ones_bf, preferred_element_type=F32))
    o = []
    for h in heads:
        acc = jnp.zeros((tc, HEAD), F32)
        for si in range(HGRN_SUB):
            vs = jnp.broadcast_to(v3[h][:, si:si + 1, :], shp).reshape(tc, HEAD)
            acc = acc + att[h][si * tc:(si + 1) * tc, :] * vs
        o.append(acc)
    last3 = [cum3[h][:, HGRN_SUB - 1:HGRN_SUB, :] for h in heads]
    qtil = [q3[h] * jnp.exp(cum3[h]) for h in heads]
    ktil = [key3[h] * jnp.exp(last3[h] - cum3[h]) for h in heads]
    elast = [jnp.exp(last3[h]) for h in heads]
    kv = [[_bdot_tn(v3[h][j], ktil[h][j]) for j in range(nb)] for h in heads]
    st = [st_ref[h] for h in heads]
    outs = [[] for _ in heads]
    for j in range(nb):
        for h in heads:
            outs[h].append(o[h][j * HGRN_SUB:(j + 1) * HGRN_SUB, :] + _bdot_nt(qtil[h][j], st[h]))
            st[h] = elast[h][j] * st[h] + kv[h][j]
    for h in heads:
        st_ref[h] = st[h]
        oh = jnp.concatenate(outs[h], axis=0)
        oh = oh * lax.rsqrt(jnp.mean(oh * oh, axis=-1, keepdims=True) + NORM_EPS) * nw_ref[...]
        o_ref[:, lanes[h]] = (oh * _silu(g_ref[:, lanes[h]])).astype(o_ref.dtype)

    @pl.when(c == pl.num_programs(2) - 1)
    def _():
        sout_ref[0] = st_ref[...]


def _hgrn(acg, llb, l1m, oml, nw, s0t, b, t):
    tc = min(t, 128)
    nt = t // tc
    hw = HGRN_HPB * HEAD
    ng = N_HEADS // HGRN_HPB
    col_spec = lambda part: pl.BlockSpec((tc, hw), lambda bi, hg, c: (bi * nt + c, part * ng + hg))
    vec_spec = pl.BlockSpec((1, hw), lambda bi, hg, c: (0, hg))
    st_spec = pl.BlockSpec((1, HGRN_HPB, HEAD, HEAD), lambda bi, hg, c: (bi, hg, 0, 0))
    return pl.pallas_call(
        functools.partial(_hgrn_kernel, tc=tc),
        grid=(b, ng, nt),
        in_specs=[col_spec(0), col_spec(1), col_spec(2), col_spec(3),
                  vec_spec, vec_spec, vec_spec, pl.BlockSpec((1, HEAD), lambda bi, hg, c: (0, 0)), st_spec],
        out_specs=[pl.BlockSpec((tc, hw), lambda bi, hg, c: (bi * nt + c, hg)), st_spec],
        out_shape=[jax.ShapeDtypeStruct((b * t, BRANCH_W), BF16),
                   jax.ShapeDtypeStruct((b, N_HEADS, HEAD, HEAD), F32)],
        scratch_shapes=[pltpu.VMEM((HGRN_HPB, HEAD, HEAD), F32)],
        compiler_params=_params("parallel", "parallel", "arbitrary"),
        name="hgrn2",
    )(acg, acg, acg, acg, llb, l1m, oml, nw, s0t)


def _conv_kernel(u_ref, w_ref, buf_ref, o_ref, tail_ref, *, tc):
    c = pl.program_id(1)

    @pl.when(c == 0)
    def _():
        tail_ref[...] = buf_ref[0]

    u = u_ref[...]
    ext = jnp.concatenate([tail_ref[...], u], axis=0)
    acc = u * w_ref[CONV_W - 1:CONV_W, :]
    for j in range(CONV_W - 1):
        acc = acc + ext[5 + j:5 + j + tc, :] * w_ref[j:j + 1, :]
    tail_ref[...] = u[tc - 8:, :]
    y = _silu(acc)
    nqk = B_QK_HEADS * HEAD
    for hh in range(2 * B_QK_HEADS):
        blk = y[:, hh * HEAD:(hh + 1) * HEAD]
        nrm = blk * lax.rsqrt(jnp.sum(blk * blk, axis=-1, keepdims=True) + NORM_EPS)
        if hh < B_QK_HEADS:
            nrm = nrm * (HEAD ** -0.5)
        o_ref[:, hh * HEAD:(hh + 1) * HEAD] = nrm
    o_ref[:, 2 * nqk:] = y[:, 2 * nqk:]


def _gdn_conv(yb, w, buf8, b, t):
    tc = min(t, 256)
    nt = t // tc
    return pl.pallas_call(
        functools.partial(_conv_kernel, tc=tc),
        grid=(b, nt),
        in_specs=[pl.BlockSpec((tc, B_CONV_CH), lambda bi, c: (bi * nt + c, 0)),
                  pl.BlockSpec((CONV_W, B_CONV_CH), lambda bi, c: (0, 0)),
                  pl.BlockSpec((1, 8, B_CONV_CH), lambda bi, c: (bi, 0, 0))],
        out_specs=pl.BlockSpec((tc, B_CONV_CH), lambda bi, c: (bi * nt + c, 0)),
        out_shape=jax.ShapeDtypeStruct((b * t, B_CONV_CH), F32),
        scratch_shapes=[pltpu.VMEM((8, B_CONV_CH), F32)],
        compiler_params=_params("parallel", "arbitrary"),
        name="gdn_conv",
    )(yb, w, buf8)


def _gdn_kernel(qkv_ref, z_ref, sm_ref, alog_ref, dtb_ref, nw_ref, s0_ref, o_ref, sout_ref, st_ref, *, nc):
    c = pl.program_id(1)
    cs = GDN_CHUNK

    @pl.when(c == 0)
    def _():
        st_ref[...] = s0_ref[0]

    r = lax.broadcasted_iota(jnp.int32, (cs, cs), 0)
    s = lax.broadcasted_iota(jnp.int32, (cs, cs), 1)
    lower = r >= s
    strict = r > s
    tril_bf = jnp.where(lower, 1.0, 0.0).astype(BF16)
    triu_bf = jnp.where(r <= s, 1.0, 0.0).astype(BF16)
    above_bf = jnp.where(r < s, 1.0, 0.0).astype(BF16)
    nw = nw_ref[...]
    d_tn = lambda a: lax.dot_general(a, triu_bf, (((0,), (0,)), ((), ())), preferred_element_type=F32)
    rep = N_HEADS // B_QK_HEADS
    heads = range(N_HEADS)
    chunks = range(nc)
    q, k, qk, cum, rem, dec, x, y = ([] for _ in range(8))
    for ci in chunks:
        rs = slice(ci * cs, (ci + 1) * cs)
        sm = sm_ref[rs, :]
        beta_all = _sigmoid(sm)
        g_all = -jnp.exp(alog_ref[...]) * _softplus(sm + dtb_ref[...])
        cum_all = _mask_dot(tril_bf, g_all)
        rem_all = _mask_dot(above_bf, g_all)
        g_hi, g_mid, g_lo = _split3(g_all)
        cum_rows = d_tn(g_hi) + d_tn(g_mid) + d_tn(g_lo)
        q.append([qkv_ref[rs, qh * HEAD:(qh + 1) * HEAD] for qh in range(B_QK_HEADS)])
        k.append([qkv_ref[rs, (B_QK_HEADS + qh) * HEAD:(B_QK_HEADS + qh + 1) * HEAD]
                  for qh in range(B_QK_HEADS)])
        kk = [_bdot_nt(k[ci][qh], k[ci][qh]) for qh in range(B_QK_HEADS)]
        qk.append([_bdot_nt(q[ci][qh], k[ci][qh]) for qh in range(B_QK_HEADS)])
        cum.append([]), rem.append([]), dec.append([]), x.append([]), y.append([])
        for h in heads:
            col = N_HEADS + h
            v = qkv_ref[rs, (2 * B_QK_HEADS + h) * HEAD:(2 * B_QK_HEADS + h + 1) * HEAD]
            beta = beta_all[:, h:h + 1]
            cum[ci].append(jnp.broadcast_to(cum_all[:, col:col + 1], (cs, HEAD)))
            rem[ci].append(jnp.broadcast_to(rem_all[:, col:col + 1], (cs, HEAD)))
            seg = cum_all[:, col:col + 1] - cum_rows[col:col + 1, :]
            dec[ci].append(jnp.where(lower, jnp.exp(jnp.where(lower, seg, 0.0)), 0.0))
            x[ci].append(jnp.where(strict, -(kk[h // rep] * beta) * dec[ci][h], 0.0))
            y[ci].append(jnp.concatenate([v * beta, k[ci][h // rep] * beta * jnp.exp(cum[ci][h])], axis=1))
    for lvl in range(len(GDN_APPLY_PASSES)):
        y = [[y[ci][h] + _dot_p(x[ci][h], y[ci][h], GDN_APPLY_PASSES[lvl]) for h in heads] for ci in chunks]
        if lvl < len(GDN_SQUARE_PASSES):
            x = [[_dot_p(x[ci][h], x[ci][h], GDN_SQUARE_PASSES[lvl]) for h in heads] for ci in chunks]
    st = [st_ref[h] for h in heads]
    for ci in chunks:
        rs = slice(ci * cs, (ci + 1) * cs)
        u = [y[ci][h][:, :HEAD] - _bdot(y[ci][h][:, HEAD:], st[h]) for h in heads]
        o = [_bdot(q[ci][h // rep] * jnp.exp(cum[ci][h]), st[h]) + _bdot(qk[ci][h // rep] * dec[ci][h], u[h])
             for h in heads]
        st = [jnp.exp(cum[ci][h][cs - 1:cs, :]) * st[h] + _bdot_tn(k[ci][h // rep] * jnp.exp(rem[ci][h]), u[h])
              for h in heads]
        for h in heads:
            on = o[h] * lax.rsqrt(jnp.mean(o[h] * o[h], axis=-1, keepdims=True) + NORM_EPS) * nw
            zg = z_ref[rs, h * HEAD:(h + 1) * HEAD]
            o_ref[rs, h * HEAD:(h + 1) * HEAD] = (on * _silu(zg)).astype(o_ref.dtype)
    for h in heads:
        st_ref[h] = st[h]

    @pl.when(c == pl.num_programs(1) - 1)
    def _():
        sout_ref[0] = st_ref[...]


def _gdn(qkvc, yb, alog_row, dtb_row, nw, s0, b, t):
    nc = min(t // GDN_CHUNK, GDN_CHUNKS_PER_STEP)
    ts = nc * GDN_CHUNK
    nt = t // ts
    st_spec = pl.BlockSpec((1, N_HEADS, HEAD, HEAD), lambda bi, c: (bi, 0, 0, 0))
    row_spec = pl.BlockSpec((1, HEAD), lambda bi, c: (0, 0))
    return pl.pallas_call(
        functools.partial(_gdn_kernel, nc=nc),
        grid=(b, nt),
        in_specs=[pl.BlockSpec((ts, B_CONV_CH), lambda bi, c: (bi * nt + c, 0)),
                  pl.BlockSpec((ts, BRANCH_W), lambda bi, c: (bi * nt + c, B_CONV_CH // BRANCH_W)),
                  pl.BlockSpec((ts, HEAD), lambda bi, c: (bi * nt + c, B_SMALL_COL // HEAD)),
                  row_spec, row_spec, row_spec, st_spec],
        out_specs=[pl.BlockSpec((ts, BRANCH_W), lambda bi, c: (bi * nt + c, 0)), st_spec],
        out_shape=[jax.ShapeDtypeStruct((b * t, BRANCH_W), BF16),
                   jax.ShapeDtypeStruct((b, N_HEADS, HEAD, HEAD), F32)],
        scratch_shapes=[pltpu.VMEM((N_HEADS, HEAD, HEAD), F32)],
        compiler_params=_params("parallel", "arbitrary"),
        name="gated_delta",
    )(qkvc, yb, yb, alog_row, dtb_row, nw, s0)


def _ret_kernel(lg_ref, q_ref, k_ref, v_ref, g_ref, cos_ref, sin_ref, nw_ref, s0_ref,
                o_ref, sout_ref, st_ref, *, cs):
    c = pl.program_id(1)

    @pl.when(c == 0)
    def _():
        st_ref[...] = s0_ref[0]

    heads = range(N_HEADS)
    lanes = [slice(h * HEAD, (h + 1) * HEAD) for h in heads]
    cos = cos_ref[...]
    sin = sin_ref[...]

    def rot(x):
        return x * cos + pltpu.roll(x, HEAD // 2, 1) * sin

    idx = lax.broadcasted_iota(jnp.int32, (cs, HEAD), 0).astype(F32)
    rel = (lax.broadcasted_iota(jnp.int32, (cs, cs), 0)
           - lax.broadcasted_iota(jnp.int32, (cs, cs), 1))
    relf = jnp.maximum(rel, 0).astype(F32)
    lg = [lg_ref[h] for h in heads]
    q = [rot(q_ref[:, lanes[h]]) for h in heads]
    k = [rot(k_ref[:, lanes[h]]) * (HEAD ** -0.5) for h in heads]
    att = [_bdot_nt(q[h], k[h]) * jnp.where(rel >= 0, jnp.exp(relf * lg[h]), 0.0) for h in heads]
    st = [st_ref[h] for h in heads]
    o = [_bdot(q[h], st[h]) * jnp.exp((idx + 1.0) * lg[h]) + _bdot(att[h], v_ref[:, lanes[h]]) for h in heads]
    for h in heads:
        c_dec = jnp.exp(jnp.zeros((1, HEAD), F32) + cs * lg[h])
        st_ref[h] = c_dec * st[h] + _bdot_tn(k[h] * jnp.exp((cs - 1.0 - idx) * lg[h]), v_ref[:, lanes[h]])
    for h in heads:
        mu = jnp.mean(o[h], axis=-1, keepdims=True)
        var = jnp.mean(jnp.square(o[h] - mu), axis=-1, keepdims=True)
        on = (o[h] - mu) * lax.rsqrt(var + NORM_EPS) * nw_ref[...]
        o_ref[:, lanes[h]] = (on * _silu(g_ref[:, lanes[h]])).astype(o_ref.dtype)

    @pl.when(c == pl.num_programs(1) - 1)
    def _():
        sout_ref[0] = st_ref[...]


def _retention(acg, lg, cos, sin, nw, s0, b, t):
    cs = min(t, 512)
    nt = t // cs
    col_spec = lambda part: pl.BlockSpec((cs, BRANCH_W), lambda bi, c, lg_ref: (bi * nt + c, 4 + part))
    tab_spec = pl.BlockSpec((cs, HEAD), lambda bi, c, lg_ref: (c, 0))
    st_spec = pl.BlockSpec((1, N_HEADS, HEAD, HEAD), lambda bi, c, lg_ref: (bi, 0, 0, 0))
    return pl.pallas_call(
        functools.partial(_ret_kernel, cs=cs),
        grid_spec=pltpu.PrefetchScalarGridSpec(
            num_scalar_prefetch=1,
            grid=(b, nt),
            in_specs=[col_spec(0), col_spec(1), col_spec(2), col_spec(3), tab_spec, tab_spec,
                      pl.BlockSpec((1, HEAD), lambda bi, c, lg_ref: (0, 0)), st_spec],
            out_specs=[pl.BlockSpec((cs, BRANCH_W), lambda bi, c, lg_ref: (bi * nt + c, 0)), st_spec],
            scratch_shapes=[pltpu.VMEM((N_HEADS, HEAD, HEAD), F32)]),
        out_shape=[jax.ShapeDtypeStruct((b * t, BRANCH_W), BF16),
                   jax.ShapeDtypeStruct((b, N_HEADS, HEAD, HEAD), F32)],
        compiler_params=_params("parallel", "arbitrary"),
        name="retention",
    )(lg, acg, acg, acg, acg, cos, sin, nw, s0)


def _merge_kernel(oa_ref, ob_ref, oc_ref, wb_ref, ga_ref, gb_ref, gc_ref, o_ref):
    acc = _sigmoid(ga_ref[...]) * jnp.dot(oa_ref[...], wb_ref[0], preferred_element_type=F32)
    acc += _sigmoid(gb_ref[...]) * jnp.dot(ob_ref[...], wb_ref[1], preferred_element_type=F32)
    acc += _sigmoid(gc_ref[...]) * jnp.dot(oc_ref[...], wb_ref[2], preferred_element_type=F32)
    o_ref[...] = acc.astype(o_ref.dtype)


def _merge(oa, ob, oc, wb, acg, tm):
    n = oa.shape[0]
    tn = 512
    o_spec = pl.BlockSpec((tm, BRANCH_W), lambda i, j: (i, 0))
    gate_spec = lambda br: pl.BlockSpec(
        (tm, tn), lambda i, j: (i, (GATE_COL0 + br * D_MODEL) // tn + j))
    return pl.pallas_call(
        _merge_kernel,
        grid=(n // tm, D_MODEL // tn),
        in_specs=[o_spec, o_spec, o_spec,
                  pl.BlockSpec((N_BRANCH, BRANCH_W, tn), lambda i, j: (0, 0, j)),
                  gate_spec(0), gate_spec(1), gate_spec(2)],
        out_specs=pl.BlockSpec((tm, tn), lambda i, j: (i, j)),
        out_shape=jax.ShapeDtypeStruct((n, D_MODEL), BF16),
        compiler_params=_params("parallel", "parallel"),
        name="branch_merge",
    )(oa, ob, oc, wb, acg, acg, acg)


def _layer_norm(y, g, b):
    mu = jnp.mean(y, axis=-1, keepdims=True)
    var = jnp.mean(jnp.square(y - mu), axis=-1, keepdims=True)
    return (y - mu) * lax.rsqrt(var + LN_EPS) * g + b


def _outproj_kernel(m_ref, x_ref, w_ref, g_ref, b_ref, wrh_ref, wrl_ref, xo_ref, xr_ref, xb_ref, sc_ref):
    h = jnp.dot(m_ref[...], w_ref[...], preferred_element_type=F32)
    x1 = _layer_norm(DN_ALPHA * x_ref[...] + h, g_ref[...], b_ref[...])
    xo_ref[...] = x1
    tm = x1.shape[0]
    for s_ in range(ROW_TILES):
        xr_ref[pl.ds(s_, tm, stride=ROW_TILES), :] = x1[:, s_ * HEAD:(s_ + 1) * HEAD]
    xh, xl = _split2(x1)
    xb_ref[...] = xh
    d = lambda a, b_: lax.dot_general(a, b_, (((1,), (1,)), ((), ())), preferred_element_type=F32)
    logits = d(wrh_ref[...], xh) + d(wrh_ref[...], xl) + d(wrl_ref[...], xh)
    sc_ref[...] = _sigmoid(logits)


def _outproj(merged, x, w_out, g, b, wr_hi, wr_lo, tm):
    n = x.shape[0]
    row = pl.BlockSpec((tm, D_MODEL), lambda i: (i, 0))
    vec = pl.BlockSpec((1, D_MODEL), lambda i: (0, 0))
    wr = pl.BlockSpec((N_EXPERTS, D_MODEL), lambda i: (0, 0))
    return pl.pallas_call(
        _outproj_kernel,
        grid=(n // tm,),
        in_specs=[row, row, pl.BlockSpec((D_MODEL, D_MODEL), lambda i: (0, 0)), vec, vec, wr, wr],
        out_specs=[row, pl.BlockSpec((tm * ROW_TILES, HEAD), lambda i: (i, 0)), row,
                   pl.BlockSpec((N_EXPERTS, tm), lambda i: (0, i))],
        out_shape=[jax.ShapeDtypeStruct((n, D_MODEL), F32), jax.ShapeDtypeStruct((n * ROW_TILES, HEAD), F32),
                   jax.ShapeDtypeStruct((n, D_MODEL), BF16), jax.ShapeDtypeStruct((N_EXPERTS, n), F32)],
        compiler_params=_params("parallel"),
        name="outproj_ln_router",
    )(merged, x, w_out, g, b, wr_hi, wr_lo)


def _route_kernel(sc_ref, bias_ref, e_ref, pos_ref, w_ref, cnt_ref, carry_ref, *, tm):
    i = pl.program_id(0)
    per_group = N_EXPERTS // N_GROUPS

    @pl.when(i == 0)
    def _():
        carry_ref[...] = jnp.zeros_like(carry_ref)

    s = sc_ref[...]
    c3 = (s + bias_ref[...]).reshape(N_GROUPS, per_group, tm)
    midx = lax.broadcasted_iota(jnp.int32, (N_GROUPS, per_group, tm), 1)
    gidx = lax.broadcasted_iota(jnp.int32, (N_GROUPS, per_group, tm), 0)

    def beats(other, mine, other_first):
        return jnp.where((other > mine) | ((other == mine) & other_first), 1, 0)

    r1 = jnp.zeros((N_GROUPS, per_group, tm), jnp.int32)
    for m in range(per_group):
        r1 = r1 + beats(c3[:, m:m + 1, :], c3, m < midx)
    gs = jnp.sum(jnp.where(r1 < 2, c3, 0.0), axis=1, keepdims=True)
    gs = jnp.broadcast_to(gs, (N_GROUPS, per_group, tm))
    r2 = jnp.zeros((N_GROUPS, per_group, tm), jnp.int32)
    for g in range(N_GROUPS):
        r2 = r2 + beats(gs[g:g + 1], gs, g < gidx)
    masked = jnp.where(r2 < TOPK_GROUPS, c3, -jnp.inf).reshape(N_EXPERTS, tm)
    eidx = lax.broadcasted_iota(jnp.int32, (N_EXPERTS, tm), 0)
    r3 = jnp.zeros((N_EXPERTS, tm), jnp.int32)
    for e in range(N_EXPERTS):
        r3 = r3 + beats(masked[e:e + 1, :], masked, e < eidx)
    sel = r3 < TOP_K
    w = jnp.where(sel, s, 0.0)
    w = w / jnp.sum(w, axis=0, keepdims=True) * ROUTE_SCALE
    sel_bf = jnp.where(sel, 1.0, 0.0).astype(BF16)
    tr = lax.broadcasted_iota(jnp.int32, (tm, tm), 0)
    tc_ = lax.broadcasted_iota(jnp.int32, (tm, tm), 1)
    before = jnp.where(tr < tc_, 1.0, 0.0).astype(BF16)
    carry = carry_ref[...]
    pos = (jnp.dot(sel_bf, before, preferred_element_type=F32)
           + jnp.concatenate([carry] * (tm // HEAD), axis=1))
    carry_ref[...] = carry + jnp.dot(sel_bf, jnp.ones((tm, HEAD), BF16), preferred_element_type=F32)
    er = lax.broadcasted_iota(jnp.int32, (N_EXPERTS, N_EXPERTS), 0)
    ec = lax.broadcasted_iota(jnp.int32, (N_EXPERTS, N_EXPERTS), 1)
    lower_e = jnp.where(ec < er, 1.0, 0.0).astype(BF16)
    jidx = jnp.dot(lower_e, sel_bf, preferred_element_type=F32)
    eidx_f = eidx.astype(F32)
    e_rows, p_rows, w_rows = [], [], []
    for j in range(TOP_K):
        oh = sel & (jidx == float(j))
        e_rows.append(jnp.sum(jnp.where(oh, eidx_f, 0.0), axis=0, keepdims=True))
        p_rows.append(jnp.sum(jnp.where(oh, pos, 0.0), axis=0, keepdims=True))
        w_rows.append(jnp.sum(jnp.where(oh, w, 0.0), axis=0, keepdims=True))
    zpad = [jnp.zeros((8 - TOP_K, tm), F32)]
    e_ref[...] = jnp.concatenate(e_rows + zpad, axis=0).astype(jnp.int32)
    pos_ref[...] = jnp.concatenate(p_rows + zpad, axis=0).astype(jnp.int32)
    w_ref[...] = jnp.concatenate(w_rows + zpad, axis=0)

    @pl.when(i == pl.num_programs(0) - 1)
    def _():
        cnt_ref[...] = carry_ref[...]


def _route(scores_t, bias_b, tm):
    n = scores_t.shape[1]
    blk = pl.BlockSpec((8, tm), lambda i: (0, i))
    return pl.pallas_call(
        functools.partial(_route_kernel, tm=tm),
        grid=(n // tm,),
        in_specs=[pl.BlockSpec((N_EXPERTS, tm), lambda i: (0, i)),
                  pl.BlockSpec((N_EXPERTS, tm), lambda i: (0, 0))],
        out_specs=[blk, blk, blk, pl.BlockSpec((N_EXPERTS, HEAD), lambda i: (0, 0))],
        out_shape=[jax.ShapeDtypeStruct((8, n), jnp.int32), jax.ShapeDtypeStruct((8, n), jnp.int32),
                   jax.ShapeDtypeStruct((8, n), F32), jax.ShapeDtypeStruct((N_EXPERTS, HEAD), F32)],
        scratch_shapes=[pltpu.VMEM((N_EXPERTS, HEAD), F32)],
        compiler_params=_params("arbitrary"),
        name="route_topk",
    )(scores_t, bias_b)


def _expert_kernel(be_ref, s_m3, s_m2, s_m1, s_cur, s_p1, s_p2, x_hbm, wg_ref, wu_ref, wd_ref,
                   out_hbm, xbuf, ybuf, sem_in, sem_out, *, n_tok):
    i = pl.program_id(0)
    last = pl.num_programs(0) - 1
    slot = i % EXPERT_BUFS
    nxt = (i + 2) % EXPERT_BUFS
    prv = nxt

    def buf_row(k):
        return k * ROW_PITCH if isinstance(k, int) else pl.multiple_of(k * ROW_PITCH, 8)

    def in_copy(slots_ref, buf, k):
        src = pl.multiple_of((slots_ref[0, 0, k] >> 3) * ROW_TILES, ROW_TILES)
        return pltpu.make_async_copy(x_hbm.at[pl.ds(src, ROW_TILES)],
                                     xbuf.at[buf, pl.ds(buf_row(k), ROW_TILES)], sem_in.at[buf, k])

    def out_copy(slots_ref, buf, k):
        v = slots_ref[0, 0, k]
        dst = pl.multiple_of(((v & 7) * n_tok + (v >> 3)) * ROW_TILES, ROW_TILES)
        return pltpu.make_async_copy(ybuf.at[buf, pl.ds(buf_row(k), ROW_TILES)],
                                     out_hbm.at[pl.ds(dst, ROW_TILES)], sem_out.at[buf, k])

    def rolled(fn):
        def body(k, carry):
            fn(k)
            return carry
        lax.fori_loop(0, MOE_BLOCK, body, 0, unroll=8)

    @pl.when(i == 0)
    def _():
        rolled(lambda k: in_copy(s_cur, 0, k).start())
        rolled(lambda k: in_copy(s_p1, 1, k).start())

    @pl.when(i < EXPERT_BUFS)
    def _():
        rolled(lambda k: in_copy(s_cur, slot, k).wait())

    @pl.when(i >= EXPERT_BUFS)
    def _():
        def both(k):
            in_copy(s_cur, slot, k).wait()
            out_copy(s_m3, slot, k).wait()
        rolled(both)

    def block(gather_ahead, scatter_prev):
        col = lambda ref, s_: ref.at[slot][pl.ds(s_, MOE_BLOCK, stride=ROW_PITCH), :]
        xb = jnp.concatenate([col(xbuf, s_) for s_ in range(ROW_TILES)], axis=1).astype(BF16)
        for k in range(MOE_BLOCK):
            if gather_ahead:
                in_copy(s_p2, nxt, k).start()
            if scatter_prev:
                out_copy(s_m1, prv, k).start()
        hg = jnp.dot(xb, wg_ref[0], preferred_element_type=F32)
        hu = jnp.dot(xb, wu_ref[0], preferred_element_type=F32)
        hb = (_silu(hg) * hu).astype(BF16)
        y = jnp.dot(hb, wd_ref[0], preferred_element_type=F32)
        for s_ in range(ROW_TILES):
            ybuf.at[slot][pl.ds(s_, MOE_BLOCK, stride=ROW_PITCH), :] = y[:, s_ * HEAD:(s_ + 1) * HEAD]

    @pl.when(i == 0)
    def _():
        block(True, False)

    @pl.when((i > 0) & (i < last - 1))
    def _():
        block(True, True)

    @pl.when(i == last - 1)
    def _():
        block(False, True)

    @pl.when(i == last)
    def _():
        block(False, True)
        rolled(lambda k: out_copy(s_m2, (i + 1) % EXPERT_BUFS, k).wait())
        rolled(lambda k: out_copy(s_cur, slot, k).start())
        rolled(lambda k: out_copy(s_m1, prv, k).wait())
        rolled(lambda k: out_copy(s_cur, slot, k).wait())


def _experts(x1r, block_e, slots, wg, wu, wd):
    n = x1r.shape[0] // ROW_TILES
    n_blocks = block_e.shape[0]
    slot_spec = lambda off: pl.BlockSpec(
        (1, 1, MOE_BLOCK), lambda i, be: (jnp.clip(i + off, 0, n_blocks - 1), 0, 0), memory_space=pltpu.SMEM)
    buf = pltpu.VMEM((EXPERT_BUFS, MOE_BLOCK * ROW_PITCH, HEAD), F32)
    sem = pltpu.SemaphoreType.DMA((EXPERT_BUFS, MOE_BLOCK))
    return pl.pallas_call(
        functools.partial(_expert_kernel, n_tok=n),
        grid_spec=pltpu.PrefetchScalarGridSpec(
            num_scalar_prefetch=1,
            grid=(n_blocks,),
            in_specs=[slot_spec(off) for off in (-3, -2, -1, 0, 1, 2)] + [
                pl.BlockSpec(memory_space=pl.ANY),
                pl.BlockSpec((1, D_MODEL, EXPERT_FF), lambda i, be: (be[i], 0, 0)),
                pl.BlockSpec((1, D_MODEL, EXPERT_FF), lambda i, be: (be[i], 0, 0)),
                pl.BlockSpec((1, EXPERT_FF, D_MODEL), lambda i, be: (be[i], 0, 0))],
            out_specs=pl.BlockSpec(memory_space=pl.ANY),
            scratch_shapes=[buf, buf, sem, sem]),
        out_shape=jax.ShapeDtypeStruct(((TOP_K * n + 2 * MOE_BLOCK) * ROW_TILES, HEAD), F32),
        compiler_params=_params("arbitrary"),
        name="routed_experts",
    )(block_e, *([slots] * 6), x1r, wg, wu, wd)


def _slot_tables(e6, pos6, counts):
    n = e6.shape[1]
    n_assign = n * TOP_K
    n_blocks = -(-n_assign // MOE_BLOCK) + N_EXPERTS
    padded = (counts + MOE_BLOCK - 1) // MOE_BLOCK * MOE_BLOCK
    pad_end = jnp.cumsum(padded)
    pad_start = pad_end - padded
    onehot = e6[:TOP_K, :, None] == jnp.arange(N_EXPERTS, dtype=jnp.int32)
    dest = jnp.sum(jnp.where(onehot, pad_start, 0), axis=-1) + pos6[:TOP_K]
    packed = (jnp.arange(n, dtype=jnp.int32)[None, :] * 8
              + jnp.arange(TOP_K, dtype=jnp.int32)[:, None])
    pad_slots = (jnp.arange(n_blocks * MOE_BLOCK, dtype=jnp.int32) % (2 * MOE_BLOCK)) * 8 + TOP_K
    slots = pad_slots.at[dest.reshape(-1)].set(packed.reshape(-1), unique_indices=True)
    blk_start = jnp.arange(n_blocks, dtype=jnp.int32) * MOE_BLOCK
    block_e = jnp.minimum(jnp.sum((pad_end[None, :] <= blk_start[:, None]).astype(jnp.int32), axis=1),
                          N_EXPERTS - 1)
    return block_e, slots.reshape(n_blocks, 1, MOE_BLOCK)


def _combine_kernel(*refs):
    r_refs = refs[:TOP_K]
    w6_ref, x_ref, xb_ref, wg_ref, wu_ref, wd_ref, g_ref, b_ref, xo_ref, xob_ref = refs[TOP_K:]
    xb = xb_ref[...]
    hb = (_silu(jnp.dot(xb, wg_ref[...], preferred_element_type=F32))
          * jnp.dot(xb, wu_ref[...], preferred_element_type=F32)).astype(BF16)
    shared = jnp.dot(hb, wd_ref[...], preferred_element_type=F32)
    w6 = w6_ref[...]
    tm = w6.shape[0]
    rows2d = lambda ref: jnp.concatenate(
        [ref[pl.ds(s_, tm, stride=ROW_TILES), :] for s_ in range(ROW_TILES)], axis=1)
    routed = rows2d(r_refs[0]) * w6[:, 0:1]
    for j in range(1, TOP_K):
        routed = routed + rows2d(r_refs[j]) * w6[:, j:j + 1]
    x2 = _layer_norm(DN_ALPHA * x_ref[...] + (routed + shared), g_ref[...], b_ref[...])
    xo_ref[...] = x2
    xob_ref[...] = x2.astype(BF16)


def _combine(routed, w6, x1, x1b, wg, wu, wd, g, b, tm):
    n = x1.shape[0]
    nt = n // tm
    row = pl.BlockSpec((tm, D_MODEL), lambda i: (i, 0))
    vec = pl.BlockSpec((1, D_MODEL), lambda i: (0, 0))
    w_in = pl.BlockSpec((D_MODEL, EXPERT_FF), lambda i: (0, 0))
    r_specs = [pl.BlockSpec((tm * ROW_TILES, HEAD), functools.partial(lambda i, j: (j * nt + i, 0), j=j))
               for j in range(TOP_K)]
    return pl.pallas_call(
        _combine_kernel,
        grid=(nt,),
        in_specs=r_specs + [pl.BlockSpec((tm, 8), lambda i: (i, 0)), row, row, w_in, w_in,
                            pl.BlockSpec((EXPERT_FF, D_MODEL), lambda i: (0, 0)), vec, vec],
        out_specs=[row, row],
        out_shape=[jax.ShapeDtypeStruct((n, D_MODEL), F32), jax.ShapeDtypeStruct((n, D_MODEL), BF16)],
        compiler_params=_params("parallel"),
        name="combine_shared_ln",
    )(*([routed] * TOP_K), w6, x1, x1b, wg, wu, wd, g, b)


def _ple_kernel(x_ref, xb_ref, p_ref, wg_ref, wp_ref, xo_ref, xob_ref):
    gate = _sigmoid(jnp.dot(xb_ref[...], wg_ref[...], preferred_element_type=F32))
    pe = jnp.dot(p_ref[...].astype(BF16), wp_ref[...], preferred_element_type=F32)
    x3 = x_ref[...] + pe * gate
    xo_ref[...] = x3
    xob_ref[...] = x3.astype(BF16)


def _ple(x2, x2b, p, wg, wp, tm):
    n = x2.shape[0]
    row = pl.BlockSpec((tm, D_MODEL), lambda i: (i, 0))
    return pl.pallas_call(
        _ple_kernel,
        grid=(n // tm,),
        in_specs=[row, row, pl.BlockSpec((tm, PLE_DIM), lambda i: (i, 0)),
                  pl.BlockSpec((D_MODEL, D_MODEL), lambda i: (0, 0)),
                  pl.BlockSpec((PLE_DIM, D_MODEL), lambda i: (0, 0))],
        out_specs=[row, row],
        out_shape=[jax.ShapeDtypeStruct((n, D_MODEL), F32), jax.ShapeDtypeStruct((n, D_MODEL), BF16)],
        compiler_params=_params("parallel"),
        name="ple",
    )(x2, x2b, p, wg, wp)


def _prep_weights(w_in, hgrn_lb_logits, gdn_a_log, gdn_dt_bias, w_branch, w_out, w_router,
                  w_exp_gate, w_exp_up, w_exp_down, w_sh_gate, w_sh_up, w_sh_down, w_ple_gate, w_ple_proj):
    pts = np.cumsum([1024, 1024, 1024, 1024, 512, 512, 1024, 1024, 8, 8, 1024, 1024, 1024, 1024]).tolist()
    a_end, b_end, c_end = pts[3], pts[9], pts[13]
    w_acg = jnp.concatenate([w_in[:, :, :a_end], w_in[:, :, b_end:c_end], w_in[:, :, c_end:]], axis=-1)
    w_b = jnp.concatenate([w_in[:, :, a_end:b_end],
                           jnp.zeros((DEPTH, D_MODEL, B_COLS - (b_end - a_end)), w_in.dtype)], axis=-1)
    lb_p = jax.nn.softmax(hgrn_lb_logits.astype(F32), axis=0)
    lb = jnp.maximum(jnp.cumsum(lb_p, axis=0) - lb_p[:1], 0.0)
    pad_row = lambda v: jnp.zeros((DEPTH, 1, HEAD), F32).at[:, 0, N_HEADS:2 * N_HEADS].set(v.astype(F32))
    wr = jnp.swapaxes(w_router.astype(F32), -1, -2)
    wr_hi = wr.astype(BF16)
    return dict(
        w_acg=w_acg.astype(BF16), w_b=w_b.astype(BF16),
        llb=jnp.log(lb)[:, None, :], l1m=jnp.log1p(-lb)[:, None, :], oml=(1.0 - lb)[:, None, :],
        alog_row=pad_row(gdn_a_log), dtb_row=pad_row(gdn_dt_bias),
        w_branch=w_branch.astype(BF16), w_out=w_out.astype(BF16),
        wr_hi=wr_hi, wr_lo=(wr - wr_hi.astype(F32)).astype(BF16),
        wg=w_exp_gate.astype(BF16), wu=w_exp_up.astype(BF16), wd=w_exp_down.astype(BF16),
        sg=w_sh_gate.astype(BF16), su=w_sh_up.astype(BF16), sd=w_sh_down.astype(BF16),
        pg=w_ple_gate.astype(BF16), pp=w_ple_proj.astype(BF16))


def _run_trunk(x, p, pos0, states, pw, small):
    (hgrn_norm, gdn_conv, gdn_norm, ret_norm, ln1_g, ln1_b, ln2_g, ln2_b, router_bias) = small
    s_hgrn, s_gdn, s_conv, s_ret = states
    b, t, d = x.shape
    n = b * t
    tm = min(n, 512)
    xf = x.reshape(n, d)
    xb = xf.astype(BF16)
    half = HEAD // 2
    inv = ROPE_BASE ** (-jnp.arange(half, dtype=F32) / half)
    ang = (pos0 + jnp.arange(t, dtype=F32))[:, None] * inv[None, :]
    cos_t = jnp.concatenate([jnp.cos(ang), jnp.cos(ang)], axis=-1)
    sin_t = jnp.concatenate([-jnp.sin(ang), jnp.sin(ang)], axis=-1)
    lg = jnp.log1p(-jnp.exp2(-5.0 - jnp.arange(N_HEADS, dtype=F32)))
    row = lambda v: v.reshape(1, -1).astype(F32)
    new = ([], [], [], [])
    for i in range(DEPTH):
        acg = _matmul(xb, pw['w_acg'][i], min(n, 2048), 1024)
        yb = _matmul(xb, pw['w_b'][i], min(n, 2048), 640)
        o_a, st_a = _hgrn(acg, pw['llb'][i], pw['l1m'][i], pw['oml'][i], row(hgrn_norm[i]),
                          jnp.swapaxes(s_hgrn[i], -1, -2), b, t)
        buf8 = jnp.concatenate([jnp.zeros((b, 8 - (CONV_W - 1), B_CONV_CH), F32), s_conv[i]], axis=1)
        qkvc = _gdn_conv(yb, gdn_conv[i], buf8, b, t)
        o_b, st_b = _gdn(qkvc, yb, pw['alog_row'][i], pw['dtb_row'][i], row(gdn_norm[i]), s_gdn[i], b, t)
        o_c, st_c = _retention(acg, lg, cos_t, sin_t, row(ret_norm[i]), s_ret[i], b, t)
        merged = _merge(o_a, o_b, o_c, pw['w_branch'][i], acg, min(n, 1024))
        x1, x1r, x1b, scores = _outproj(merged, xf, pw['w_out'][i], row(ln1_g[i]), row(ln1_b[i]),
                                        pw['wr_hi'][i], pw['wr_lo'][i], tm)
        bias_b = jnp.broadcast_to(router_bias[i].astype(F32)[:, None], (N_EXPERTS, tm))
        e6, pos6, w6, cnt = _route(scores, bias_b, tm)
        block_e, slots = _slot_tables(e6, pos6, cnt[:, 0].astype(jnp.int32))
        routed = _experts(x1r, block_e, slots, pw['wg'][i], pw['wu'][i], pw['wd'][i])
        x2, x2b = _combine(routed, w6.T, x1, x1b, pw['sg'][i], pw['su'][i],
                           pw['sd'][i], row(ln2_g[i]), row(ln2_b[i]), min(n, 256))
        xf, xb = _ple(x2, x2b, p[i].reshape(n, PLE_DIM), pw['pg'][i], pw['pp'][i], tm)
        new[0].append(jnp.swapaxes(st_a, -1, -2))
        new[1].append(st_b)
        new[2].append(yb.reshape(b, t, B_COLS)[:, t - (CONV_W - 1):, :B_CONV_CH])
        new[3].append(st_c)
    return xf.reshape(b, t, d), tuple(jnp.stack(a) for a in new)


def kernel(x_prompt, x_sample, p_prompt, p_sample, state_hgrn, state_gdn, state_gdn_conv, state_ret,
           w_in, hgrn_lb_logits, hgrn_norm, gdn_conv, gdn_a_log, gdn_dt_bias, gdn_norm, ret_norm,
           w_branch, w_out, ln1_g, ln1_b, ln2_g, ln2_b, w_router, router_bias,
           w_exp_gate, w_exp_up, w_exp_down, w_sh_gate, w_sh_up, w_sh_down, w_ple_gate, w_ple_proj):
    pw = _prep_weights(w_in, hgrn_lb_logits, gdn_a_log, gdn_dt_bias, w_branch, w_out, w_router,
                       w_exp_gate, w_exp_up, w_exp_down, w_sh_gate, w_sh_up, w_sh_down,
                       w_ple_gate, w_ple_proj)
    small = (hgrn_norm, gdn_conv, gdn_norm, ret_norm, ln1_g, ln1_b, ln2_g, ln2_b, router_bias)
    bp = x_prompt.shape[0]
    zero_states = (jnp.zeros((DEPTH, bp, N_HEADS, HEAD, HEAD), F32),
                   jnp.zeros((DEPTH, bp, N_HEADS, HEAD, HEAD), F32),
                   jnp.zeros((DEPTH, bp, CONV_W - 1, B_CONV_CH), F32),
                   jnp.zeros((DEPTH, bp, N_HEADS, HEAD, HEAD), F32))
    y_p, (hp, gp, cp, rp) = _run_trunk(x_prompt, p_prompt, 0.0, zero_states, pw, small)
    y_s, (hs, gs, cs, rs) = _run_trunk(x_sample, p_sample, float(PAST_LEN),
                                       (state_hgrn, state_gdn, state_gdn_conv, state_ret), pw, small)
    return (y_p, y_s, hp, gp, cp, rp, hs, gs, cs, rs)
```

```python
import functools
import math

import jax
import jax.numpy as jnp
import numpy as np
from jax import lax
from jax.experimental import pallas as pl
from jax.experimental.pallas import tpu as pltpu

F32 = jnp.float32
BF16 = jnp.bfloat16

D_MODEL = 2048
DEPTH = 4
PAST_LEN = 4096
HEAD = 128
N_HEADS = 8
B_QK_HEADS = 4
CONV_W = 4
BRANCH_W = N_HEADS * HEAD
B_CONV_CH = 2 * B_QK_HEADS * HEAD + N_HEADS * HEAD
ROPE_BASE = 10000.0
N_BRANCH = 3
N_EXPERTS = 64
N_GROUPS = 8
TOPK_GROUPS = 4
TOP_K = 6
EXPERT_FF = 512
ROUTE_SCALE = 2.5
MOE_BLOCK = 256
ROW_TILES = D_MODEL // HEAD
EXPERT_BUFS = 3
ROW_PITCH = 24
PLE_DIM = 256
DN_ALPHA = (2.0 * DEPTH) ** 0.25
LN_EPS = 1e-5
NORM_EPS = 1e-6

HGRN_SUB = 16
HGRN_HPB = 8
GDN_CHUNK = 64
GDN_CHUNKS_PER_STEP = 4
GDN_APPLY_PASSES = (3, 1, 1, 1, 1, 1)
GDN_SQUARE_PASSES = (3, 1, 1, 1, 1)
ACG_COLS = 4 * BRANCH_W + 4 * BRANCH_W + N_BRANCH * D_MODEL
GATE_COL0 = 8 * BRANCH_W
B_COLS = 3200
B_SMALL_COL = 3072

VMEM_LIMIT = 56 * 1024 * 1024


def _params(*sem):
    return pltpu.CompilerParams(dimension_semantics=sem, vmem_limit_bytes=VMEM_LIMIT)


def _bdot(a, b):
    return jnp.dot(a.astype(BF16), b.astype(BF16), preferred_element_type=F32)


def _bdot_nt(a, b):
    return lax.dot_general(a.astype(BF16), b.astype(BF16), (((1,), (1,)), ((), ())),
                           preferred_element_type=F32)


def _bdot_tn(a, b):
    return lax.dot_general(a.astype(BF16), b.astype(BF16), (((0,), (0,)), ((), ())),
                           preferred_element_type=F32)


def _split2(x):
    hi = x.astype(BF16)
    lo = (x - hi.astype(F32)).astype(BF16)
    return hi, lo


def _split3(x):
    hi = x.astype(BF16)
    r = x - hi.astype(F32)
    mid = r.astype(BF16)
    lo = (r - mid.astype(F32)).astype(BF16)
    return hi, mid, lo


def _mask_dot(mask_bf, x):
    hi, mid, lo = _split3(x)
    d = functools.partial(jnp.dot, preferred_element_type=F32)
    return d(mask_bf, hi) + d(mask_bf, mid) + d(mask_bf, lo)


def _dot_x3(a, b):
    ah, al = _split2(a)
    bh, bl = _split2(b)
    d = functools.partial(jnp.dot, preferred_element_type=F32)
    return d(ah, bh) + d(ah, bl) + d(al, bh)


def _dot_p(a, b, passes):
    return _dot_x3(a, b) if passes == 3 else _bdot(a, b)


def _sigmoid(x):
    return 1.0 / (1.0 + jnp.exp(-x))


def _silu(x):
    return x * _sigmoid(x)


def _softplus(x):
    return jnp.maximum(x, 0.0) + jnp.log1p(jnp.exp(-jnp.abs(x)))


def _log_sigmoid(x):
    return jnp.minimum(x, 0.0) - jnp.log1p(jnp.exp(-jnp.abs(x)))


def _mm_kernel(x_ref, w_ref, o_ref):
    o_ref[...] = jnp.dot(x_ref[...], w_ref[...], preferred_element_type=F32).astype(o_ref.dtype)


def _matmul(x, w, tm, tn):
    m, k = x.shape
    n = w.shape[1]
    return pl.pallas_call(
        _mm_kernel,
        grid=(m // tm, n // tn),
        in_specs=[pl.BlockSpec((tm, k), lambda i, j: (i, 0)),
                  pl.BlockSpec((k, tn), lambda i, j: (0, j))],
        out_specs=pl.BlockSpec((tm, tn), lambda i, j: (i, j)),
        out_shape=jax.ShapeDtypeStruct((m, n), F32),
        compiler_params=_params("parallel", "parallel"),
        name="inproj_matmul",
    )(x, w)


def _hgrn_kernel(q_ref, f_ref, i_ref, g_ref, llb_ref, l1m_ref, oml_ref, nw_ref, s0_ref,
                 o_ref, sout_ref, st_ref, *, tc):
    c = pl.program_id(2)

    @pl.when(c == 0)
    def _():
        st_ref[...] = s0_ref[0]

    nb = tc // HGRN_SUB
    heads = range(HGRN_HPB)
    lanes = [slice(h * HEAD, (h + 1) * HEAD) for h in heads]
    shp = (nb, HGRN_SUB, HEAD)
    z = [f_ref[:, lanes[h]] for h in heads]
    bterm = [l1m_ref[:, lanes[h]] + _log_sigmoid(z[h]) for h in heads]
    log_f = [jnp.maximum(llb_ref[:, lanes[h]], bterm[h])
             + jnp.log1p(jnp.exp(-jnp.abs(llb_ref[:, lanes[h]] - bterm[h]))) for h in heads]
    key3 = [(oml_ref[:, lanes[h]] * (1.0 / (1.0 + jnp.exp(z[h])))).reshape(shp) for h in heads]
    r = lax.broadcasted_iota(jnp.int32, (tc, tc), 0)
    s = lax.broadcasted_iota(jnp.int32, (tc, tc), 1)
    seg_mask = jnp.where((r // HGRN_SUB == s // HGRN_SUB) & (s <= r), 1.0, 0.0).astype(BF16)
    cum3 = [_mask_dot(seg_mask, log_f[h]).reshape(shp) for h in heads]
    q3 = [q_ref[:, lanes[h]].reshape(shp) for h in heads]
    v3 = [i_ref[:, lanes[h]].reshape(shp) for h in heads]
    rows = lax.broadcasted_iota(jnp.int32, shp, 1)
    ones_bf = jnp.ones((HEAD, HEAD), BF16)
    att = []
    for h in heads:
        parts = []
        for si in range(HGRN_SUB):
            arg = jnp.where(rows >= si, cum3[h] - cum3[h][:, si:si + 1, :], -1e30)
            parts.append((q3[h] * (key3[h][:, si:si + 1, :] * jnp.exp(arg))).reshape(tc, HEAD))
        p_hi, p_lo = _split2(jnp.concatenate(parts, axis=0))
        att.append(jnp.dot(p_hi, ones_bf, preferred_element_type=F32)
                   + jnp.dot(p_lo, ones_bf, preferred_element_type=F32))
    o = []
    for h in heads:
        acc = jnp.zeros((tc, HEAD), F32)
        for si in range(HGRN_SUB):
            vs = jnp.broadcast_to(v3[h][:, si:si + 1, :], shp).reshape(tc, HEAD)
            acc = acc + att[h][si * tc:(si + 1) * tc, :] * vs
        o.append(acc)
    last3 = [cum3[h][:, HGRN_SUB - 1:HGRN_SUB, :] for h in heads]
    qtil = [q3[h] * jnp.exp(cum3[h]) for h in heads]
    ktil = [key3[h] * jnp.exp(last3[h] - cum3[h]) for h in heads]
    elast = [jnp.exp(last3[h]) for h in heads]
    kv = [[_bdot_tn(v3[h][j], ktil[h][j]) for j in range(nb)] for h in heads]
    st = [st_ref[h] for h in heads]
    outs = [[] for _ in heads]
    for j in range(nb):
        for h in heads:
            outs[h].append(o[h][j * HGRN_SUB:(j + 1) * HGRN_SUB, :] + _bdot_nt(qtil[h][j], st[h]))
            st[h] = elast[h][j] * st[h] + kv[h][j]
    for h in heads:
        st_ref[h] = st[h]
        oh = jnp.concatenate(outs[h], axis=0)
        oh = oh * lax.rsqrt(jnp.mean(oh * oh, axis=-1, keepdims=True) + NORM_EPS) * nw_ref[...]
        o_ref[:, lanes[h]] = (oh * _silu(g_ref[:, lanes[h]])).astype(o_ref.dtype)

    @pl.when(c == pl.num_programs(2) - 1)
    def _():
        sout_ref[0] = st_ref[...]


def _hgrn(acg, llb, l1m, oml, nw, s0t, b, t):
    tc = min(t, 128)
    nt = t // tc
    hw = HGRN_HPB * HEAD
    ng = N_HEADS // HGRN_HPB
    col_spec = lambda part: pl.BlockSpec((tc, hw), lambda bi, hg, c: (bi * nt + c, part * ng + hg))
    vec_spec = pl.BlockSpec((1, hw), lambda bi, hg, c: (0, hg))
    st_spec = pl.BlockSpec((1, HGRN_HPB, HEAD, HEAD), lambda bi, hg, c: (bi, hg, 0, 0))
    return pl.pallas_call(
        functools.partial(_hgrn_kernel, tc=tc),
        grid=(b, ng, nt),
        in_specs=[col_spec(0), col_spec(1), col_spec(2), col_spec(3),
                  vec_spec, vec_spec, vec_spec, pl.BlockSpec((1, HEAD), lambda bi, hg, c: (0, 0)), st_spec],
        out_specs=[pl.BlockSpec((tc, hw), lambda bi, hg, c: (bi * nt + c, hg)), st_spec],
        out_shape=[jax.ShapeDtypeStruct((b * t, BRANCH_W), BF16),
                   jax.ShapeDtypeStruct((b, N_HEADS, HEAD, HEAD), F32)],
        scratch_shapes=[pltpu.VMEM((HGRN_HPB, HEAD, HEAD), F32)],
        compiler_params=_params("parallel", "parallel", "arbitrary"),
        name="hgrn2",
    )(acg, acg, acg, acg, llb, l1m, oml, nw, s0t)


def _conv_kernel(u_ref, w_ref, buf_ref, o_ref, tail_ref, *, tc):
    c = pl.program_id(1)

    @pl.when(c == 0)
    def _():
        tail_ref[...] = buf_ref[0]

    u = u_ref[...]
    ext = jnp.concatenate([tail_ref[...], u], axis=0)
    acc = u * w_ref[CONV_W - 1:CONV_W, :]
    for j in range(CONV_W - 1):
        acc = acc + ext[5 + j:5 + j + tc, :] * w_ref[j:j + 1, :]
    tail_ref[...] = u[tc - 8:, :]
    y = _silu(acc)
    nqk = B_QK_HEADS * HEAD
    for hh in range(2 * B_QK_HEADS):
        blk = y[:, hh * HEAD:(hh + 1) * HEAD]
        nrm = blk * lax.rsqrt(jnp.sum(blk * blk, axis=-1, keepdims=True) + NORM_EPS)
        if hh < B_QK_HEADS:
            nrm = nrm * (HEAD ** -0.5)
        o_ref[:, hh * HEAD:(hh + 1) * HEAD] = nrm
    o_ref[:, 2 * nqk:] = y[:, 2 * nqk:]


def _gdn_conv(yb, w, buf8, b, t):
    tc = min(t, 256)
    nt = t // tc
    return pl.pallas_call(
        functools.partial(_conv_kernel, tc=tc),
        grid=(b, nt),
        in_specs=[pl.BlockSpec((tc, B_CONV_CH), lambda bi, c: (bi * nt + c, 0)),
                  pl.BlockSpec((CONV_W, B_CONV_CH), lambda bi, c: (0, 0)),
                  pl.BlockSpec((1, 8, B_CONV_CH), lambda bi, c: (bi, 0, 0))],
        out_specs=pl.BlockSpec((tc, B_CONV_CH), lambda bi, c: (bi * nt + c, 0)),
        out_shape=jax.ShapeDtypeStruct((b * t, B_CONV_CH), F32),
        scratch_shapes=[pltpu.VMEM((8, B_CONV_CH), F32)],
        compiler_params=_params("parallel", "arbitrary"),
        name="gdn_conv",
    )(yb, w, buf8)


def _gdn_kernel(qkv_ref, z_ref, sm_ref, alog_ref, dtb_ref, nw_ref, s0_ref, o_ref, sout_ref, st_ref, *, nc):
    c = pl.program_id(1)
    cs = GDN_CHUNK

    @pl.when(c == 0)
    def _():
        st_ref[...] = s0_ref[0]

    r = lax.broadcasted_iota(jnp.int32, (cs, cs), 0)
    s = lax.broadcasted_iota(jnp.int32, (cs, cs), 1)
    lower = r >= s
    strict = r > s
    tril_bf = jnp.where(lower, 1.0, 0.0).astype(BF16)
    triu_bf = jnp.where(r <= s, 1.0, 0.0).astype(BF16)
    above_bf = jnp.where(r < s, 1.0, 0.0).astype(BF16)
    nw = nw_ref[...]
    d_tn = lambda a: lax.dot_general(a, triu_bf, (((0,), (0,)), ((), ())), preferred_element_type=F32)
    rep = N_HEADS // B_QK_HEADS
    heads = range(N_HEADS)
    chunks = range(nc)
    q, k, qk, cum, rem, dec, x, y = ([] for _ in range(8))
    for ci in chunks:
        rs = slice(ci * cs, (ci + 1) * cs)
        sm = sm_ref[rs, :]
        beta_all = _sigmoid(sm)
        g_all = -jnp.exp(alog_ref[...]) * _softplus(sm + dtb_ref[...])
        cum_all = _mask_dot(tril_bf, g_all)
        rem_all = _mask_dot(above_bf, g_all)
        g_hi, g_mid, g_lo = _split3(g_all)
        cum_rows = d_tn(g_hi) + d_tn(g_mid) + d_tn(g_lo)
        q.append([qkv_ref[rs, qh * HEAD:(qh + 1) * HEAD] for qh in range(B_QK_HEADS)])
        k.append([qkv_ref[rs, (B_QK_HEADS + qh) * HEAD:(B_QK_HEADS + qh + 1) * HEAD]
                  for qh in range(B_QK_HEADS)])
        kk = [_bdot_nt(k[ci][qh], k[ci][qh]) for qh in range(B_QK_HEADS)]
        qk.append([_bdot_nt(q[ci][qh], k[ci][qh]) for qh in range(B_QK_HEADS)])
        cum.append([]), rem.append([]), dec.append([]), x.append([]), y.append([])
        for h in heads:
            col = N_HEADS + h
            v = qkv_ref[rs, (2 * B_QK_HEADS + h) * HEAD:(2 * B_QK_HEADS + h + 1) * HEAD]
            beta = beta_all[:, h:h + 1]
            cum[ci].append(jnp.broadcast_to(cum_all[:, col:col + 1], (cs, HEAD)))
            rem[ci].append(jnp.broadcast_to(rem_all[:, col:col + 1], (cs, HEAD)))
            seg = cum_all[:, col:col + 1] - cum_rows[col:col + 1, :]
            dec[ci].append(jnp.where(lower, jnp.exp(jnp.where(lower, seg, 0.0)), 0.0))
            x[ci].append(jnp.where(strict, -(kk[h // rep] * beta) * dec[ci][h], 0.0))
            y[ci].append(jnp.concatenate([v * beta, k[ci][h // rep] * beta * jnp.exp(cum[ci][h])], axis=1))
    for lvl in range(len(GDN_APPLY_PASSES)):
        y = [[y[ci][h] + _dot_p(x[ci][h], y[ci][h], GDN_APPLY_PASSES[lvl]) for h in heads] for ci in chunks]
        if lvl < len(GDN_SQUARE_PASSES):
            x = [[_dot_p(x[ci][h], x[ci][h], GDN_SQUARE_PASSES[lvl]) for h in heads] for ci in chunks]
    st = [st_ref[h] for h in heads]
    for ci in chunks:
        rs = slice(ci * cs, (ci + 1) * cs)
        u = [y[ci][h][:, :HEAD] - _bdot(y[ci][h][:, HEAD:], st[h]) for h in heads]
        o = [_bdot(q[ci][h // rep] * jnp.exp(cum[ci][h]), st[h]) + _bdot(qk[ci][h // rep] * dec[ci][h], u[h])
             for h in heads]
        st = [jnp.exp(cum[ci][h][cs - 1:cs, :]) * st[h] + _bdot_tn(k[ci][h // rep] * jnp.exp(rem[ci][h]), u[h])
              for h in heads]
        for h in heads:
            on = o[h] * lax.rsqrt(jnp.mean(o[h] * o[h], axis=-1, keepdims=True) + NORM_EPS) * nw
            zg = z_ref[rs, h * HEAD:(h + 1) * HEAD]
            o_ref[rs, h * HEAD:(h + 1) * HEAD] = (on * _silu(zg)).astype(o_ref.dtype)
    for h in heads:
        st_ref[h] = st[h]

    @pl.when(c == pl.num_programs(1) - 1)
    def _():
        sout_ref[0] = st_ref[...]


def _gdn(qkvc, yb, alog_row, dtb_row, nw, s0, b, t):
    nc = min(t // GDN_CHUNK, GDN_CHUNKS_PER_STEP)
    ts = nc * GDN_CHUNK
    nt = t // ts
    st_spec = pl.BlockSpec((1, N_HEADS, HEAD, HEAD), lambda bi, c: (bi, 0, 0, 0))
    row_spec = pl.BlockSpec((1, HEAD), lambda bi, c: (0, 0))
    return pl.pallas_call(
        functools.partial(_gdn_kernel, nc=nc),
        grid=(b, nt),
        in_specs=[pl.BlockSpec((ts, B_CONV_CH), lambda bi, c: (bi * nt + c, 0)),
                  pl.BlockSpec((ts, BRANCH_W), lambda bi, c: (bi * nt + c, B_CONV_CH // BRANCH_W)),
                  pl.BlockSpec((ts, HEAD), lambda bi, c: (bi * nt + c, B_SMALL_COL // HEAD)),
                  row_spec, row_spec, row_spec, st_spec],
        out_specs=[pl.BlockSpec((ts, BRANCH_W), lambda bi, c: (bi * nt + c, 0)), st_spec],
        out_shape=[jax.ShapeDtypeStruct((b * t, BRANCH_W), BF16),
                   jax.ShapeDtypeStruct((b, N_HEADS, HEAD, HEAD), F32)],
        scratch_shapes=[pltpu.VMEM((N_HEADS, HEAD, HEAD), F32)],
        compiler_params=_params("parallel", "arbitrary"),
        name="gated_delta",
    )(qkvc, yb, yb, alog_row, dtb_row, nw, s0)


def _ret_kernel(lg_ref, q_ref, k_ref, v_ref, g_ref, cos_ref, sin_ref, nw_ref, s0_ref,
                o_ref, sout_ref, st_ref, *, cs):
    c = pl.program_id(1)

    @pl.when(c == 0)
    def _():
        st_ref[...] = s0_ref[0]

    heads = range(N_HEADS)
    lanes = [slice(h * HEAD, (h + 1) * HEAD) for h in heads]
    cos = cos_ref[...]
    sin = sin_ref[...]

    def rot(x):
        return x * cos + pltpu.roll(x, HEAD // 2, 1) * sin

    idx = lax.broadcasted_iota(jnp.int32, (cs, HEAD), 0).astype(F32)
    rel = (lax.broadcasted_iota(jnp.int32, (cs, cs), 0)
           - lax.broadcasted_iota(jnp.int32, (cs, cs), 1))
    relf = jnp.maximum(rel, 0).astype(F32)
    lg = [lg_ref[h] for h in heads]
    q = [rot(q_ref[:, lanes[h]]) for h in heads]
    k = [rot(k_ref[:, lanes[h]]) * (HEAD ** -0.5) for h in heads]
    att = [_bdot_nt(q[h], k[h]) * jnp.where(rel >= 0, jnp.exp(relf * lg[h]), 0.0) for h in heads]
    st = [st_ref[h] for h in heads]
    o = [_bdot(q[h], st[h]) * jnp.exp((idx + 1.0) * lg[h]) + _bdot(att[h], v_ref[:, lanes[h]]) for h in heads]
    for h in heads:
        c_dec = jnp.exp(jnp.zeros((1, HEAD), F32) + cs * lg[h])
        st_ref[h] = c_dec * st[h] + _bdot_tn(k[h] * jnp.exp((cs - 1.0 - idx) * lg[h]), v_ref[:, lanes[h]])
    for h in heads:
        mu = jnp.mean(o[h], axis=-1, keepdims=True)
        var = jnp.mean(jnp.square(o[h] - mu), axis=-1, keepdims=True)
        on = (o[h] - mu) * lax.rsqrt(var + NORM_EPS) * nw_ref[...]
        o_ref[:, lanes[h]] = (on * _silu(g_ref[:, lanes[h]])).astype(o_ref.dtype)

    @pl.when(c == pl.num_programs(1) - 1)
    def _():
        sout_ref[0] = st_ref[...]


def _retention(acg, lg, cos, sin, nw, s0, b, t):
    cs = min(t, 512)
    nt = t // cs
    col_spec = lambda part: pl.BlockSpec((cs, BRANCH_W), lambda bi, c, lg_ref: (bi * nt + c, 4 + part))
    tab_spec = pl.BlockSpec((cs, HEAD), lambda bi, c, lg_ref: (c, 0))
    st_spec = pl.BlockSpec((1, N_HEADS, HEAD, HEAD), lambda bi, c, lg_ref: (bi, 0, 0, 0))
    return pl.pallas_call(
        functools.partial(_ret_kernel, cs=cs),
        grid_spec=pltpu.PrefetchScalarGridSpec(
            num_scalar_prefetch=1,
            grid=(b, nt),
            in_specs=[col_spec(0), col_spec(1), col_spec(2), col_spec(3), tab_spec, tab_spec,
                      pl.BlockSpec((1, HEAD), lambda bi, c, lg_ref: (0, 0)), st_spec],
            out_specs=[pl.BlockSpec((cs, BRANCH_W), lambda bi, c, lg_ref: (bi * nt + c, 0)), st_spec],
            scratch_shapes=[pltpu.VMEM((N_HEADS, HEAD, HEAD), F32)]),
        out_shape=[jax.ShapeDtypeStruct((b * t, BRANCH_W), BF16),
                   jax.ShapeDtypeStruct((b, N_HEADS, HEAD, HEAD), F32)],
        compiler_params=_params("parallel", "arbitrary"),
        name="retention",
    )(lg, acg, acg, acg, acg, cos, sin, nw, s0)


def _merge_kernel(oa_ref, ob_ref, oc_ref, wb_ref, ga_ref, gb_ref, gc_ref, o_ref):
    acc = _sigmoid(ga_ref[...]) * jnp.dot(oa_ref[...], wb_ref[0], preferred_element_type=F32)
    acc += _sigmoid(gb_ref[...]) * jnp.dot(ob_ref[...], wb_ref[1], preferred_element_type=F32)
    acc += _sigmoid(gc_ref[...]) * jnp.dot(oc_ref[...], wb_ref[2], preferred_element_type=F32)
    o_ref[...] = acc.astype(o_ref.dtype)


def _merge(oa, ob, oc, wb, acg, tm):
    n = oa.shape[0]
    tn = 512
    o_spec = pl.BlockSpec((tm, BRANCH_W), lambda i, j: (i, 0))
    gate_spec = lambda br: pl.BlockSpec(
        (tm, tn), lambda i, j: (i, (GATE_COL0 + br * D_MODEL) // tn + j))
    return pl.pallas_call(
        _merge_kernel,
        grid=(n // tm, D_MODEL // tn),
        in_specs=[o_spec, o_spec, o_spec,
                  pl.BlockSpec((N_BRANCH, BRANCH_W, tn), lambda i, j: (0, 0, j)),
                  gate_spec(0), gate_spec(1), gate_spec(2)],
        out_specs=pl.BlockSpec((tm, tn), lambda i, j: (i, j)),
        out_shape=jax.ShapeDtypeStruct((n, D_MODEL), BF16),
        compiler_params=_params("parallel", "parallel"),
        name="branch_merge",
    )(oa, ob, oc, wb, acg, acg, acg)


def _layer_norm(y, g, b):
    mu = jnp.mean(y, axis=-1, keepdims=True)
    var = jnp.mean(jnp.square(y - mu), axis=-1, keepdims=True)
    return (y - mu) * lax.rsqrt(var + LN_EPS) * g + b


def _outproj_kernel(m_ref, x_ref, w_ref, g_ref, b_ref, wrh_ref, wrl_ref, xo_ref, xr_ref, xb_ref, sc_ref):
    h = jnp.dot(m_ref[...], w_ref[...], preferred_element_type=F32)
    x1 = _layer_norm(DN_ALPHA * x_ref[...] + h, g_ref[...], b_ref[...])
    xo_ref[...] = x1
    tm = x1.shape[0]
    for s_ in range(ROW_TILES):
        xr_ref[pl.ds(s_, tm, stride=ROW_TILES), :] = x1[:, s_ * HEAD:(s_ + 1) * HEAD]
    xh, xl = _split2(x1)
    xb_ref[...] = xh
    d = lambda a, b_: lax.dot_general(a, b_, (((1,), (1,)), ((), ())), preferred_element_type=F32)
    logits = d(wrh_ref[...], xh) + d(wrh_ref[...], xl) + d(wrl_ref[...], xh)
    sc_ref[...] = _sigmoid(logits)


def _outproj(merged, x, w_out, g, b, wr_hi, wr_lo, tm):
    n = x.shape[0]
    row = pl.BlockSpec((tm, D_MODEL), lambda i: (i, 0))
    vec = pl.BlockSpec((1, D_MODEL), lambda i: (0, 0))
    wr = pl.BlockSpec((N_EXPERTS, D_MODEL), lambda i: (0, 0))
    return pl.pallas_call(
        _outproj_kernel,
        grid=(n // tm,),
        in_specs=[row, row, pl.BlockSpec((D_MODEL, D_MODEL), lambda i: (0, 0)), vec, vec, wr, wr],
        out_specs=[row, pl.BlockSpec((tm * ROW_TILES, HEAD), lambda i: (i, 0)), row,
                   pl.BlockSpec((N_EXPERTS, tm), lambda i: (0, i))],
        out_shape=[jax.ShapeDtypeStruct((n, D_MODEL), F32), jax.ShapeDtypeStruct((n * ROW_TILES, HEAD), F32),
                   jax.ShapeDtypeStruct((n, D_MODEL), BF16), jax.ShapeDtypeStruct((N_EXPERTS, n), F32)],
        compiler_params=_params("parallel"),
        name="outproj_ln_router",
    )(merged, x, w_out, g, b, wr_hi, wr_lo)


def _route_kernel(sc_ref, bias_ref, e_ref, pos_ref, w_ref, cnt_ref, carry_ref, *, tm):
    i = pl.program_id(0)
    per_group = N_EXPERTS // N_GROUPS

    @pl.when(i == 0)
    def _():
        carry_ref[...] = jnp.zeros_like(carry_ref)

    s = sc_ref[...]
    c3 = (s + bias_ref[...]).reshape(N_GROUPS, per_group, tm)
    midx = lax.broadcasted_iota(jnp.int32, (N_GROUPS, per_group, tm), 1)
    gidx = lax.broadcasted_iota(jnp.int32, (N_GROUPS, per_group, tm), 0)

    def beats(other, mine, other_first):
        return jnp.where((other > mine) | ((other == mine) & other_first), 1, 0)

    r1 = jnp.zeros((N_GROUPS, per_group, tm), jnp.int32)
    for m in range(per_group):
        r1 = r1 + beats(c3[:, m:m + 1, :], c3, m < midx)
    gs = jnp.sum(jnp.where(r1 < 2, c3, 0.0), axis=1, keepdims=True)
    gs = jnp.broadcast_to(gs, (N_GROUPS, per_group, tm))
    r2 = jnp.zeros((N_GROUPS, per_group, tm), jnp.int32)
    for g in range(N_GROUPS):
        r2 = r2 + beats(gs[g:g + 1], gs, g < gidx)
    masked = jnp.where(r2 < TOPK_GROUPS, c3, -jnp.inf).reshape(N_EXPERTS, tm)
    eidx = lax.broadcasted_iota(jnp.int32, (N_EXPERTS, tm), 0)
    r3 = jnp.zeros((N_EXPERTS, tm), jnp.int32)
    for e in range(N_EXPERTS):
        r3 = r3 + beats(masked[e:e + 1, :], masked, e < eidx)
    sel = r3 < TOP_K
    w = jnp.where(sel, s, 0.0)
    w = w / jnp.sum(w, axis=0, keepdims=True) * ROUTE_SCALE
    sel_bf = jnp.where(sel, 1.0, 0.0).astype(BF16)
    tr = lax.broadcasted_iota(jnp.int32, (tm, tm), 0)
    tc_ = lax.broadcasted_iota(jnp.int32, (tm, tm), 1)
    before = jnp.where(tr < tc_, 1.0, 0.0).astype(BF16)
    carry = carry_ref[...]
    pos = (jnp.dot(sel_bf, before, preferred_element_type=F32)
           + jnp.concatenate([carry] * (tm // HEAD), axis=1))
    carry_ref[...] = carry + jnp.dot(sel_bf, jnp.ones((tm, HEAD), BF16), preferred_element_type=F32)
    er = lax.broadcasted_iota(jnp.int32, (N_EXPERTS, N_EXPERTS), 0)
    ec = lax.broadcasted_iota(jnp.int32, (N_EXPERTS, N_EXPERTS), 1)
    lower_e = jnp.where(ec < er, 1.0, 0.0).astype(BF16)
    jidx = jnp.dot(lower_e, sel_bf, preferred_element_type=F32)
    eidx_f = eidx.astype(F32)
    e_rows, p_rows, w_rows = [], [], []
    for j in range(TOP_K):
        oh = sel & (jidx == float(j))
        e_rows.append(jnp.sum(jnp.where(oh, eidx_f, 0.0), axis=0, keepdims=True))
        p_rows.append(jnp.sum(jnp.where(oh, pos, 0.0), axis=0, keepdims=True))
        w_rows.append(jnp.sum(jnp.where(oh, w, 0.0), axis=0, keepdims=True))
    zpad = [jnp.zeros((8 - TOP_K, tm), F32)]
    e_ref[...] = jnp.concatenate(e_rows + zpad, axis=0).astype(jnp.int32)
    pos_ref[...] = jnp.concatenate(p_rows + zpad, axis=0).astype(jnp.int32)
    w_ref[...] = jnp.concatenate(w_rows + zpad, axis=0)

    @pl.when(i == pl.num_programs(0) - 1)
    def _():
        cnt_ref[...] = carry_ref[...]


def _route(scores_t, bias_b, tm):
    n = scores_t.shape[1]
    blk = pl.BlockSpec((8, tm), lambda i: (0, i))
    return pl.pallas_call(
        functools.partial(_route_kernel, tm=tm),
        grid=(n // tm,),
        in_specs=[pl.BlockSpec((N_EXPERTS, tm), lambda i: (0, i)),
                  pl.BlockSpec((N_EXPERTS, tm), lambda i: (0, 0))],
        out_specs=[blk, blk, blk, pl.BlockSpec((N_EXPERTS, HEAD), lambda i: (0, 0))],
        out_shape=[jax.ShapeDtypeStruct((8, n), jnp.int32), jax.ShapeDtypeStruct((8, n), jnp.int32),
                   jax.ShapeDtypeStruct((8, n), F32), jax.ShapeDtypeStruct((N_EXPERTS, HEAD), F32)],
        scratch_shapes=[pltpu.VMEM((N_EXPERTS, HEAD), F32)],
        compiler_params=_params("arbitrary"),
        name="route_topk",
    )(scores_t, bias_b)


def _expert_kernel(be_ref, s_m3, s_m2, s_m1, s_cur, s_p1, s_p2, x_hbm, wg_ref, wu_ref, wd_ref,
                   out_hbm, xbuf, ybuf, sem_in, sem_out, *, n_tok):
    i = pl.program_id(0)
    last = pl.num_programs(0) - 1
    slot = i % EXPERT_BUFS
    nxt = (i + 2) % EXPERT_BUFS
    prv = nxt

    def buf_row(k):
        return k * ROW_PITCH if isinstance(k, int) else pl.multiple_of(k * ROW_PITCH, 8)

    def in_copy(slots_ref, buf, k):
        src = pl.multiple_of((slots_ref[0, 0, k] >> 3) * ROW_TILES, ROW_TILES)
        return pltpu.make_async_copy(x_hbm.at[pl.ds(src, ROW_TILES)],
                                     xbuf.at[buf, pl.ds(buf_row(k), ROW_TILES)], sem_in.at[buf, k])

    def out_copy(slots_ref, buf, k):
        v = slots_ref[0, 0, k]
        dst = pl.multiple_of(((v & 7) * n_tok + (v >> 3)) * ROW_TILES, ROW_TILES)
        return pltpu.make_async_copy(ybuf.at[buf, pl.ds(buf_row(k), ROW_TILES)],
                                     out_hbm.at[pl.ds(dst, ROW_TILES)], sem_out.at[buf, k])

    def rolled(fn):
        def body(k, carry):
            fn(k)
            return carry
        lax.fori_loop(0, MOE_BLOCK, body, 0, unroll=8)

    @pl.when(i == 0)
    def _():
        rolled(lambda k: in_copy(s_cur, 0, k).start())
        rolled(lambda k: in_copy(s_p1, 1, k).start())

    @pl.when(i < EXPERT_BUFS)
    def _():
        rolled(lambda k: in_copy(s_cur, slot, k).wait())

    @pl.when(i >= EXPERT_BUFS)
    def _():
        def both(k):
            in_copy(s_cur, slot, k).wait()
            out_copy(s_m3, slot, k).wait()
        rolled(both)

    def block(gather_ahead, scatter_prev):
        col = lambda ref, s_: ref.at[slot][pl.ds(s_, MOE_BLOCK, stride=ROW_PITCH), :]
        xb = jnp.concatenate([col(xbuf, s_) for s_ in range(ROW_TILES)], axis=1).astype(BF16)
        for k in range(MOE_BLOCK):
            if gather_ahead:
                in_copy(s_p2, nxt, k).start(priority=k % 2)
            if scatter_prev:
                out_copy(s_m1, prv, k).start(priority=k % 2)
        hg = jnp.dot(xb, wg_ref[0], preferred_element_type=F32)
        hu = jnp.dot(xb, wu_ref[0], preferred_element_type=F32)
        hb = (_silu(hg) * hu).astype(BF16)
        y = jnp.dot(hb, wd_ref[0], preferred_element_type=F32)
        for s_ in range(ROW_TILES):
            ybuf.at[slot][pl.ds(s_, MOE_BLOCK, stride=ROW_PITCH), :] = y[:, s_ * HEAD:(s_ + 1) * HEAD]

    @pl.when(i == 0)
    def _():
        block(True, False)

    @pl.when((i > 0) & (i < last - 1))
    def _():
        block(True, True)

    @pl.when(i == last - 1)
    def _():
        block(False, True)

    @pl.when(i == last)
    def _():
        block(False, True)
        rolled(lambda k: out_copy(s_m2, (i + 1) % EXPERT_BUFS, k).wait())
        rolled(lambda k: out_copy(s_cur, slot, k).start())
        rolled(lambda k: out_copy(s_m1, prv, k).wait())
        rolled(lambda k: out_copy(s_cur, slot, k).wait())


def _experts(x1r, block_e, slots, wg, wu, wd):
    n = x1r.shape[0] // ROW_TILES
    n_blocks = block_e.shape[0]
    slot_spec = lambda off: pl.BlockSpec(
        (1, 1, MOE_BLOCK), lambda i, be: (jnp.clip(i + off, 0, n_blocks - 1), 0, 0), memory_space=pltpu.SMEM)
    buf = pltpu.VMEM((EXPERT_BUFS, MOE_BLOCK * ROW_PITCH, HEAD), F32)
    sem = pltpu.SemaphoreType.DMA((EXPERT_BUFS, MOE_BLOCK))
    return pl.pallas_call(
        functools.partial(_expert_kernel, n_tok=n),
        grid_spec=pltpu.PrefetchScalarGridSpec(
            num_scalar_prefetch=1,
            grid=(n_blocks,),
            in_specs=[slot_spec(off) for off in (-3, -2, -1, 0, 1, 2)] + [
                pl.BlockSpec(memory_space=pl.ANY),
                pl.BlockSpec((1, D_MODEL, EXPERT_FF), lambda i, be: (be[i], 0, 0)),
                pl.BlockSpec((1, D_MODEL, EXPERT_FF), lambda i, be: (be[i], 0, 0)),
                pl.BlockSpec((1, EXPERT_FF, D_MODEL), lambda i, be: (be[i], 0, 0))],
            out_specs=pl.BlockSpec(memory_space=pl.ANY),
            scratch_shapes=[buf, buf, sem, sem]),
        out_shape=jax.ShapeDtypeStruct(((TOP_K * n + 2 * MOE_BLOCK) * ROW_TILES, HEAD), F32),
        compiler_params=_params("arbitrary"),
        name="routed_experts",
    )(block_e, *([slots] * 6), x1r, wg, wu, wd)


def _slot_tables(e6, pos6, counts):
    n = e6.shape[1]
    n_assign = n * TOP_K
    n_blocks = -(-n_assign // MOE_BLOCK) + N_EXPERTS
    padded = (counts + MOE_BLOCK - 1) // MOE_BLOCK * MOE_BLOCK
    pad_end = jnp.cumsum(padded)
    pad_start = pad_end - padded
    onehot = e6[:TOP_K, :, None] == jnp.arange(N_EXPERTS, dtype=jnp.int32)
    dest = jnp.sum(jnp.where(onehot, pad_start, 0), axis=-1) + pos6[:TOP_K]
    packed = (jnp.arange(n, dtype=jnp.int32)[None, :] * 8
              + jnp.arange(TOP_K, dtype=jnp.int32)[:, None])
    pad_slots = (jnp.arange(n_blocks * MOE_BLOCK, dtype=jnp.int32) % (2 * MOE_BLOCK)) * 8 + TOP_K
    slots = pad_slots.at[dest.reshape(-1)].set(packed.reshape(-1), unique_indices=True)
    blk_start = jnp.arange(n_blocks, dtype=jnp.int32) * MOE_BLOCK
    block_e = jnp.minimum(jnp.sum((pad_end[None, :] <= blk_start[:, None]).astype(jnp.int32), axis=1),
                          N_EXPERTS - 1)
    return block_e, slots.reshape(n_blocks, 1, MOE_BLOCK)


def _combine_kernel(*refs):
    r_refs = refs[:TOP_K]
    w6_ref, x_ref, xb_ref, wg_ref, wu_ref, wd_ref, g_ref, b_ref, xo_ref, xob_ref = refs[TOP_K:]
    xb = xb_ref[...]
    hb = (_silu(jnp.dot(xb, wg_ref[...], preferred_element_type=F32))
          * jnp.dot(xb, wu_ref[...], preferred_element_type=F32)).astype(BF16)
    shared = jnp.dot(hb, wd_ref[...], preferred_element_type=F32)
    w6 = w6_ref[...]
    tm = w6.shape[0]
    rows2d = lambda ref: jnp.concatenate(
        [ref[pl.ds(s_, tm, stride=ROW_TILES), :] for s_ in range(ROW_TILES)], axis=1)
    routed = rows2d(r_refs[0]) * w6[:, 0:1]
    for j in range(1, TOP_K):
        routed = routed + rows2d(r_refs[j]) * w6[:, j:j + 1]
    x2 = _layer_norm(DN_ALPHA * x_ref[...] + (routed + shared), g_ref[...], b_ref[...])
    xo_ref[...] = x2
    xob_ref[...] = x2.astype(BF16)


def _combine(routed, w6, x1, x1b, wg, wu, wd, g, b, tm):
    n = x1.shape[0]
    nt = n // tm
    row = pl.BlockSpec((tm, D_MODEL), lambda i: (i, 0))
    vec = pl.BlockSpec((1, D_MODEL), lambda i: (0, 0))
    w_in = pl.BlockSpec((D_MODEL, EXPERT_FF), lambda i: (0, 0))
    r_specs = [pl.BlockSpec((tm * ROW_TILES, HEAD), functools.partial(lambda i, j: (j * nt + i, 0), j=j))
               for j in range(TOP_K)]
    return pl.pallas_call(
        _combine_kernel,
        grid=(nt,),
        in_specs=r_specs + [pl.BlockSpec((tm, 8), lambda i: (i, 0)), row, row, w_in, w_in,
                            pl.BlockSpec((EXPERT_FF, D_MODEL), lambda i: (0, 0)), vec, vec],
        out_specs=[row, row],
        out_shape=[jax.ShapeDtypeStruct((n, D_MODEL), F32), jax.ShapeDtypeStruct((n, D_MODEL), BF16)],
        compiler_params=_params("parallel"),
        name="combine_shared_ln",
    )(*([routed] * TOP_K), w6, x1, x1b, wg, wu, wd, g, b)


def _ple_kernel(x_ref, xb_ref, p_ref, wg_ref, wp_ref, xo_ref, xob_ref):
    gate = _sigmoid(jnp.dot(xb_ref[...], wg_ref[...], preferred_element_type=F32))
    pe = jnp.dot(p_ref[...].astype(BF16), wp_ref[...], preferred_element_type=F32)
    x3 = x_ref[...] + pe * gate
    xo_ref[...] = x3
    xob_ref[...] = x3.astype(BF16)


def _ple(x2, x2b, p, wg, wp, tm):
    n = x2.shape[0]
    row = pl.BlockSpec((tm, D_MODEL), lambda i: (i, 0))
    return pl.pallas_call(
        _ple_kernel,
        grid=(n // tm,),
        in_specs=[row, row, pl.BlockSpec((tm, PLE_DIM), lambda i: (i, 0)),
                  pl.BlockSpec((D_MODEL, D_MODEL), lambda i: (0, 0)),
                  pl.BlockSpec((PLE_DIM, D_MODEL), lambda i: (0, 0))],
        out_specs=[row, row],
        out_shape=[jax.ShapeDtypeStruct((n, D_MODEL), F32), jax.ShapeDtypeStruct((n, D_MODEL), BF16)],
        compiler_params=_params("parallel"),
        name="ple",
    )(x2, x2b, p, wg, wp)


def _prep_weights(w_in, hgrn_lb_logits, gdn_a_log, gdn_dt_bias, w_branch, w_out, w_router,
                  w_exp_gate, w_exp_up, w_exp_down, w_sh_gate, w_sh_up, w_sh_down, w_ple_gate, w_ple_proj):
    pts = np.cumsum([1024, 1024, 1024, 1024, 512, 512, 1024, 1024, 8, 8, 1024, 1024, 1024, 1024]).tolist()
    a_end, b_end, c_end = pts[3], pts[9], pts[13]
    w_acg = jnp.concatenate([w_in[:, :, :a_end], w_in[:, :, b_end:c_end], w_in[:, :, c_end:]], axis=-1)
    w_b = jnp.concatenate([w_in[:, :, a_end:b_end],
                           jnp.zeros((DEPTH, D_MODEL, B_COLS - (b_end - a_end)), w_in.dtype)], axis=-1)
    lb_p = jax.nn.softmax(hgrn_lb_logits.astype(F32), axis=0)
    lb = jnp.maximum(jnp.cumsum(lb_p, axis=0) - lb_p[:1], 0.0)
    pad_row = lambda v: jnp.zeros((DEPTH, 1, HEAD), F32).at[:, 0, N_HEADS:2 * N_HEADS].set(v.astype(F32))
    wr = jnp.swapaxes(w_router.astype(F32), -1, -2)
    wr_hi = wr.astype(BF16)
    return dict(
        w_acg=w_acg.astype(BF16), w_b=w_b.astype(BF16),
        llb=jnp.log(lb)[:, None, :], l1m=jnp.log1p(-lb)[:, None, :], oml=(1.0 - lb)[:, None, :],
        alog_row=pad_row(gdn_a_log), dtb_row=pad_row(gdn_dt_bias),
        w_branch=w_branch.astype(BF16), w_out=w_out.astype(BF16),
        wr_hi=wr_hi, wr_lo=(wr - wr_hi.astype(F32)).astype(BF16),
        wg=w_exp_gate.astype(BF16), wu=w_exp_up.astype(BF16), wd=w_exp_down.astype(BF16),
        sg=w_sh_gate.astype(BF16), su=w_sh_up.astype(BF16), sd=w_sh_down.astype(BF16),
        pg=w_ple_gate.astype(BF16), pp=w_ple_proj.astype(BF16))


def _run_trunk(x, p, pos0, states, pw, small):
    (hgrn_norm, gdn_conv, gdn_norm, ret_norm, ln1_g, ln1_b, ln2_g, ln2_b, router_bias) = small
    s_hgrn, s_gdn, s_conv, s_ret = states
    b, t, d = x.shape
    n = b * t
    tm = min(n, 512)
    xf = x.reshape(n, d)
    xb = xf.astype(BF16)
    half = HEAD // 2
    inv = ROPE_BASE ** (-jnp.arange(half, dtype=F32) / half)
    ang = (pos0 + jnp.arange(t, dtype=F32))[:, None] * inv[None, :]
    cos_t = jnp.concatenate([jnp.cos(ang), jnp.cos(ang)], axis=-1)
    sin_t = jnp.concatenate([-jnp.sin(ang), jnp.sin(ang)], axis=-1)
    lg = jnp.log1p(-jnp.exp2(-5.0 - jnp.arange(N_HEADS, dtype=F32)))
    row = lambda v: v.reshape(1, -1).astype(F32)
    new = ([], [], [], [])
    for i in range(DEPTH):
        acg = _matmul(xb, pw['w_acg'][i], min(n, 2048), 1024)
        yb = _matmul(xb, pw['w_b'][i], min(n, 2048), 640)
        o_a, st_a = _hgrn(acg, pw['llb'][i], pw['l1m'][i], pw['oml'][i], row(hgrn_norm[i]),
                          jnp.swapaxes(s_hgrn[i], -1, -2), b, t)
        buf8 = jnp.concatenate([jnp.zeros((b, 8 - (CONV_W - 1), B_CONV_CH), F32), s_conv[i]], axis=1)
        qkvc = _gdn_conv(yb, gdn_conv[i], buf8, b, t)
        o_b, st_b = _gdn(qkvc, yb, pw['alog_row'][i], pw['dtb_row'][i], row(gdn_norm[i]), s_gdn[i], b, t)
        o_c, st_c = _retention(acg, lg, cos_t, sin_t, row(ret_norm[i]), s_ret[i], b, t)
        merged = _merge(o_a, o_b, o_c, pw['w_branch'][i], acg, min(n, 1024))
        x1, x1r, x1b, scores = _outproj(merged, xf, pw['w_out'][i], row(ln1_g[i]), row(ln1_b[i]),
                                        pw['wr_hi'][i], pw['wr_lo'][i], tm)
        bias_b = jnp.broadcast_to(router_bias[i].astype(F32)[:, None], (N_EXPERTS, tm))
        e6, pos6, w6, cnt = _route(scores, bias_b, tm)
        block_e, slots = _slot_tables(e6, pos6, cnt[:, 0].astype(jnp.int32))
        routed = _experts(x1r, block_e, slots, pw['wg'][i], pw['wu'][i], pw['wd'][i])
        x2, x2b = _combine(routed, w6.T, x1, x1b, pw['sg'][i], pw['su'][i],
                           pw['sd'][i], row(ln2_g[i]), row(ln2_b[i]), min(n, 256))
        xf, xb = _ple(x2, x2b, p[i].reshape(n, PLE_DIM), pw['pg'][i], pw['pp'][i], tm)
        new[0].append(jnp.swapaxes(st_a, -1, -2))
        new[1].append(st_b)
        new[2].append(yb.reshape(b, t, B_COLS)[:, t - (CONV_W - 1):, :B_CONV_CH])
        new[3].append(st_c)
    return xf.reshape(b, t, d), tuple(jnp.stack(a) for a in new)


def kernel(x_prompt, x_sample, p_prompt, p_sample, state_hgrn, state_gdn, state_gdn_conv, state_ret,
           w_in, hgrn_lb_logits, hgrn_norm, gdn_conv, gdn_a_log, gdn_dt_bias, gdn_norm, ret_norm,
           w_branch, w_out, ln1_g, ln1_b, ln2_g, ln2_b, w_router, router_bias,
           w_exp_gate, w_exp_up, w_exp_down, w_sh_gate, w_sh_up, w_sh_down, w_ple_gate, w_ple_proj):
    pw = _prep_weights(w_in, hgrn_lb_logits, gdn_a_log, gdn_dt_bias, w_branch, w_out, w_router,
                       w_exp_gate, w_exp_up, w_exp_down, w_sh_gate, w_sh_up, w_sh_down,
                       w_ple_gate, w_ple_proj)
    small = (hgrn_norm, gdn_conv, gdn_norm, ret_norm, ln1_g, ln1_b, ln2_g, ln2_b, router_bias)
    bp = x_prompt.shape[0]
    zero_states = (jnp.zeros((DEPTH, bp, N_HEADS, HEAD, HEAD), F32),
                   jnp.zeros((DEPTH, bp, N_HEADS, HEAD, HEAD), F32),
                   jnp.zeros((DEPTH, bp, CONV_W - 1, B_CONV_CH), F32),
                   jnp.zeros((DEPTH, bp, N_HEADS, HEAD, HEAD), F32))
    y_p, (hp, gp, cp, rp) = _run_trunk(x_prompt, p_prompt, 0.0, zero_states, pw, small)
    y_s, (hs, gs, cs, rs) = _run_trunk(x_sample, p_sample, float(PAST_LEN),
                                       (state_hgrn, state_gdn, state_gdn_conv, state_ret), pw, small)
    return (y_p, y_s, hp, gp, cp, rp, hs, gs, cs, rs)
```
